```python
import math
import jax, jax.numpy as jnp
from jax import lax
import numpy as np

D_MODEL = 1024
BATCH = 16
SEQ = 4096
DEPTH = 4
DEC_BATCH = 32
DEC_SEQ = 16
PAST_LEN = 1024

CHUNK = 64
N_A_LAYERS = DEPTH // 2
N_B_LAYERS = DEPTH - N_A_LAYERS
EXPAND = 2
D_INNER = EXPAND * D_MODEL
HEAD_DIM_A = 64
N_HEADS_A = D_INNER // HEAD_DIM_A
D_STATE = 128
N_GROUPS = 4
CONV_W = 4
CONV_DIM = D_INNER + 2 * N_GROUPS * D_STATE
IN_DIM = D_INNER + CONV_DIM + N_HEADS_A
HEAD_DIM_B = 64
N_HEADS_B = D_MODEL // HEAD_DIM_B
Q_BLOCK = 128
N_EXPERTS = 32
TOP_K = 4
D_FF_EXPERT = D_MODEL
SWIGLU_ALPHA = 1.702
SWIGLU_LIMIT = 7.0
MOE_BLOCK = 128
NORM_EPS = 1e-6

kernel_name = 'yoco_ssd_stickbreak_moe_stream_step'


def rmsnorm(x, w):
    xf = x.astype(jnp.float32)
    y = xf * lax.rsqrt(jnp.mean(xf * xf, axis=-1, keepdims=True) + NORM_EPS)
    return (y * w.astype(jnp.float32)).astype(x.dtype)


def modulate(x, w, shift, scale):
    return rmsnorm(x, w) * (1 + scale) + shift


def segsum(a):
    t = a.shape[-1]
    rep = jnp.broadcast_to(a[..., :, None], a.shape + (t,))
    strict = jnp.tril(jnp.ones((t, t), dtype=bool), -1)
    ss = jnp.cumsum(jnp.where(strict, rep, 0.0), axis=-2)
    return jnp.where(jnp.tril(jnp.ones((t, t), dtype=bool)), ss, -jnp.inf)


def ssd_scan(x, dt, a_head, bm, cm, h0, blk):
    b, L, H, P = x.shape
    G, N = bm.shape[2], bm.shape[3]
    R = H // G
    nc = L // blk
    xd = (x * dt[..., None]).reshape(b, nc, blk, G, R, P)
    a = jnp.moveaxis((dt * a_head).reshape(b, nc, blk, G, R), 2, -1)
    a_cs = jnp.cumsum(a, axis=-1)
    bc = bm.reshape(b, nc, blk, G, N)
    cc = cm.reshape(b, nc, blk, G, N)
    decay_in = jnp.exp(segsum(a))
    cb = jnp.einsum('bclgn,bcsgn->bcgls', cc, bc)
    y_diag = jnp.einsum('bcgrls,bcsgrp->bclgrp', cb[:, :, :, None] * decay_in, xd)
    decay_to_end = jnp.moveaxis(jnp.exp(a_cs[..., -1:] - a_cs), -1, 2)[..., None]
    chunk_states = jnp.einsum('bclgn,bclgrp->bcgrpn', bc, xd * decay_to_end)
    chunk_decay = jnp.exp(a_cs[..., -1])

    def step(h, inp):
        s, d = inp
        return h * d[..., None, None] + s, h

    h_final, h_enter = lax.scan(step, h0.reshape(b, G, R, P, N),
                                (jnp.moveaxis(chunk_states, 1, 0), jnp.moveaxis(chunk_decay, 1, 0)))
    y_off = jnp.einsum('bclgn,cbgrpn->bclgrp', cc, h_enter) * jnp.moveaxis(jnp.exp(a_cs), -1, 2)[..., None]
    y = (y_diag + y_off).reshape(b, L, H, P)
    return y, h_final.reshape(b, H, P, N)


def mamba2_mixer(h, conv_state, ssm_state, in_proj, conv_w, conv_b, dt_bias, a_log, d_skip, norm_w, out_proj):
    b, L, _ = h.shape
    f32 = jnp.float32
    zxbcdt = h @ in_proj
    z, xbc, dt = jnp.split(zxbcdt, [D_INNER, D_INNER + CONV_DIM], axis=-1)
    xbc_ext = jnp.concatenate([conv_state.astype(xbc.dtype), xbc], axis=1)
    new_conv = xbc_ext[:, -(CONV_W - 1):]
    acc = conv_b
    for tap in range(CONV_W):
        acc = acc + xbc_ext[:, tap:tap + L] * conv_w[tap]
    xbc = jax.nn.silu(acc)
    xs, bm, cm = jnp.split(xbc, [D_INNER, D_INNER + N_GROUPS * D_STATE], axis=-1)
    xs = xs.reshape(b, L, N_HEADS_A, HEAD_DIM_A).astype(f32)
    bm = bm.reshape(b, L, N_GROUPS, D_STATE).astype(f32)
    cm = cm.reshape(b, L, N_GROUPS, D_STATE).astype(f32)
    dt = jax.nn.softplus(dt.astype(f32) + dt_bias.astype(f32))
    a_head = -jnp.exp(a_log.astype(f32))
    y, new_ssm = ssd_scan(xs, dt, a_head, bm, cm, ssm_state.astype(f32), min(CHUNK, L))
    y = y + xs * d_skip.astype(f32)[:, None]
    g = y.reshape(b, L, D_INNER) * jax.nn.silu(z.astype(f32))
    g = g.reshape(b, L, N_GROUPS, D_INNER // N_GROUPS)
    g = g * lax.rsqrt(jnp.mean(g * g, axis=-1, keepdims=True) + NORM_EPS)
    g = g.reshape(b, L, D_INNER) * norm_w.astype(f32)
    out = g.astype(h.dtype) @ out_proj
    return out, new_conv, new_ssm.astype(ssm_state.dtype)


def stick_breaking_attention(q, k, v, q_pos, k_pos):
    b, L, H, Dh = q.shape
    blk = min(Q_BLOCK, L)
    nb = L // blk
    qb = jnp.moveaxis(q.reshape(b, nb, blk, H, Dh), 1, 0)
    pb = q_pos.reshape(nb, blk)
    scale = 1.0 / math.sqrt(Dh)

    def one_block(args):
        qi, pi = args
        z = jnp.einsum('bqhd,bkhd->bhqk', qi, k).astype(jnp.float32) * scale
        vis = k_pos[None, :] < pi[:, None]
        log_keep = jnp.where(vis, jax.nn.log_sigmoid(-z), 0.0)
        log_after = lax.cumsum(log_keep, axis=3, reverse=True) - log_keep
        w = jnp.where(vis, jnp.exp(jax.nn.log_sigmoid(z) + log_after), 0.0)
        return jnp.einsum('bhqk,bkhd->bqhd', w.astype(v.dtype), v)

    o = lax.map(one_block, (qb, pb))
    return jnp.moveaxis(o, 0, 1).reshape(b, L, H * Dh)


def clamped_swiglu(hid):
    glu, lin = jnp.split(hid, 2, axis=-1)
    glu = jnp.minimum(glu, SWIGLU_LIMIT)
    lin = jnp.clip(lin, -SWIGLU_LIMIT, SWIGLU_LIMIT)
    return glu * jax.nn.sigmoid(SWIGLU_ALPHA * glu) * (lin + 1)


def moe_ffn(h, router_w, router_b, w1, b1, w2, b2):
    T = h.shape[0]
    logits = (h @ router_w + router_b).astype(jnp.float32)
    top_logit, top_e = lax.top_k(logits, TOP_K)
    gate = jax.nn.softmax(top_logit, axis=-1).astype(h.dtype)
    n_assign = T * TOP_K
    flat_e = top_e.reshape(-1)
    order = jnp.argsort(flat_e)
    sorted_e = flat_e[order]
    src_token = order // TOP_K
    counts = jnp.bincount(flat_e, length=N_EXPERTS)
    padded = (counts + MOE_BLOCK - 1) // MOE_BLOCK * MOE_BLOCK
    start = jnp.cumsum(counts) - counts
    pend = jnp.cumsum(padded)
    pstart = pend - padded
    dest = pstart[sorted_e] + (jnp.arange(n_assign) - start[sorted_e])
    n_blocks = (n_assign + N_EXPERTS * (MOE_BLOCK - 1) + MOE_BLOCK - 1) // MOE_BLOCK
    n_rows = n_blocks * MOE_BLOCK
    row_token = jnp.full((n_rows,), T, dtype=jnp.int32).at[dest].set(src_token.astype(jnp.int32))
    h_ext = jnp.concatenate([h, jnp.zeros((1, h.shape[1]), h.dtype)], axis=0)
    xs = h_ext[row_token].reshape(n_blocks, MOE_BLOCK, h.shape[1])
    block_e = jnp.minimum(jnp.searchsorted(pend, jnp.arange(n_blocks) * MOE_BLOCK, side='right'), N_EXPERTS - 1)

    def run_block(args):
        xb, e = args
        return clamped_swiglu(xb @ w1[e] + b1[e]) @ w2[e] + b2[e]

    ys = lax.map(run_block, (xs, block_e)).reshape(n_rows, h.shape[1])
    y_assign = ys[dest] * gate.reshape(-1)[order][:, None]
    return jnp.zeros_like(h).at[src_token].add(y_assign)


def trunk(x, c, conv_state, ssm_state, past_k, past_v, p):
    b, L, _ = x.shape
    pos0 = past_k.shape[1]
    q_pos = pos0 + jnp.arange(L)
    k_pos = jnp.arange(pos0 + L)
    cs = jax.nn.silu(c)
    conv_out, ssm_out = [], []
    k_all = v_all = k_new = v_new = None
    for layer in range(DEPTH):
        mod = (cs @ p['mod_w'][layer] + p['mod_b'][layer])[:, None, :]
        sh1, sc1, g1, sh2, sc2, g2 = jnp.split(mod, 6, axis=-1)
        h = modulate(x, p['mix_norm_w'][layer], sh1, sc1)
        if layer < N_A_LAYERS:
            i = layer
            out, cst, sst = mamba2_mixer(h, conv_state[i], ssm_state[i], p['a_in_proj'][i], p['a_conv_w'][i],
                                         p['a_conv_b'][i], p['a_dt_bias'][i], p['a_A_log'][i], p['a_D'][i],
                                         p['a_norm_w'][i], p['a_out_proj'][i])
            conv_out.append(cst)
            ssm_out.append(sst)
        else:
            j = layer - N_A_LAYERS
            q = (h @ p['b_w_q'][j]).reshape(b, L, N_HEADS_B, HEAD_DIM_B)
            out = stick_breaking_attention(q, k_all, v_all, q_pos, k_pos) @ p['b_w_o'][j]
        x = x + g1 * out
        h = modulate(x, p['ffn_norm_w'][layer], sh2, sc2)
        ff = moe_ffn(h.reshape(b * L, D_MODEL), p['router_w'][layer], p['router_b'][layer], p['moe_w1'][layer],
                     p['moe_b1'][layer], p['moe_w2'][layer], p['moe_b2'][layer])
        x = x + g2 * ff.reshape(b, L, D_MODEL)
        if layer == N_A_LAYERS - 1:
            mkv = (cs @ p['kv_mod_w'] + p['kv_mod_b'])[:, None, :]
            sh_kv, sc_kv = jnp.split(mkv, 2, axis=-1)
            kv = modulate(x, p['kv_norm_w'], sh_kv, sc_kv) @ p['w_kv']
            k_new, v_new = jnp.split(kv, 2, axis=-1)
            k_new = k_new.reshape(b, L, N_HEADS_B, HEAD_DIM_B)
            v_new = v_new.reshape(b, L, N_HEADS_B, HEAD_DIM_B)
            k_all = jnp.concatenate([past_k.astype(k_new.dtype), k_new], axis=1)
            v_all = jnp.concatenate([past_v.astype(v_new.dtype), v_new], axis=1)
    y = rmsnorm(x, p['final_norm_w'])
    return y, jnp.stack(conv_out), jnp.stack(ssm_out), k_new, v_new


def setup_inputs(seed: int = 0) -> dict:
    key = jax.random.key(seed)
    ks = jax.random.split(key, 40)
    f32 = jnp.float32

    def nrm(k, shape, s):
        return jax.random.normal(k, shape, f32) * s

    dt0 = jnp.exp(jax.random.uniform(ks[33], (N_A_LAYERS, N_HEADS_A), f32)
                  * (math.log(0.1) - math.log(0.001)) + math.log(0.001))
    hb = N_HEADS_B * HEAD_DIM_B
    return {
        'x_prompt': nrm(ks[0], (BATCH, SEQ, D_MODEL), 1.0),
        'x_sample': nrm(ks[1], (DEC_BATCH, DEC_SEQ, D_MODEL), 1.0),
        'state_conv': nrm(ks[2], (N_A_LAYERS, DEC_BATCH, CONV_W - 1, CONV_DIM), 1.0),
        'state_ssm': nrm(ks[3], (N_A_LAYERS, DEC_BATCH, N_HEADS_A, HEAD_DIM_A, D_STATE), 0.1),
        'cache_k': nrm(ks[4], (DEC_BATCH, PAST_LEN, N_HEADS_B, HEAD_DIM_B), 1.0),
        'cache_v': nrm(ks[5], (DEC_BATCH, PAST_LEN, N_HEADS_B, HEAD_DIM_B), 1.0),
        'c_prompt': nrm(ks[6], (BATCH, D_MODEL), 1.0),
        'c_sample': nrm(ks[7], (DEC_BATCH, D_MODEL), 1.0),
        'mod_w': nrm(ks[8], (DEPTH, D_MODEL, 6 * D_MODEL), 0.5 * D_MODEL ** -0.5),
        'mod_b': nrm(ks[9], (DEPTH, 6 * D_MODEL), 0.02),
        'mix_norm_w': 1.0 + nrm(ks[10], (DEPTH, D_MODEL), 0.02),
        'ffn_norm_w': 1.0 + nrm(ks[11], (DEPTH, D_MODEL), 0.02),
        'a_in_proj': nrm(ks[12], (N_A_LAYERS, D_MODEL, IN_DIM), D_MODEL ** -0.5),
        'a_conv_w': nrm(ks[13], (N_A_LAYERS, CONV_W, CONV_DIM), CONV_W ** -0.5),
        'a_conv_b': nrm(ks[14], (N_A_LAYERS, CONV_DIM), 0.02),
        'a_dt_bias': dt0 + jnp.log(-jnp.expm1(-dt0)),
        'a_A_log': jnp.log(jax.random.uniform(ks[15], (N_A_LAYERS, N_HEADS_A), f32, 1.0, 16.0)),
        'a_D': 1.0 + nrm(ks[16], (N_A_LAYERS, N_HEADS_A), 0.1),
        'a_norm_w': 1.0 + nrm(ks[17], (N_A_LAYERS, D_INNER), 0.02),
        'a_out_proj': nrm(ks[18], (N_A_LAYERS, D_INNER, D_MODEL), D_INNER ** -0.5),
        'kv_mod_w': nrm(ks[19], (D_MODEL, 2 * D_MODEL), 0.5 * D_MODEL ** -0.5),
        'kv_mod_b': nrm(ks[20], (2 * D_MODEL,), 0.02),
        'kv_norm_w': 1.0 + nrm(ks[21], (D_MODEL,), 0.02),
        'w_kv': nrm(ks[22], (D_MODEL, 2 * hb), D_MODEL ** -0.5),
        'b_w_q': nrm(ks[23], (N_B_LAYERS, D_MODEL, hb), D_MODEL ** -0.5),
        'b_w_o': nrm(ks[24], (N_B_LAYERS, hb, D_MODEL), hb ** -0.5),
        'router_w': nrm(ks[25], (DEPTH, D_MODEL, N_EXPERTS), D_MODEL ** -0.5),
        'router_b': nrm(ks[26], (DEPTH, N_EXPERTS), 0.01),
        'moe_w1': nrm(ks[27], (DEPTH, N_EXPERTS, D_MODEL, 2 * D_FF_EXPERT), D_MODEL ** -0.5),
        'moe_b1': nrm(ks[28], (DEPTH, N_EXPERTS, 2 * D_FF_EXPERT), 0.01),
        'moe_w2': nrm(ks[29], (DEPTH, N_EXPERTS, D_FF_EXPERT, D_MODEL), D_FF_EXPERT ** -0.5),
        'moe_b2': nrm(ks[30], (DEPTH, N_EXPERTS, D_MODEL), 0.01),
        'final_norm_w': 1.0 + nrm(ks[31], (D_MODEL,), 0.02),
    }


def reference(x_prompt, x_sample, state_conv, state_ssm, cache_k, cache_v, c_prompt, c_sample,
              mod_w, mod_b, mix_norm_w, ffn_norm_w, a_in_proj, a_conv_w, a_conv_b, a_dt_bias, a_A_log, a_D,
              a_norm_w, a_out_proj, kv_mod_w, kv_mod_b, kv_norm_w, w_kv, b_w_q, b_w_o, router_w, router_b,
              moe_w1, moe_b1, moe_w2, moe_b2, final_norm_w):
    p = dict(mod_w=mod_w, mod_b=mod_b, mix_norm_w=mix_norm_w, ffn_norm_w=ffn_norm_w, a_in_proj=a_in_proj,
             a_conv_w=a_conv_w, a_conv_b=a_conv_b, a_dt_bias=a_dt_bias, a_A_log=a_A_log, a_D=a_D,
             a_norm_w=a_norm_w, a_out_proj=a_out_proj, kv_mod_w=kv_mod_w, kv_mod_b=kv_mod_b,
             kv_norm_w=kv_norm_w, w_kv=w_kv, b_w_q=b_w_q, b_w_o=b_w_o, router_w=router_w, router_b=router_b,
             moe_w1=moe_w1, moe_b1=moe_b1, moe_w2=moe_w2, moe_b2=moe_b2, final_norm_w=final_norm_w)
    bp = x_prompt.shape[0]
    conv0 = jnp.zeros((N_A_LAYERS, bp, CONV_W - 1, CONV_DIM), x_prompt.dtype)
    ssm0 = jnp.zeros((N_A_LAYERS, bp, N_HEADS_A, HEAD_DIM_A, D_STATE), jnp.float32)
    kv0 = jnp.zeros((bp, 0, N_HEADS_B, HEAD_DIM_B), x_prompt.dtype)
    y_prompt, conv_prompt, ssm_prompt, k_prompt, v_prompt = trunk(x_prompt, c_prompt, conv0, ssm0, kv0, kv0, p)
    y_sample, conv_sample, ssm_sample, k_sample, v_sample = trunk(x_sample, c_sample, state_conv, state_ssm,
                                                                  cache_k, cache_v, p)
    return (y_prompt, y_sample, conv_prompt, ssm_prompt, k_prompt, v_prompt,
            conv_sample, ssm_sample, k_sample, v_sample)
```

```python
import functools
import math

import jax
import jax.numpy as jnp
from jax import lax
from jax.experimental import pallas as pl
from jax.experimental.pallas import tpu as pltpu

F32 = jnp.float32
BF16 = jnp.bfloat16
HIGHEST = lax.Precision.HIGHEST

D_MODEL = 1024
DEPTH = 4
N_A_LAYERS = 2
D_INNER = 2048
HEAD_DIM_A = 64
N_HEADS_A = 32
D_STATE = 128
N_GROUPS = 4
GROUP_W = D_INNER // N_GROUPS
CONV_W = 4
CONV_DIM = D_INNER + 2 * N_GROUPS * D_STATE
HEAD_DIM_B = 64
N_HEADS_B = 16
N_EXPERTS = 32
TOP_K = 4
D_FF = 1024
SWIGLU_ALPHA = 1.702
SWIGLU_LIMIT = 7.0
NORM_EPS = 1e-6

LANE = 128
SSD_Q = 64
TOKEN_TILE = 512
MOE_TILE = 512
ATTN_TK = 256
VMEM_LIMIT = 56 * 1024 * 1024


def _cparams(sem):
    return pltpu.CompilerParams(dimension_semantics=sem, vmem_limit_bytes=VMEM_LIMIT)


def _softplus(x):
    return jnp.maximum(x, 0.0) + jnp.log1p(jnp.exp(-jnp.abs(x)))


def _silu(x):
    return x * jax.nn.sigmoid(x)


def _norm_mod(x, nw, sh, sc):
    ms = jnp.mean(x * x, axis=-1, keepdims=True)
    return (x * lax.rsqrt(ms + NORM_EPS) * nw) * (1.0 + sc) + sh


def _mod_spec(mod3, tile, rows_per_group):
    steps = max(rows_per_group // tile, 1)
    g = mod3.shape[1]
    return pl.BlockSpec((1, g, mod3.shape[2]), lambda i: (i // steps, 0, 0))


def _mod_kernel(c_ref, w_ref, b_ref, o_ref):
    cs = _silu(c_ref[...])
    o_ref[0] = jnp.dot(cs, w_ref[0], precision=HIGHEST, preferred_element_type=F32) + b_ref[0]


def mod_vectors(c, w, b):
    nl, d, n = w.shape
    tn = 1024
    return pl.pallas_call(
        _mod_kernel,
        grid=(nl, n // tn),
        in_specs=[pl.BlockSpec(c.shape, lambda l, j: (0, 0)),
                  pl.BlockSpec((1, d, tn), lambda l, j: (l, 0, j)),
                  pl.BlockSpec((1, 1, tn), lambda l, j: (l, 0, j))],
        out_specs=pl.BlockSpec((1, c.shape[0], tn), lambda l, j: (l, 0, j)),
        out_shape=jax.ShapeDtypeStruct((nl, c.shape[0], n), F32),
        compiler_params=_cparams(("arbitrary", "arbitrary")),
    )(c, w, b)


def _nmm_kernel(x_ref, nw_ref, sh_ref, sc_ref, *refs, n_w):
    h = _norm_mod(x_ref[...], nw_ref[...], sh_ref[0], sc_ref[0]).astype(BF16)
    for w_ref, o_ref in zip(refs[:n_w], refs[n_w:]):
        o_ref[...] = jnp.dot(h, w_ref[...], preferred_element_type=F32).astype(o_ref.dtype)


def norm_mod_matmul(x, nw, sh3, sc3, ws, out_dtypes, rows_per_group):
    t, d = x.shape
    tm = min(TOKEN_TILE, t)
    in_specs = [pl.BlockSpec((tm, d), lambda i: (i, 0)),
                pl.BlockSpec((1, d), lambda i: (0, 0)),
                _mod_spec(sh3, tm, rows_per_group), _mod_spec(sc3, tm, rows_per_group)]
    in_specs += [pl.BlockSpec(w.shape, lambda i: (0, 0), pipeline_mode=pl.Buffered(1)) for w in ws]
    return pl.pallas_call(
        functools.partial(_nmm_kernel, n_w=len(ws)),
        grid=(t // tm,),
        in_specs=in_specs,
        out_specs=[pl.BlockSpec((tm, w.shape[1]), lambda i: (i, 0)) for w in ws],
        out_shape=[jax.ShapeDtypeStruct((t, w.shape[1]), dt) for w, dt in zip(ws, out_dtypes)],
        compiler_params=_cparams(("parallel",)),
    )(x, nw, sh3, sc3, *ws)


def _mmres_kernel(a_ref, w_ref, x_ref, g_ref, o_ref):
    acc = jnp.dot(a_ref[...], w_ref[...], preferred_element_type=F32)
    o_ref[...] = x_ref[...] + g_ref[0] * acc


def matmul_residual(a, w, x, g3, rows_per_group):
    t, d = x.shape
    tm = min(TOKEN_TILE, t)
    return pl.pallas_call(
        _mmres_kernel,
        grid=(t // tm,),
        in_specs=[pl.BlockSpec((tm, a.shape[1]), lambda i: (i, 0)),
                  pl.BlockSpec(w.shape, lambda i: (0, 0), pipeline_mode=pl.Buffered(1)),
                  pl.BlockSpec((tm, d), lambda i: (i, 0)),
                  _mod_spec(g3, tm, rows_per_group)],
        out_specs=pl.BlockSpec((tm, d), lambda i: (i, 0)),
        out_shape=jax.ShapeDtypeStruct((t, d), F32),
        compiler_params=_cparams(("parallel",)),
    )(a, w, x, g3)


def _rms_kernel(x_ref, nw_ref, o_ref):
    x = x_ref[...]
    ms = jnp.mean(x * x, axis=-1, keepdims=True)
    o_ref[...] = x * lax.rsqrt(ms + NORM_EPS) * nw_ref[...]


def final_rmsnorm(x, nw):
    t, d = x.shape
    tm = min(TOKEN_TILE, t)
    return pl.pallas_call(
        _rms_kernel,
        grid=(t // tm,),
        in_specs=[pl.BlockSpec((tm, d), lambda i: (i, 0)), pl.BlockSpec((1, d), lambda i: (0, 0))],
        out_specs=pl.BlockSpec((tm, d), lambda i: (i, 0)),
        out_shape=jax.ShapeDtypeStruct((t, d), F32),
        compiler_params=_cparams(("parallel",)),
    )(x, nw)


def _router_kernel(x_ref, nw_ref, sh_ref, sc_ref, rwt_ref, rb_ref, h_ref, e_ref, g_ref):
    h = _norm_mod(x_ref[...], nw_ref[...], sh_ref[0], sc_ref[0])
    h_ref[...] = h.astype(BF16)
    logits = lax.dot_general(rwt_ref[...], h, (((1,), (1,)), ((), ())),
                             precision=HIGHEST, preferred_element_type=F32) + rb_ref[...]
    eid = lax.broadcasted_iota(jnp.int32, logits.shape, 0)
    vals, idxs = [], []
    for _ in range(TOP_K):
        m = jnp.max(logits, axis=0, keepdims=True)
        idx = jnp.min(jnp.where(logits == m, eid, N_EXPERTS), axis=0, keepdims=True)
        vals.append(m)
        idxs.append(idx)
        logits = jnp.where(eid == idx, -jnp.inf, logits)
    ex = [jnp.exp(v - vals[0]) for v in vals]
    den = ex[0] + ex[1] + ex[2] + ex[3]
    for k in range(TOP_K):
        e_ref[k:k + 1, :] = idxs[k]
        g_ref[k:k + 1, :] = ex[k] / den


def moe_router(x, nw, sh3, sc3, rwt, rb, rows_per_group):
    t, d = x.shape
    tm = min(TOKEN_TILE, t)
    return pl.pallas_call(
        _router_kernel,
        grid=(t // tm,),
        in_specs=[pl.BlockSpec((tm, d), lambda i: (i, 0)),
                  pl.BlockSpec((1, d), lambda i: (0, 0)),
                  _mod_spec(sh3, tm, rows_per_group), _mod_spec(sc3, tm, rows_per_group),
                  pl.BlockSpec(rwt.shape, lambda i: (0, 0)),
                  pl.BlockSpec(rb.shape, lambda i: (0, 0))],
        out_specs=[pl.BlockSpec((tm, d), lambda i: (i, 0)),
                   pl.BlockSpec((TOP_K, tm), lambda i: (0, i)),
                   pl.BlockSpec((TOP_K, tm), lambda i: (0, i))],
        out_shape=[jax.ShapeDtypeStruct((t, d), BF16),
                   jax.ShapeDtypeStruct((TOP_K, t), jnp.int32),
                   jax.ShapeDtypeStruct((TOP_K, t), F32)],
        compiler_params=_cparams(("parallel",)),
    )(x, nw, sh3, sc3, rwt, rb)


def _ffn_kernel(be_ref, nu_ref, xs_ref, w1_ref, b1_ref, w2_ref, b2_ref, o_ref):
    @pl.when(pl.program_id(0) < nu_ref[0])
    def _():
        hid = jnp.dot(xs_ref[...], w1_ref[0], preferred_element_type=F32) + b1_ref[0]
        glu = jnp.minimum(hid[:, :D_FF], SWIGLU_LIMIT)
        lin = jnp.clip(hid[:, D_FF:], -SWIGLU_LIMIT, SWIGLU_LIMIT)
        act = glu * jax.nn.sigmoid(SWIGLU_ALPHA * glu) * (lin + 1.0)
        out = jnp.dot(act.astype(BF16), w2_ref[0], preferred_element_type=F32) + b2_ref[0]
        o_ref[...] = out.astype(o_ref.dtype)


def moe_ffn_blocks(xs, block_e, n_used, w1, b1, w2, b2):
    n_rows, d = xs.shape
    tm = MOE_TILE
    grid_spec = pltpu.PrefetchScalarGridSpec(
        num_scalar_prefetch=2,
        grid=(n_rows // tm,),
        in_specs=[pl.BlockSpec((tm, d), lambda i, be, nu: (i, 0)),
                  pl.BlockSpec((1, d, 2 * D_FF), lambda i, be, nu: (be[i], 0, 0)),
                  pl.BlockSpec((1, 1, 2 * D_FF), lambda i, be, nu: (be[i], 0, 0)),
                  pl.BlockSpec((1, D_FF, d), lambda i, be, nu: (be[i], 0, 0)),
                  pl.BlockSpec((1, 1, d), lambda i, be, nu: (be[i], 0, 0))],
        out_specs=pl.BlockSpec((tm, d), lambda i, be, nu: (i, 0)),
    )
    return pl.pallas_call(
        _ffn_kernel,
        grid_spec=grid_spec,
        out_shape=jax.ShapeDtypeStruct((n_rows, d), BF16),
        compiler_params=_cparams(("arbitrary",)),
    )(block_e, n_used, xs, w1, b1, w2, b2)


def moe_layer(x, nw, sh3, sc3, g_tok, rwt, rb, w1, b1, w2, b2, rows_per_group):
    t, d = x.shape
    h, top_e, gates = moe_router(x, nw, sh3, sc3, rwt, rb, rows_per_group)
    tm = MOE_TILE
    n_assign = t * TOP_K
    flat_e = top_e.T.reshape(-1)
    order = jnp.argsort(flat_e)
    sorted_e = flat_e[order]
    src_token = (order // TOP_K).astype(jnp.int32)
    counts = jnp.bincount(flat_e, length=N_EXPERTS)
    padded = (counts + tm - 1) // tm * tm
    start = jnp.cumsum(counts) - counts
    pend = jnp.cumsum(padded)
    pstart = pend - padded
    dest = (pstart[sorted_e] + (jnp.arange(n_assign) - start[sorted_e])).astype(jnp.int32)
    n_blocks = (n_assign + N_EXPERTS * (tm - 1) + tm - 1) // tm
    n_rows = n_blocks * tm
    row_token = jnp.full((n_rows,), t - 1, dtype=jnp.int32).at[dest].set(src_token)
    block_e = jnp.minimum(jnp.searchsorted(pend, jnp.arange(n_blocks) * tm, side='right'),
                          N_EXPERTS - 1).astype(jnp.int32)
    n_used = (pend[-1] // tm).astype(jnp.int32).reshape(1)
    xs = jnp.take(h, row_token, axis=0)
    ys = moe_ffn_blocks(xs, block_e, n_used, w1, b1, w2, b2)
    dest_tok = jnp.zeros((n_assign,), jnp.int32).at[order].set(dest)
    ysg = jnp.take(ys, dest_tok, axis=0).reshape(t, TOP_K, d).astype(F32)
    ff = jnp.sum(ysg * gates.T[:, :, None], axis=1)
    return x + g_tok * ff


def _ssd_kernel(z_ref, xbc_ref, dt_ref, cw_ref, cb_ref, dtb_ref, alog_ref, dx_ref, nw_ref, e_ref, tri_ref,
                ic_ref, is_ref, g_ref, so_ref, ext_scr, s_scr, y_scr, *, valid_rows):
    q = SSD_Q
    c = pl.program_id(1)

    @pl.when(c == 0)
    def _():
        ext_scr[0:8, :] = ic_ref[0]
        s_scr[...] = is_ref[0]

    ext_scr[8:8 + q, :] = xbc_ref[...].astype(F32)
    acc = cb_ref[...]
    for tap in range(CONV_W):
        acc = acc + ext_scr[pl.ds(8 - (CONV_W - 1) + tap, q), :] * cw_ref[tap:tap + 1, :]
    ext_scr[0:8, :] = ext_scr[q:q + 8, :]
    xc = _silu(acc)
    xs = xc[:, :D_INNER]

    dt = _softplus(dt_ref[...] + dtb_ref[...])
    if valid_rows < q:
        rows = lax.broadcasted_iota(jnp.int32, dt.shape, 0)
        dt = jnp.where(rows < valid_rows, dt, 0.0)
    a = dt * (-jnp.exp(alog_ref[...]))
    a_cs = jnp.dot(tri_ref[...], a, precision=HIGHEST, preferred_element_type=F32)
    dtx = jnp.dot(dt, e_ref[...], precision=HIGHEST, preferred_element_type=F32)
    acsx = jnp.dot(a_cs, e_ref[...], precision=HIGHEST, preferred_element_type=F32)

    row = lax.broadcasted_iota(jnp.int32, (q, D_INNER), 0)
    col = lax.broadcasted_iota(jnp.int32, (q, D_INNER), 1) & (HEAD_DIM_A - 1)
    acs_row = jnp.sum(jnp.where(row == col, acsx, 0.0), axis=0, keepdims=True)
    decay_in = jnp.where(row >= col, jnp.exp(jnp.minimum(acsx - acs_row, 0.0)), 0.0)
    last = acsx[q - 1:q, :]
    e_in = jnp.exp(acsx)
    e_last = jnp.exp(last)
    xd = xs * dtx
    xdd = xd * jnp.exp(last - acsx)
    lane = lax.broadcasted_iota(jnp.int32, (q, LANE), 1)
    lo = lane < HEAD_DIM_A

    for g in range(N_GROUPS):
        gsl = slice(g * GROUP_W, (g + 1) * GROUP_W)
        b_off = D_INNER + g * D_STATE
        c_off = D_INNER + N_GROUPS * D_STATE + g * D_STATE
        bg = xc[:, b_off:b_off + D_STATE]
        cgb = xc[:, c_off:c_off + D_STATE].astype(BF16)
        bgb = bg.astype(BF16)
        b2 = jnp.concatenate([bgb, bgb], axis=0)
        cb2 = lax.dot_general(cgb, b2, (((1,), (1,)), ((), ())), preferred_element_type=F32)
        s_g = s_scr[:, gsl]
        y_off = jnp.dot(cgb, s_g.astype(BF16), preferred_element_type=F32) * e_in[:, gsl]
        for pp in range(GROUP_W // LANE):
            sl = slice(g * GROUP_W + pp * LANE, g * GROUP_W + (pp + 1) * LANE)
            m = (decay_in[:, sl] * cb2).astype(BF16)
            xp = xd[:, sl]
            bd = jnp.concatenate([jnp.where(lo, xp, 0.0), jnp.where(lo, 0.0, xp)], axis=0).astype(BF16)
            y_diag = jnp.dot(m, bd, preferred_element_type=F32)
            y_scr[:, sl] = y_diag + y_off[:, pp * LANE:(pp + 1) * LANE] + xs[:, sl] * dx_ref[:, sl]
        s_new = jnp.dot(bg.T.astype(BF16), xdd[:, gsl].astype(BF16), preferred_element_type=F32)
        s_scr[:, gsl] = s_g * e_last[:, gsl] + s_new

    for g in range(N_GROUPS):
        gsl = slice(g * GROUP_W, (g + 1) * GROUP_W)
        gz = y_scr[:, gsl] * _silu(z_ref[:, gsl].astype(F32))
        ms = jnp.mean(gz * gz, axis=-1, keepdims=True)
        g_ref[:, gsl] = (gz * lax.rsqrt(ms + NORM_EPS) * nw_ref[:, gsl]).astype(g_ref.dtype)

    @pl.when(c == pl.num_programs(1) - 1)
    def _():
        so_ref[0] = s_scr[...]


def ssd_mixer(z, xbc, dt, cw, cb, dtb, alog, dx, nw, e_mat, tri, init_conv, init_ssm, nb, nc, valid_rows):
    q = SSD_Q
    const = lambda a: pl.BlockSpec(a.shape, lambda b, c: (0,) * a.ndim)
    return pl.pallas_call(
        functools.partial(_ssd_kernel, valid_rows=valid_rows),
        grid=(nb, nc),
        in_specs=[pl.BlockSpec((q, D_INNER), lambda b, c: (b * nc + c, 0)),
                  pl.BlockSpec((q, CONV_DIM), lambda b, c: (b * nc + c, 0)),
                  pl.BlockSpec((q, LANE), lambda b, c: (b * nc + c, 0)),
                  const(cw), const(cb), const(dtb), const(alog), const(dx), const(nw), const(e_mat), const(tri),
                  pl.BlockSpec((1, 8, CONV_DIM), lambda b, c: (b, 0, 0)),
                  pl.BlockSpec((1, D_STATE, D_INNER), lambda b, c: (b, 0, 0))],
        out_specs=[pl.BlockSpec((q, D_INNER), lambda b, c: (b * nc + c, 0)),
                   pl.BlockSpec((1, D_STATE, D_INNER), lambda b, c: (b, 0, 0))],
        out_shape=[jax.ShapeDtypeStruct((nb * nc * q, D_INNER), BF16),
                   jax.ShapeDtypeStruct((nb, D_STATE, D_INNER), F32)],
        scratch_shapes=[pltpu.VMEM((q + 8, CONV_DIM), F32),
                        pltpu.VMEM((D_STATE, D_INNER), F32),
                        pltpu.VMEM((q, D_INNER), F32)],
        compiler_params=_cparams(("parallel", "arbitrary")),
    )(z, xbc, dt, cw, cb, dtb, alog, dx, nw, e_mat, tri, init_conv, init_ssm)


def _attn_kernel(q_ref, k_ref, v_ref, u_ref, o_ref, acc_scr, c_scr, *, tq, tk, pos0, n_kblocks):
    qi = pl.program_id(2)
    scale = 1.0 / math.sqrt(HEAD_DIM_B)
    qv = q_ref[0]
    lane_q = lax.broadcasted_iota(jnp.int32, qv.shape, 1)
    zero_q = jnp.zeros_like(qv)
    q_halves = (jnp.where(lane_q < HEAD_DIM_B, qv, zero_q), jnp.where(lane_q < HEAD_DIM_B, zero_q, qv))
    lane_v = lax.broadcasted_iota(jnp.int32, (tk, LANE), 1)
    q_pos = pos0 + qi * tq + lax.broadcasted_iota(jnp.int32, (tq, tk), 0)
    k_idx = lax.broadcasted_iota(jnp.int32, (tq, tk), 1)
    n_vis = jnp.minimum((pos0 + (qi + 1) * tq - 2) // tk + 1, n_kblocks)
    acc_scr[...] = jnp.zeros_like(acc_scr)
    c_scr[...] = jnp.zeros_like(c_scr)

    def body(jj, carry):
        j = n_vis - 1 - jj
        start = pl.multiple_of(j * tk, tk)
        kb = k_ref[0, pl.ds(start, tk), :]
        vb = v_ref[0, pl.ds(start, tk), :]
        vis = (k_idx + j * tk) < q_pos
        zero_v = jnp.zeros_like(vb)
        for hh in range(2):
            z = lax.dot_general(q_halves[hh], kb, (((1,), (1,)), ((), ())), preferred_element_type=F32) * scale
            sp = jnp.where(vis, _softplus(z), 0.0)
            cs = jnp.dot(sp.astype(BF16), u_ref[...], preferred_element_type=F32) + c_scr[hh]
            w = jnp.where(vis, jnp.exp(z - cs), 0.0)
            vm = jnp.where(lane_v < HEAD_DIM_B, vb, zero_v) if hh == 0 else jnp.where(lane_v < HEAD_DIM_B, zero_v, vb)
            acc_scr[...] += jnp.dot(w.astype(BF16), vm, preferred_element_type=F32)
            c_scr[hh] = c_scr[hh] + jnp.sum(sp, axis=-1, keepdims=True)
        return carry

    lax.fori_loop(0, n_vis, body, 0)
    o_ref[0] = acc_scr[...].astype(o_ref.dtype)


def stick_breaking(q, k, v, u, pos0, tq):
    b, l, hd = q.shape
    kp = k.shape[1]
    tk = ATTN_TK
    return pl.pallas_call(
        functools.partial(_attn_kernel, tq=tq, tk=tk, pos0=pos0, n_kblocks=kp // tk),
        grid=(b, hd // LANE, l // tq),
        in_specs=[pl.BlockSpec((1, tq, LANE), lambda bi, p, i: (bi, i, p)),
                  pl.BlockSpec((1, kp, LANE), lambda bi, p, i: (bi, 0, p)),
                  pl.BlockSpec((1, kp, LANE), lambda bi, p, i: (bi, 0, p)),
                  pl.BlockSpec(u.shape, lambda bi, p, i: (0, 0))],
        out_specs=pl.BlockSpec((1, tq, LANE), lambda bi, p, i: (bi, i, p)),
        out_shape=jax.ShapeDtypeStruct((b, l, hd), BF16),
        scratch_shapes=[pltpu.VMEM((tq, LANE), F32), pltpu.VMEM((2, tq, 1), F32)],
        compiler_params=_cparams(("parallel", "parallel", "arbitrary")),
    )(q, k, v, u)


def _trunk(x3, mods, kv_mod, conv_state, ssm_state, past_k, past_v, p):
    b, l, d = x3.shape
    t = b * l
    x = x3.reshape(t, d)
    per_token = l < TOKEN_TILE

    def mod3(v):
        if per_token:
            return jnp.repeat(v, l, axis=0).reshape(t // min(TOKEN_TILE, t), min(TOKEN_TILE, t), d)
        return v[:, None, :]

    def tok(v):
        return jnp.repeat(v, l, axis=0)

    rows_per_group = min(TOKEN_TILE, t) if per_token else l
    pos0 = past_k.shape[1]
    conv_out, ssm_out = [], []
    k_new = v_new = kb = vb = None
    for layer in range(DEPTH):
        sh1, sc1, g1, sh2, sc2, g2 = jnp.split(mods[layer], 6, axis=-1)
        nw1 = p['mix_norm_w'][layer][None, :]
        if layer < N_A_LAYERS:
            i = layer
            z, xbc, dt = norm_mod_matmul(x, nw1, mod3(sh1), mod3(sc1), [p['w_z'][i], p['w_xbc'][i], p['w_dt'][i]],
                                         [BF16, BF16, F32], rows_per_group)
            xbc3 = xbc.reshape(b, l, CONV_DIM)
            conv_out.append(xbc3[:, l - (CONV_W - 1):].astype(F32))
            nc = max(l // SSD_Q, 1)
            if l < SSD_Q:
                pad = lambda a: jnp.pad(a.reshape(b, l, -1), ((0, 0), (0, SSD_Q - l), (0, 0))).reshape(b * SSD_Q, -1)
                z, xbc, dt = pad(z), pad(xbc), pad(dt)
            ic = jnp.pad(conv_state[i].astype(F32), ((0, 0), (8 - (CONV_W - 1), 0), (0, 0)))
            iss = ssm_state[i].astype(F32).transpose(0, 3, 1, 2).reshape(b, D_STATE, D_INNER)
            gm, s_fin = ssd_mixer(z, xbc, dt, p['a_conv_w'][i], p['a_conv_b'][i][None, :], p['dt_bias'][i],
                                  p['a_log'][i], p['d_x'][i], p['a_norm_w'][i][None, :], p['e_mat'], p['tri'],
                                  ic, iss, b, nc, min(l, SSD_Q))
            if l < SSD_Q:
                gm = gm.reshape(b, SSD_Q, D_INNER)[:, :l].reshape(t, D_INNER)
            ssm_out.append(s_fin.reshape(b, D_STATE, N_HEADS_A, HEAD_DIM_A).transpose(0, 2, 3, 1))
            x = matmul_residual(gm, p['a_out_proj'][i], x, mod3(g1), rows_per_group)
        else:
            j = layer - N_A_LAYERS
            (qp,) = norm_mod_matmul(x, nw1, mod3(sh1), mod3(sc1), [p['b_w_q'][j]], [BF16], rows_per_group)
            o = stick_breaking(qp.reshape(b, l, d), kb, vb, p['u_tri'], pos0, min(l, 256))
            x = matmul_residual(o.reshape(t, d), p['b_w_o'][j], x, mod3(g1), rows_per_group)
        x = moe_layer(x, p['ffn_norm_w'][layer][None, :], mod3(sh2), mod3(sc2), tok(g2),
                      p['router_wt'][layer], p['router_b'][layer], p['moe_w1'][layer], p['moe_b1'][layer],
                      p['moe_w2'][layer], p['moe_b2'][layer], rows_per_group)
        if layer == N_A_LAYERS - 1:
            sh_kv, sc_kv = jnp.split(kv_mod, 2, axis=-1)
            k_new, v_new = norm_mod_matmul(x, p['kv_norm_w'][None, :], mod3(sh_kv), mod3(sc_kv),
                                           [p['w_k'], p['w_v']], [F32, F32], rows_per_group)
            k_all = jnp.concatenate([past_k.reshape(b, pos0, d), k_new.reshape(b, l, d)], axis=1)
            v_all = jnp.concatenate([past_v.reshape(b, pos0, d), v_new.reshape(b, l, d)], axis=1)
            kpad = (-k_all.shape[1]) % ATTN_TK
            kb = jnp.pad(k_all, ((0, 0), (0, kpad), (0, 0))).astype(BF16)
            vb = jnp.pad(v_all, ((0, 0), (0, kpad), (0, 0))).astype(BF16)
    y = final_rmsnorm(x, p['final_norm_w'][None, :])
    return (y.reshape(b, l, d), jnp.stack(conv_out), jnp.stack(ssm_out),
            k_new.reshape(b, l, N_HEADS_B, HEAD_DIM_B), v_new.reshape(b, l, N_HEADS_B, HEAD_DIM_B))


def kernel(x_prompt, x_sample, state_conv, state_ssm, cache_k, cache_v, c_prompt, c_sample, mod_w, mod_b,
           mix_norm_w, ffn_norm_w, a_in_proj, a_conv_w, a_conv_b, a_dt_bias, a_A_log, a_D, a_norm_w, a_out_proj,
           kv_mod_w, kv_mod_b, kv_norm_w, w_kv, b_w_q, b_w_o, router_w, router_b, moe_w1, moe_b1, moe_w2, moe_b2,
           final_norm_w):
    bp = x_prompt.shape[0]
    hb = N_HEADS_B * HEAD_DIM_B
    pad_heads = lambda a: jnp.pad(a, ((0, 0), (0, LANE - N_HEADS_A)))[:, None, :]
    head_of_lane = jnp.arange(D_INNER) // HEAD_DIM_A
    p = dict(
        mix_norm_w=mix_norm_w, ffn_norm_w=ffn_norm_w, kv_norm_w=kv_norm_w, final_norm_w=final_norm_w,
        w_z=a_in_proj[:, :, :D_INNER].astype(BF16),
        w_xbc=a_in_proj[:, :, D_INNER:D_INNER + CONV_DIM].astype(BF16),
        w_dt=jnp.pad(a_in_proj[:, :, D_INNER + CONV_DIM:], ((0, 0), (0, 0), (0, LANE - N_HEADS_A))).astype(BF16),
        a_conv_w=a_conv_w, a_conv_b=a_conv_b, dt_bias=pad_heads(a_dt_bias), a_log=pad_heads(a_A_log),
        d_x=a_D[:, head_of_lane][:, None, :], a_norm_w=a_norm_w, a_out_proj=a_out_proj.astype(BF16),
        e_mat=(jnp.arange(LANE)[:, None] == head_of_lane[None, :]).astype(F32),
        tri=(jnp.arange(SSD_Q)[:, None] >= jnp.arange(SSD_Q)[None, :]).astype(F32),
        u_tri=(jnp.arange(ATTN_TK)[:, None] >= jnp.arange(ATTN_TK)[None, :]).astype(BF16),
        w_k=w_kv[:, :hb].astype(BF16), w_v=w_kv[:, hb:].astype(BF16),
        b_w_q=b_w_q.astype(BF16), b_w_o=b_w_o.astype(BF16),
        router_wt=router_w.transpose(0, 2, 1), router_b=router_b[:, :, None],
        moe_w1=moe_w1.astype(BF16), moe_b1=moe_b1[:, :, None, :],
        moe_w2=moe_w2.astype(BF16), moe_b2=moe_b2[:, :, None, :],
    )
    c_all = jnp.concatenate([c_prompt, c_sample], axis=0)
    mods = mod_vectors(c_all, mod_w, mod_b[:, None, :])
    kv_mod = mod_vectors(c_all, kv_mod_w[None], kv_mod_b[None, None, :])[0]

    conv0 = jnp.zeros((N_A_LAYERS, bp, CONV_W - 1, CONV_DIM), F32)
    ssm0 = jnp.zeros((N_A_LAYERS, bp, N_HEADS_A, HEAD_DIM_A, D_STATE), F32)
    kv0 = jnp.zeros((bp, 0, N_HEADS_B, HEAD_DIM_B), F32)
    out_p = _trunk(x_prompt, mods[:, :bp], kv_mod[:bp], conv0, ssm0, kv0, kv0, p)
    out_s = _trunk(x_sample, mods[:, bp:], kv_mod[bp:], state_conv, state_ssm, cache_k, cache_v, p)
    return (out_p[0], out_s[0]) + out_p[1:] + out_s[1:]
```

```python
import functools
import math

import jax
import jax.numpy as jnp
from jax import lax
from jax.experimental import pallas as pl
from jax.experimental.pallas import tpu as pltpu

F32 = jnp.float32
BF16 = jnp.bfloat16
HIGHEST = lax.Precision.HIGHEST

D_MODEL = 1024
DEPTH = 4
N_A_LAYERS = 2
D_INNER = 2048
HEAD_DIM_A = 64
N_HEADS_A = 32
D_STATE = 128
N_GROUPS = 4
GROUP_W = D_INNER // N_GROUPS
CONV_W = 4
CONV_DIM = D_INNER + 2 * N_GROUPS * D_STATE
HEAD_DIM_B = 64
N_HEADS_B = 16
N_EXPERTS = 32
TOP_K = 4
D_FF = 1024
SWIGLU_ALPHA = 1.702
SWIGLU_LIMIT = 7.0
NORM_EPS = 1e-6
LOG2_E = 1.4426950408889634
HIDDEN = -1e30

LANE = 128
SSD_Q = 64
TOKEN_TILE = 512
MOE_TILE = 512
ATTN_TK = 256
VMEM_LIMIT = 56 * 1024 * 1024


def _cparams(sem):
    return pltpu.CompilerParams(dimension_semantics=sem, vmem_limit_bytes=VMEM_LIMIT)


def _softplus(x):
    return jnp.maximum(x, 0.0) + jnp.log1p(jnp.exp(-jnp.abs(x)))


def _silu(x):
    return x * jax.nn.sigmoid(x)


def _norm_mod(x, nw, sh, sc):
    ms = jnp.mean(x * x, axis=-1, keepdims=True)
    return (x * lax.rsqrt(ms + NORM_EPS) * nw) * (1.0 + sc) + sh


def _mod_spec(mod3, tile, rows_per_group):
    steps = max(rows_per_group // tile, 1)
    g = mod3.shape[1]
    return pl.BlockSpec((1, g, mod3.shape[2]), lambda i: (i // steps, 0, 0))


def _mod_kernel(c_ref, w_ref, b_ref, o_ref):
    cs = _silu(c_ref[...])
    o_ref[0] = jnp.dot(cs, w_ref[0], precision=HIGHEST, preferred_element_type=F32) + b_ref[0]


def mod_vectors(c, w, b):
    nl, d, n = w.shape
    tn = 1024
    return pl.pallas_call(
        _mod_kernel,
        grid=(nl, n // tn),
        in_specs=[pl.BlockSpec(c.shape, lambda l, j: (0, 0)),
                  pl.BlockSpec((1, d, tn), lambda l, j: (l, 0, j)),
                  pl.BlockSpec((1, 1, tn), lambda l, j: (l, 0, j))],
        out_specs=pl.BlockSpec((1, c.shape[0], tn), lambda l, j: (l, 0, j)),
        out_shape=jax.ShapeDtypeStruct((nl, c.shape[0], n), F32),
        compiler_params=_cparams(("arbitrary", "arbitrary")),
        name="mod_vectors",
    )(c, w, b)


def _nmm_kernel(x_ref, nw_ref, sh_ref, sc_ref, *refs, n_w):
    h = _norm_mod(x_ref[...], nw_ref[...], sh_ref[0], sc_ref[0]).astype(BF16)
    for w_ref, o_ref in zip(refs[:n_w], refs[n_w:]):
        o_ref[...] = jnp.dot(h, w_ref[...], preferred_element_type=F32).astype(o_ref.dtype)


def norm_mod_matmul(x, nw, sh3, sc3, ws, out_dtypes, rows_per_group):
    t, d = x.shape
    tm = min(TOKEN_TILE, t)
    in_specs = [pl.BlockSpec((tm, d), lambda i: (i, 0)),
                pl.BlockSpec((1, d), lambda i: (0, 0)),
                _mod_spec(sh3, tm, rows_per_group), _mod_spec(sc3, tm, rows_per_group)]
    in_specs += [pl.BlockSpec(w.shape, lambda i: (0, 0), pipeline_mode=pl.Buffered(1)) for w in ws]
    return pl.pallas_call(
        functools.partial(_nmm_kernel, n_w=len(ws)),
        grid=(t // tm,),
        in_specs=in_specs,
        out_specs=[pl.BlockSpec((tm, w.shape[1]), lambda i: (i, 0)) for w in ws],
        out_shape=[jax.ShapeDtypeStruct((t, w.shape[1]), dt) for w, dt in zip(ws, out_dtypes)],
        compiler_params=_cparams(("parallel",)),
        name="norm_mod_matmul",
    )(x, nw, sh3, sc3, *ws)


def _mmres_kernel(a_ref, w_ref, x_ref, g_ref, o_ref):
    acc = jnp.dot(a_ref[...], w_ref[...], preferred_element_type=F32)
    o_ref[...] = x_ref[...] + g_ref[0] * acc


def matmul_residual(a, w, x, g3, rows_per_group):
    t, d = x.shape
    tm = min(TOKEN_TILE, t)
    return pl.pallas_call(
        _mmres_kernel,
        grid=(t // tm,),
        in_specs=[pl.BlockSpec((tm, a.shape[1]), lambda i: (i, 0)),
                  pl.BlockSpec(w.shape, lambda i: (0, 0), pipeline_mode=pl.Buffered(1)),
                  pl.BlockSpec((tm, d), lambda i: (i, 0)),
                  _mod_spec(g3, tm, rows_per_group)],
        out_specs=pl.BlockSpec((tm, d), lambda i: (i, 0)),
        out_shape=jax.ShapeDtypeStruct((t, d), F32),
        compiler_params=_cparams(("parallel",)),
        name="matmul_residual",
    )(a, w, x, g3)


def _rms_kernel(x_ref, nw_ref, o_ref):
    x = x_ref[...]
    ms = jnp.mean(x * x, axis=-1, keepdims=True)
    o_ref[...] = x * lax.rsqrt(ms + NORM_EPS) * nw_ref[...]


def final_rmsnorm(x, nw):
    t, d = x.shape
    tm = min(TOKEN_TILE, t)
    return pl.pallas_call(
        _rms_kernel,
        grid=(t // tm,),
        in_specs=[pl.BlockSpec((tm, d), lambda i: (i, 0)), pl.BlockSpec((1, d), lambda i: (0, 0))],
        out_specs=pl.BlockSpec((tm, d), lambda i: (i, 0)),
        out_shape=jax.ShapeDtypeStruct((t, d), F32),
        compiler_params=_cparams(("parallel",)),
        name="final_rmsnorm",
    )(x, nw)


def _router_kernel(x_ref, nw_ref, sh_ref, sc_ref, rwt_ref, rb_ref, su_ref, h_ref, e_ref, g_ref, r_ref, cnt_ref,
                   run_scr):
    @pl.when(pl.program_id(0) == 0)
    def _():
        run_scr[...] = jnp.zeros_like(run_scr)

    h = _norm_mod(x_ref[...], nw_ref[...], sh_ref[0], sc_ref[0])
    h_ref[...] = h.astype(BF16)
    logits = lax.dot_general(rwt_ref[...], h, (((1,), (1,)), ((), ())),
                             precision=HIGHEST, preferred_element_type=F32) + rb_ref[...]
    eid = lax.broadcasted_iota(jnp.int32, logits.shape, 0)
    vals, idxs, hits = [], [], []
    for _ in range(TOP_K):
        m = jnp.max(logits, axis=0, keepdims=True)
        idx = jnp.min(jnp.where(logits == m, eid, N_EXPERTS), axis=0, keepdims=True)
        hit = eid == idx
        vals.append(m)
        idxs.append(idx)
        hits.append(hit)
        logits = jnp.where(hit, -jnp.inf, logits)
    ex = [jnp.exp(v - vals[0]) for v in vals]
    den = ex[0] + ex[1] + ex[2] + ex[3]
    cnt = (hits[0] | hits[1] | hits[2] | hits[3]).astype(F32)
    before = jnp.dot(cnt.astype(BF16), su_ref[...], preferred_element_type=F32) + run_scr[...]
    for k in range(TOP_K):
        e_ref[k:k + 1, :] = idxs[k]
        g_ref[k:k + 1, :] = ex[k] / den
        r_ref[k:k + 1, :] = jnp.sum(jnp.where(hits[k], before, 0.0), axis=0, keepdims=True).astype(jnp.int32)
    run_scr[...] += jnp.sum(cnt, axis=1, keepdims=True)
    cnt_ref[...] = run_scr[...]


def moe_router(x, nw, sh3, sc3, rwt, rb, su, rows_per_group):
    t, d = x.shape
    tm = min(TOKEN_TILE, t)
    return pl.pallas_call(
        _router_kernel,
        grid=(t // tm,),
        in_specs=[pl.BlockSpec((tm, d), lambda i: (i, 0)),
                  pl.BlockSpec((1, d), lambda i: (0, 0)),
                  _mod_spec(sh3, tm, rows_per_group), _mod_spec(sc3, tm, rows_per_group),
                  pl.BlockSpec(rwt.shape, lambda i: (0, 0)),
                  pl.BlockSpec(rb.shape, lambda i: (0, 0)),
                  pl.BlockSpec(su.shape, lambda i: (0, 0))],
        out_specs=[pl.BlockSpec((tm, d), lambda i: (i, 0)),
                   pl.BlockSpec((TOP_K, tm), lambda i: (0, i)),
                   pl.BlockSpec((TOP_K, tm), lambda i: (0, i)),
                   pl.BlockSpec((TOP_K, tm), lambda i: (0, i)),
                   pl.BlockSpec((N_EXPERTS, 1), lambda i: (0, 0))],
        out_shape=[jax.ShapeDtypeStruct((t, d), BF16),
                   jax.ShapeDtypeStruct((TOP_K, t), jnp.int32),
                   jax.ShapeDtypeStruct((TOP_K, t), F32),
                   jax.ShapeDtypeStruct((TOP_K, t), jnp.int32),
                   jax.ShapeDtypeStruct((N_EXPERTS, 1), F32)],
        scratch_shapes=[pltpu.VMEM((N_EXPERTS, 1), F32)],
        compiler_params=_cparams(("arbitrary",)),
        name="moe_router",
    )(x, nw, sh3, sc3, rwt, rb, su)


def _ffn_kernel(be_ref, nu_ref, xs_ref, gate_ref, w1_ref, b1_ref, w2_ref, b2_ref, o_ref, w1_scr, w2_scr):
    i = pl.program_id(0)

    @pl.when(i < nu_ref[0])
    def _():
        @pl.when((i == 0) | (be_ref[i] != be_ref[jnp.maximum(i - 1, 0)]))
        def _():
            w1_scr[...] = w1_ref[0].astype(BF16)
            w2_scr[...] = w2_ref[0].astype(BF16)

        hid = jnp.dot(xs_ref[...], w1_scr[...], preferred_element_type=F32) + b1_ref[0]
        glu = jnp.minimum(hid[:, :D_FF], SWIGLU_LIMIT)
        lin = jnp.clip(hid[:, D_FF:], -SWIGLU_LIMIT, SWIGLU_LIMIT)
        act = glu * jax.nn.sigmoid(SWIGLU_ALPHA * glu) * (lin + 1.0)
        out = jnp.dot(act.astype(BF16), w2_scr[...], preferred_element_type=F32) + b2_ref[0]
        o_ref[...] = (out * gate_ref[...]).astype(o_ref.dtype)


def moe_ffn_blocks(xs, gate_rows, block_e, n_used, w1, b1, w2, b2):
    n_rows, d = xs.shape
    tm = MOE_TILE
    grid_spec = pltpu.PrefetchScalarGridSpec(
        num_scalar_prefetch=2,
        grid=(n_rows // tm,),
        in_specs=[pl.BlockSpec((tm, d), lambda i, be, nu: (i, 0)),
                  pl.BlockSpec((tm, 1), lambda i, be, nu: (i, 0)),
                  pl.BlockSpec((1, d, 2 * D_FF), lambda i, be, nu: (be[i], 0, 0)),
                  pl.BlockSpec((1, 1, 2 * D_FF), lambda i, be, nu: (be[i], 0, 0)),
                  pl.BlockSpec((1, D_FF, d), lambda i, be, nu: (be[i], 0, 0)),
                  pl.BlockSpec((1, 1, d), lambda i, be, nu: (be[i], 0, 0))],
        out_specs=pl.BlockSpec((tm, d), lambda i, be, nu: (i, 0)),
        scratch_shapes=[pltpu.VMEM((d, 2 * D_FF), BF16), pltpu.VMEM((D_FF, d), BF16)],
    )
    return pl.pallas_call(
        _ffn_kernel,
        grid_spec=grid_spec,
        out_shape=jax.ShapeDtypeStruct((n_rows, d), BF16),
        compiler_params=_cparams(("arbitrary",)),
        name="moe_ffn",
    )(block_e, n_used, xs, gate_rows, w1, b1, w2, b2)


def _combine_kernel(y_ref, x_ref, g_ref, o_ref):
    y = y_ref[...].astype(F32)
    o_ref[...] = x_ref[...] + g_ref[0] * ((y[0] + y[1]) + (y[2] + y[3]))


def moe_combine(ysg, x, g3, rows_per_group):
    t, d = x.shape
    tm = min(TOKEN_TILE, t)
    return pl.pallas_call(
        _combine_kernel,
        grid=(t // tm,),
        in_specs=[pl.BlockSpec((TOP_K, tm, d), lambda i: (0, i, 0)),
                  pl.BlockSpec((tm, d), lambda i: (i, 0)),
                  _mod_spec(g3, tm, rows_per_group)],
        out_specs=pl.BlockSpec((tm, d), lambda i: (i, 0)),
        out_shape=jax.ShapeDtypeStruct((t, d), F32),
        compiler_params=_cparams(("parallel",)),
        name="moe_combine",
    )(ysg, x, g3)


def moe_layer(x, nw, sh3, sc3, g3, rwt, rb, su, w1, b1, w2, b2, rows_per_group):
    t, d = x.shape
    h, top_e, gates, rank, cnt = moe_router(x, nw, sh3, sc3, rwt, rb, su, rows_per_group)
    tm = MOE_TILE
    n_assign = t * TOP_K
    n_blocks = (n_assign + N_EXPERTS * (tm - 1) + tm - 1) // tm
    n_rows = n_blocks * tm
    counts = cnt[:, 0].astype(jnp.int32)
    padded = (counts + tm - 1) // tm * tm
    start = jnp.cumsum(counts) - counts
    pend = jnp.cumsum(padded)
    pstart = pend - padded
    block_e = jnp.minimum(jnp.searchsorted(pend, jnp.arange(n_blocks) * tm, side='right'),
                          N_EXPERTS - 1).astype(jnp.int32)
    n_used = (pend[-1] // tm).astype(jnp.int32).reshape(1)
    experts = jnp.arange(N_EXPERTS, dtype=jnp.int32)
    dest = rank + jnp.sum(jnp.where(top_e[:, :, None] == experts, pstart.astype(jnp.int32), 0), axis=-1)
    order = jnp.argsort(top_e.T.reshape(-1)).astype(jnp.int32)
    row_e = jnp.repeat(block_e, tm)
    in_e = jnp.arange(n_rows, dtype=jnp.int32) - pstart[row_e].astype(jnp.int32)
    valid = in_e < counts[row_e]
    row_assign = jnp.take(order, jnp.where(valid, start[row_e].astype(jnp.int32) + in_e, 0))
    row_token = row_assign // TOP_K
    gate_rows = jnp.where(valid, jnp.take(gates.T.reshape(-1), row_assign), 0.0)[:, None]
    xs = jnp.take(h, row_token, axis=0)
    ys = moe_ffn_blocks(xs, gate_rows, block_e, n_used, w1, b1, w2, b2)
    ysg = jnp.take(ys, dest.reshape(-1), axis=0).reshape(TOP_K, t, d)
    return moe_combine(ysg, x, g3, rows_per_group)


def _ssd_kernel(z_ref, xbc_ref, dt_ref, cw_ref, cb_ref, dtb_ref, alog_ref, dx_ref, nw_ref, e_ref, tri_ref,
                ic_ref, is_ref, g_ref, so_ref, ext_scr, s_scr, y_scr, *, valid_rows):
    q = SSD_Q
    c = pl.program_id(1)

    @pl.when(c == 0)
    def _():
        ext_scr[0:8, :] = ic_ref[0]
        s_scr[...] = is_ref[0]

    ext_scr[8:8 + q, :] = xbc_ref[...].astype(F32)
    acc = cb_ref[...]
    for tap in range(CONV_W):
        acc = acc + ext_scr[pl.ds(8 - (CONV_W - 1) + tap, q), :] * cw_ref[tap:tap + 1, :]
    ext_scr[0:8, :] = ext_scr[q:q + 8, :]
    xc = _silu(acc)
    xs = xc[:, :D_INNER]

    dt = _softplus(dt_ref[...] + dtb_ref[...])
    if valid_rows < q:
        rows = lax.broadcasted_iota(jnp.int32, dt.shape, 0)
        dt = jnp.where(rows < valid_rows, dt, 0.0)
    a = dt * (-jnp.exp(alog_ref[...]))
    a_cs = jnp.dot(tri_ref[...], a, precision=HIGHEST, preferred_element_type=F32)
    dtx = jnp.dot(dt, e_ref[...], precision=HIGHEST, preferred_element_type=F32)
    acsx = jnp.dot(a_cs, e_ref[...], precision=HIGHEST, preferred_element_type=F32)

    row = lax.broadcasted_iota(jnp.int32, (q, D_INNER), 0)
    col = lax.broadcasted_iota(jnp.int32, (q, D_INNER), 1) & (HEAD_DIM_A - 1)
    acs_row = jnp.sum(jnp.where(row == col, acsx, 0.0), axis=0, keepdims=True)
    decay_in = jnp.where(row >= col, jnp.exp(jnp.minimum(acsx - acs_row, 0.0)), 0.0)
    last = acsx[q - 1:q, :]
    e_in = jnp.exp(acsx)
    e_last = jnp.exp(last)
    xd = xs * dtx
    xdd = xd * jnp.exp(last - acsx)
    lane = lax.broadcasted_iota(jnp.int32, (q, LANE), 1)
    lo = lane < HEAD_DIM_A

    for g in range(N_GROUPS):
        gsl = slice(g * GROUP_W, (g + 1) * GROUP_W)
        b_off = D_INNER + g * D_STATE
        c_off = D_INNER + N_GROUPS * D_STATE + g * D_STATE
        bg = xc[:, b_off:b_off + D_STATE]
        cgb = xc[:, c_off:c_off + D_STATE].astype(BF16)
        bgb = bg.astype(BF16)
        b2 = jnp.concatenate([bgb, bgb], axis=0)
        cb2 = lax.dot_general(cgb, b2, (((1,), (1,)), ((), ())), preferred_element_type=F32)
        s_g = s_scr[:, gsl]
        y_off = jnp.dot(cgb, s_g.astype(BF16), preferred_element_type=F32) * e_in[:, gsl]
        for pp in range(GROUP_W // LANE):
            sl = slice(g * GROUP_W + pp * LANE, g * GROUP_W + (pp + 1) * LANE)
            m = (decay_in[:, sl] * cb2).astype(BF16)
            xp = xd[:, sl]
            bd = jnp.concatenate([jnp.where(lo, xp, 0.0), jnp.where(lo, 0.0, xp)], axis=0).astype(BF16)
            y_diag = jnp.dot(m, bd, preferred_element_type=F32)
            y_scr[:, sl] = y_diag + y_off[:, pp * LANE:(pp + 1) * LANE] + xs[:, sl] * dx_ref[:, sl]
        s_new = jnp.dot(bg.T.astype(BF16), xdd[:, gsl].astype(BF16), preferred_element_type=F32)
        s_scr[:, gsl] = s_g * e_last[:, gsl] + s_new

    for g in range(N_GROUPS):
        gsl = slice(g * GROUP_W, (g + 1) * GROUP_W)
        gz = y_scr[:, gsl] * _silu(z_ref[:, gsl].astype(F32))
        ms = jnp.mean(gz * gz, axis=-1, keepdims=True)
        g_ref[:, gsl] = (gz * lax.rsqrt(ms + NORM_EPS) * nw_ref[:, gsl]).astype(g_ref.dtype)

    @pl.when(c == pl.num_programs(1) - 1)
    def _():
        so_ref[0] = s_scr[...]


def ssd_mixer(z, xbc, dt, cw, cb, dtb, alog, dx, nw, e_mat, tri, init_conv, init_ssm, nb, nc, valid_rows):
    q = SSD_Q
    const = lambda a: pl.BlockSpec(a.shape, lambda b, c: (0,) * a.ndim)
    return pl.pallas_call(
        functools.partial(_ssd_kernel, valid_rows=valid_rows),
        grid=(nb, nc),
        in_specs=[pl.BlockSpec((q, D_INNER), lambda b, c: (b * nc + c, 0)),
                  pl.BlockSpec((q, CONV_DIM), lambda b, c: (b * nc + c, 0)),
                  pl.BlockSpec((q, LANE), lambda b, c: (b * nc + c, 0)),
                  const(cw), const(cb), const(dtb), const(alog), const(dx), const(nw), const(e_mat), const(tri),
                  pl.BlockSpec((1, 8, CONV_DIM), lambda b, c: (b, 0, 0)),
                  pl.BlockSpec((1, D_STATE, D_INNER), lambda b, c: (b, 0, 0))],
        out_specs=[pl.BlockSpec((q, D_INNER), lambda b, c: (b * nc + c, 0)),
                   pl.BlockSpec((1, D_STATE, D_INNER), lambda b, c: (b, 0, 0))],
        out_shape=[jax.ShapeDtypeStruct((nb * nc * q, D_INNER), BF16),
                   jax.ShapeDtypeStruct((nb, D_STATE, D_INNER), F32)],
        scratch_shapes=[pltpu.VMEM((q + 8, CONV_DIM), F32),
                        pltpu.VMEM((D_STATE, D_INNER), F32),
                        pltpu.VMEM((q, D_INNER), F32)],
        compiler_params=_cparams(("parallel", "arbitrary")),
        name="ssd_mixer",
    )(z, xbc, dt, cw, cb, dtb, alog, dx, nw, e_mat, tri, init_conv, init_ssm)


def _attn_kernel(q_ref, k_ref, v_ref, u_ref, o_ref, qs_scr, acc_scr, c_scr, y_scr, t_scr, tot_scr,
                 *, tq, tk, pos0, n_kblocks):
    qi = pl.program_id(2)
    qv = q_ref[0]
    lane_q = lax.broadcasted_iota(jnp.int32, qv.shape, 1)
    zero_q = jnp.zeros_like(qv)
    qs_scr[0:tq, :] = jnp.where(lane_q < HEAD_DIM_B, qv, zero_q)
    qs_scr[tq:2 * tq, :] = jnp.where(lane_q < HEAD_DIM_B, zero_q, qv)
    acc_scr[...] = jnp.zeros_like(acc_scr)
    c_scr[...] = jnp.zeros_like(c_scr)
    y_scale = LOG2_E / math.sqrt(HEAD_DIM_B)
    q_idx = lax.broadcasted_iota(jnp.int32, (2 * tq, tk), 0) & (tq - 1)
    pos_gap = pos0 + qi * tq + q_idx - lax.broadcasted_iota(jnp.int32, (2 * tq, tk), 1)
    n_vis = jnp.minimum((pos0 + (qi + 1) * tq - 2) // tk + 1, n_kblocks)
    n_full = jnp.minimum((pos0 + qi * tq) // tk, n_kblocks)
    n_masked = n_vis - n_full

    def kblock(i):
        return jnp.maximum(n_vis - 1 - i, 0)

    def scores(i):
        start = pl.multiple_of(kblock(i) * tk, tk)
        kb = k_ref[0, pl.ds(start, tk), :]
        y = lax.dot_general(qs_scr[...], kb, (((1,), (1,)), ((), ())), preferred_element_type=F32)
        y_scr[i & 1] = y * y_scale

    def suffix(i, masked):
        y = y_scr[i & 1]
        sp = jnp.maximum(y, jnp.log(1.0 + jnp.exp2(jnp.minimum(y, 126.0))) * LOG2_E)
        if masked:
            vis = pos_gap > kblock(i) * tk
            sp = jnp.where(vis, sp, 0.0)
        cs = jnp.dot(sp.astype(BF16), u_ref[...], preferred_element_type=F32)
        t = y - cs
        if masked:
            t = jnp.where(vis, t, HIDDEN)
        t_scr[i & 1] = t
        tot_scr[i & 1] = cs[:, 0:1]

    def weigh(i):
        start = pl.multiple_of(kblock(i) * tk, tk)
        vb = v_ref[0, pl.ds(start, tk), :]
        c = c_scr[...]
        w = jnp.exp2(t_scr[i & 1] - c)
        acc_scr[...] += jnp.dot(w.astype(BF16), vb, preferred_element_type=F32)
        c_scr[...] = c + tot_scr[i & 1]

    scores(0)
    suffix(0, True)
    scores(1)

    def trip(i, masked):
        weigh(i)
        suffix(i + 1, masked)
        scores(i + 2)

    def masked_trip(i, carry):
        trip(i, True)
        return carry

    def full_trip(i, carry):
        trip(i, False)
        return carry

    n_mt = jnp.maximum(n_masked - 1, 0)
    lax.fori_loop(0, n_mt, masked_trip, 0)
    lax.fori_loop(n_mt, n_vis, full_trip, 0)
    lane_o = lax.broadcasted_iota(jnp.int32, (tq, LANE), 1)
    o_ref[0] = jnp.where(lane_o < HEAD_DIM_B, acc_scr[0:tq, :], acc_scr[tq:2 * tq, :]).astype(o_ref.dtype)


def stick_breaking(q, k, v, u, pos0, tq):
    b, l, hd = q.shape
    kp = k.shape[1]
    tk = ATTN_TK
    assert tq & (tq - 1) == 0 and l % tq == 0 and kp % tk == 0
    return pl.pallas_call(
        functools.partial(_attn_kernel, tq=tq, tk=tk, pos0=pos0, n_kblocks=kp // tk),
        grid=(b, hd // LANE, l // tq),
        in_specs=[pl.BlockSpec((1, tq, LANE), lambda bi, p, i: (bi, i, p)),
                  pl.BlockSpec((1, kp, LANE), lambda bi, p, i: (bi, 0, p)),
                  pl.BlockSpec((1, kp, LANE), lambda bi, p, i: (bi, 0, p)),
                  pl.BlockSpec(u.shape, lambda bi, p, i: (0, 0))],
        out_specs=pl.BlockSpec((1, tq, LANE), lambda bi, p, i: (bi, i, p)),
        out_shape=jax.ShapeDtypeStruct((b, l, hd), BF16),
        scratch_shapes=[pltpu.VMEM((2 * tq, LANE), BF16), pltpu.VMEM((2 * tq, LANE), F32),
                        pltpu.VMEM((2 * tq, 1), F32), pltpu.VMEM((2, 2 * tq, tk), F32),
                        pltpu.VMEM((2, 2 * tq, tk), F32), pltpu.VMEM((2, 2 * tq, 1), F32)],
        compiler_params=_cparams(("parallel", "parallel", "arbitrary")),
        name="stick_breaking",
    )(q, k, v, u)


def _trunk(x3, mods, kv_mod, conv_state, ssm_state, past_k, past_v, p):
    b, l, d = x3.shape
    t = b * l
    x = x3.reshape(t, d)
    per_token = l < TOKEN_TILE

    def mod3(v):
        if per_token:
            return jnp.repeat(v, l, axis=0).reshape(t // min(TOKEN_TILE, t), min(TOKEN_TILE, t), d)
        return v[:, None, :]

    rows_per_group = min(TOKEN_TILE, t) if per_token else l
    pos0 = past_k.shape[1]
    conv_out, ssm_out = [], []
    k_new = v_new = kb = vb = None
    for layer in range(DEPTH):
        sh1, sc1, g1, sh2, sc2, g2 = jnp.split(mods[layer], 6, axis=-1)
        nw1 = p['mix_norm_w'][layer][None, :]
        if layer < N_A_LAYERS:
            i = layer
            z, xbc, dt = norm_mod_matmul(x, nw1, mod3(sh1), mod3(sc1), [p['w_z'][i], p['w_xbc'][i], p['w_dt'][i]],
                                         [BF16, BF16, F32], rows_per_group)
            xbc3 = xbc.reshape(b, l, CONV_DIM)
            conv_out.append(xbc3[:, l - (CONV_W - 1):].astype(F32))
            nc = max(l // SSD_Q, 1)
            if l < SSD_Q:
                pad = lambda a: jnp.pad(a.reshape(b, l, -1), ((0, 0), (0, SSD_Q - l), (0, 0))).reshape(b * SSD_Q, -1)
                z, xbc, dt = pad(z), pad(xbc), pad(dt)
            ic = jnp.pad(conv_state[i].astype(F32), ((0, 0), (8 - (CONV_W - 1), 0), (0, 0)))
            iss = ssm_state[i].astype(F32).transpose(0, 3, 1, 2).reshape(b, D_STATE, D_INNER)
            gm, s_fin = ssd_mixer(z, xbc, dt, p['a_conv_w'][i], p['a_conv_b'][i][None, :], p['dt_bias'][i],
                                  p['a_log'][i], p['d_x'][i], p['a_norm_w'][i][None, :], p['e_mat'], p['tri'],
                                  ic, iss, b, nc, min(l, SSD_Q))
            if l < SSD_Q:
                gm = gm.reshape(b, SSD_Q, D_INNER)[:, :l].reshape(t, D_INNER)
            ssm_out.append(s_fin.reshape(b, D_STATE, N_HEADS_A, HEAD_DIM_A).transpose(0, 2, 3, 1))
            x = matmul_residual(gm, p['a_out_proj'][i], x, mod3(g1), rows_per_group)
        else:
            j = layer - N_A_LAYERS
            (qp,) = norm_mod_matmul(x, nw1, mod3(sh1), mod3(sc1), [p['b_w_q'][j]], [BF16], rows_per_group)
            o = stick_breaking(qp.reshape(b, l, d), kb, vb, p['u_tri'], pos0, min(l, 256))
            x = matmul_residual(o.reshape(t, d), p['b_w_o'][j], x, mod3(g1), rows_per_group)
        x = moe_layer(x, p['ffn_norm_w'][layer][None, :], mod3(sh2), mod3(sc2), mod3(g2),
                      p['router_wt'][layer], p['router_b'][layer], p['su_tri'], p['moe_w1'][layer],
                      p['moe_b1'][layer], p['moe_w2'][layer], p['moe_b2'][layer], rows_per_group)
        if layer == N_A_LAYERS - 1:
            sh_kv, sc_kv = jnp.split(kv_mod, 2, axis=-1)
            k_new, v_new = norm_mod_matmul(x, p['kv_norm_w'][None, :], mod3(sh_kv), mod3(sc_kv),
                                           [p['w_k'], p['w_v']], [F32, F32], rows_per_group)
            k_all = jnp.concatenate([past_k.reshape(b, pos0, d), k_new.reshape(b, l, d)], axis=1)
            v_all = jnp.concatenate([past_v.reshape(b, pos0, d), v_new.reshape(b, l, d)], axis=1)
            kpad = (-k_all.shape[1]) % ATTN_TK
            kb = jnp.pad(k_all, ((0, 0), (0, kpad), (0, 0))).astype(BF16)
            vb = jnp.pad(v_all, ((0, 0), (0, kpad), (0, 0))).astype(BF16)
    y = final_rmsnorm(x, p['final_norm_w'][None, :])
    return (y.reshape(b, l, d), jnp.stack(conv_out), jnp.stack(ssm_out),
            k_new.reshape(b, l, N_HEADS_B, HEAD_DIM_B), v_new.reshape(b, l, N_HEADS_B, HEAD_DIM_B))


def kernel(x_prompt, x_sample, state_conv, state_ssm, cache_k, cache_v, c_prompt, c_sample, mod_w, mod_b,
           mix_norm_w, ffn_norm_w, a_in_proj, a_conv_w, a_conv_b, a_dt_bias, a_A_log, a_D, a_norm_w, a_out_proj,
           kv_mod_w, kv_mod_b, kv_norm_w, w_kv, b_w_q, b_w_o, router_w, router_b, moe_w1, moe_b1, moe_w2, moe_b2,
           final_norm_w):
    bp = x_prompt.shape[0]
    hb = N_HEADS_B * HEAD_DIM_B
    pad_heads = lambda a: jnp.pad(a, ((0, 0), (0, LANE - N_HEADS_A)))[:, None, :]
    head_of_lane = jnp.arange(D_INNER) // HEAD_DIM_A
    p = dict(
        mix_norm_w=mix_norm_w, ffn_norm_w=ffn_norm_w, kv_norm_w=kv_norm_w, final_norm_w=final_norm_w,
        w_z=a_in_proj[:, :, :D_INNER].astype(BF16),
        w_xbc=a_in_proj[:, :, D_INNER:D_INNER + CONV_DIM].astype(BF16),
        w_dt=jnp.pad(a_in_proj[:, :, D_INNER + CONV_DIM:], ((0, 0), (0, 0), (0, LANE - N_HEADS_A))).astype(BF16),
        a_conv_w=a_conv_w, a_conv_b=a_conv_b, dt_bias=pad_heads(a_dt_bias), a_log=pad_heads(a_A_log),
        d_x=a_D[:, head_of_lane][:, None, :], a_norm_w=a_norm_w, a_out_proj=a_out_proj.astype(BF16),
        e_mat=(jnp.arange(LANE)[:, None] == head_of_lane[None, :]).astype(F32),
        tri=(jnp.arange(SSD_Q)[:, None] >= jnp.arange(SSD_Q)[None, :]).astype(F32),
        u_tri=(jnp.arange(ATTN_TK)[:, None] >= jnp.arange(ATTN_TK)[None, :]).astype(BF16),
        w_k=w_kv[:, :hb].astype(BF16), w_v=w_kv[:, hb:].astype(BF16),
        b_w_q=b_w_q.astype(BF16), b_w_o=b_w_o.astype(BF16),
        router_wt=router_w.transpose(0, 2, 1), router_b=router_b[:, :, None],
        su_tri=(jnp.arange(TOKEN_TILE)[:, None] < jnp.arange(TOKEN_TILE)[None, :]).astype(BF16),
        moe_w1=moe_w1, moe_b1=moe_b1[:, :, None, :], moe_w2=moe_w2, moe_b2=moe_b2[:, :, None, :],
    )
    c_all = jnp.concatenate([c_prompt, c_sample], axis=0)
    mods = mod_vectors(c_all, mod_w, mod_b[:, None, :])
    kv_mod = mod_vectors(c_all, kv_mod_w[None], kv_mod_b[None, None, :])[0]

    conv0 = jnp.zeros((N_A_LAYERS, bp, CONV_W - 1, CONV_DIM), F32)
    ssm0 = jnp.zeros((N_A_LAYERS, bp, N_HEADS_A, HEAD_DIM_A, D_STATE), F32)
    kv0 = jnp.zeros((bp, 0, N_HEADS_B, HEAD_DIM_B), F32)
    out_p = _trunk(x_prompt, mods[:, :bp], kv_mod[:bp], conv0, ssm0, kv0, kv0, p)
    out_s = _trunk(x_sample, mods[:, bp:], kv_mod[bp:], state_conv, state_ssm, cache_k, cache_v, p)
    return (out_p[0], out_s[0]) + out_p[1:] + out_s[1:]
```

```python
import functools
import math

import jax
import jax.numpy as jnp
from jax import lax
from jax.experimental import pallas as pl
from jax.experimental.pallas import tpu as pltpu

F32 = jnp.float32
BF16 = jnp.bfloat16
HIGHEST = lax.Precision.HIGHEST

D_MODEL = 1024
DEPTH = 4
N_A_LAYERS = 2
D_INNER = 2048
HEAD_DIM_A = 64
N_HEADS_A = 32
D_STATE = 128
N_GROUPS = 4
GROUP_W = D_INNER // N_GROUPS
CONV_W = 4
CONV_DIM = D_INNER + 2 * N_GROUPS * D_STATE
HEAD_DIM_B = 64
N_HEADS_B = 16
N_EXPERTS = 32
TOP_K = 4
D_FF = 1024
SWIGLU_ALPHA = 1.702
SWIGLU_LIMIT = 7.0
NORM_EPS = 1e-6
LOG2_E = 1.4426950408889634
HIDDEN = -1e30

LANE = 128
SSD_Q = 64
TOKEN_TILE = 512
MOE_TILE = 512
ATTN_TK = 256
VMEM_LIMIT = 56 * 1024 * 1024


def _cparams(sem):
    return pltpu.CompilerParams(dimension_semantics=sem, vmem_limit_bytes=VMEM_LIMIT)


def _softplus(x):
    return jnp.maximum(x, 0.0) + jnp.log1p(jnp.exp(-jnp.abs(x)))


def _silu(x):
    return x * jax.nn.sigmoid(x)


def _norm_mod(x, nw, sh, sc):
    ms = jnp.mean(x * x, axis=-1, keepdims=True)
    return (x * lax.rsqrt(ms + NORM_EPS) * nw) * (1.0 + sc) + sh


def _mod_spec(mod3, tile, rows_per_group):
    steps = max(rows_per_group // tile, 1)
    g = mod3.shape[1]
    return pl.BlockSpec((1, g, mod3.shape[2]), lambda i: (i // steps, 0, 0))


def _mod_kernel(c_ref, w_ref, b_ref, o_ref):
    cs = _silu(c_ref[...])
    o_ref[0] = jnp.dot(cs, w_ref[0], precision=HIGHEST, preferred_element_type=F32) + b_ref[0]


def mod_vectors(c, w, b):
    nl, d, n = w.shape
    tn = 1024
    return pl.pallas_call(
        _mod_kernel,
        grid=(nl, n // tn),
        in_specs=[pl.BlockSpec(c.shape, lambda l, j: (0, 0)),
                  pl.BlockSpec((1, d, tn), lambda l, j: (l, 0, j)),
                  pl.BlockSpec((1, 1, tn), lambda l, j: (l, 0, j))],
        out_specs=pl.BlockSpec((1, c.shape[0], tn), lambda l, j: (l, 0, j)),
        out_shape=jax.ShapeDtypeStruct((nl, c.shape[0], n), F32),
        compiler_params=_cparams(("arbitrary", "arbitrary")),
        name="mod_vectors",
    )(c, w, b)


def _nmm_kernel(x_ref, nw_ref, sh_ref, sc_ref, *refs, n_w):
    h = _norm_mod(x_ref[...], nw_ref[...], sh_ref[0], sc_ref[0]).astype(BF16)
    for w_ref, o_ref in zip(refs[:n_w], refs[n_w:]):
        o_ref[...] = jnp.dot(h, w_ref[...], preferred_element_type=F32).astype(o_ref.dtype)


def norm_mod_matmul(x, nw, sh3, sc3, ws, out_dtypes, rows_per_group):
    t, d = x.shape
    tm = min(TOKEN_TILE, t)
    in_specs = [pl.BlockSpec((tm, d), lambda i: (i, 0)),
                pl.BlockSpec((1, d), lambda i: (0, 0)),
                _mod_spec(sh3, tm, rows_per_group), _mod_spec(sc3, tm, rows_per_group)]
    in_specs += [pl.BlockSpec(w.shape, lambda i: (0, 0), pipeline_mode=pl.Buffered(1)) for w in ws]
    return pl.pallas_call(
        functools.partial(_nmm_kernel, n_w=len(ws)),
        grid=(t // tm,),
        in_specs=in_specs,
        out_specs=[pl.BlockSpec((tm, w.shape[1]), lambda i: (i, 0)) for w in ws],
        out_shape=[jax.ShapeDtypeStruct((t, w.shape[1]), dt) for w, dt in zip(ws, out_dtypes)],
        compiler_params=_cparams(("parallel",)),
        name="norm_mod_matmul",
    )(x, nw, sh3, sc3, *ws)


def _mmres_kernel(a_ref, w_ref, x_ref, g_ref, o_ref):
    acc = jnp.dot(a_ref[...], w_ref[...], preferred_element_type=F32)
    o_ref[...] = x_ref[...] + g_ref[0] * acc


def matmul_residual(a, w, x, g3, rows_per_group):
    t, d = x.shape
    tm = min(TOKEN_TILE, t)
    return pl.pallas_call(
        _mmres_kernel,
        grid=(t // tm,),
        in_specs=[pl.BlockSpec((tm, a.shape[1]), lambda i: (i, 0)),
                  pl.BlockSpec(w.shape, lambda i: (0, 0), pipeline_mode=pl.Buffered(1)),
                  pl.BlockSpec((tm, d), lambda i: (i, 0)),
                  _mod_spec(g3, tm, rows_per_group)],
        out_specs=pl.BlockSpec((tm, d), lambda i: (i, 0)),
        out_shape=jax.ShapeDtypeStruct((t, d), F32),
        compiler_params=_cparams(("parallel",)),
        name="matmul_residual",
    )(a, w, x, g3)


def _rms_kernel(x_ref, nw_ref, o_ref):
    x = x_ref[...]
    ms = jnp.mean(x * x, axis=-1, keepdims=True)
    o_ref[...] = x * lax.rsqrt(ms + NORM_EPS) * nw_ref[...]


def final_rmsnorm(x, nw):
    t, d = x.shape
    tm = min(TOKEN_TILE, t)
    return pl.pallas_call(
        _rms_kernel,
        grid=(t // tm,),
        in_specs=[pl.BlockSpec((tm, d), lambda i: (i, 0)), pl.BlockSpec((1, d), lambda i: (0, 0))],
        out_specs=pl.BlockSpec((tm, d), lambda i: (i, 0)),
        out_shape=jax.ShapeDtypeStruct((t, d), F32),
        compiler_params=_cparams(("parallel",)),
        name="final_rmsnorm",
    )(x, nw)


def _router_kernel(x_ref, nw_ref, sh_ref, sc_ref, rwt_ref, rb_ref, su_ref, h_ref, e_ref, g_ref, r_ref, cnt_ref,
                   run_scr):
    @pl.when(pl.program_id(0) == 0)
    def _():
        run_scr[...] = jnp.zeros_like(run_scr)

    h = _norm_mod(x_ref[...], nw_ref[...], sh_ref[0], sc_ref[0])
    h_ref[...] = h.astype(BF16)
    logits = lax.dot_general(rwt_ref[...], h, (((1,), (1,)), ((), ())),
                             precision=HIGHEST, preferred_element_type=F32) + rb_ref[...]
    eid = lax.broadcasted_iota(jnp.int32, logits.shape, 0)
    vals, idxs, hits = [], [], []
    for _ in range(TOP_K):
        m = jnp.max(logits, axis=0, keepdims=True)
        idx = jnp.min(jnp.where(logits == m, eid, N_EXPERTS), axis=0, keepdims=True)
        hit = eid == idx
        vals.append(m)
        idxs.append(idx)
        hits.append(hit)
        logits = jnp.where(hit, -jnp.inf, logits)
    ex = [jnp.exp(v - vals[0]) for v in vals]
    den = ex[0] + ex[1] + ex[2] + ex[3]
    cnt = (hits[0] | hits[1] | hits[2] | hits[3]).astype(F32)
    before = jnp.dot(cnt.astype(BF16), su_ref[...], preferred_element_type=F32) + run_scr[...]
    for k in range(TOP_K):
        e_ref[k:k + 1, :] = idxs[k]
        g_ref[k:k + 1, :] = ex[k] / den
        r_ref[k:k + 1, :] = jnp.sum(jnp.where(hits[k], before, 0.0), axis=0, keepdims=True).astype(jnp.int32)
    run_scr[...] += jnp.sum(cnt, axis=1, keepdims=True)
    cnt_ref[...] = run_scr[...]


def moe_router(x, nw, sh3, sc3, rwt, rb, su, rows_per_group):
    t, d = x.shape
    tm = min(TOKEN_TILE, t)
    return pl.pallas_call(
        _router_kernel,
        grid=(t // tm,),
        in_specs=[pl.BlockSpec((tm, d), lambda i: (i, 0)),
                  pl.BlockSpec((1, d), lambda i: (0, 0)),
                  _mod_spec(sh3, tm, rows_per_group), _mod_spec(sc3, tm, rows_per_group),
                  pl.BlockSpec(rwt.shape, lambda i: (0, 0)),
                  pl.BlockSpec(rb.shape, lambda i: (0, 0)),
                  pl.BlockSpec(su.shape, lambda i: (0, 0))],
        out_specs=[pl.BlockSpec((tm, d), lambda i: (i, 0)),
                   pl.BlockSpec((TOP_K, tm), lambda i: (0, i)),
                   pl.BlockSpec((TOP_K, tm), lambda i: (0, i)),
                   pl.BlockSpec((TOP_K, tm), lambda i: (0, i)),
                   pl.BlockSpec((N_EXPERTS, 1), lambda i: (0, 0))],
        out_shape=[jax.ShapeDtypeStruct((t, d), BF16),
                   jax.ShapeDtypeStruct((TOP_K, t), jnp.int32),
                   jax.ShapeDtypeStruct((TOP_K, t), F32),
                   jax.ShapeDtypeStruct((TOP_K, t), jnp.int32),
                   jax.ShapeDtypeStruct((N_EXPERTS, 1), F32)],
        scratch_shapes=[pltpu.VMEM((N_EXPERTS, 1), F32)],
        compiler_params=_cparams(("arbitrary",)),
        name="moe_router",
    )(x, nw, sh3, sc3, rwt, rb, su)


def _ffn_kernel(be_ref, nu_ref, xs_ref, w1_ref, b1_ref, w2_ref, b2_ref, o_ref, w1_scr, w2_scr):
    i = pl.program_id(0)

    @pl.when(i < nu_ref[0])
    def _():
        @pl.when((i == 0) | (be_ref[i] != be_ref[jnp.maximum(i - 1, 0)]))
        def _():
            w1_scr[...] = w1_ref[0].astype(BF16)
            w2_scr[...] = w2_ref[0].astype(BF16)

        hid = jnp.dot(xs_ref[...], w1_scr[...], preferred_element_type=F32) + b1_ref[0]
        glu = jnp.minimum(hid[:, :D_FF], SWIGLU_LIMIT)
        lin = jnp.clip(hid[:, D_FF:], -SWIGLU_LIMIT, SWIGLU_LIMIT)
        act = glu * jax.nn.sigmoid(SWIGLU_ALPHA * glu) * (lin + 1.0)
        out = jnp.dot(act.astype(BF16), w2_scr[...], preferred_element_type=F32) + b2_ref[0]
        o_ref[...] = out.astype(o_ref.dtype)


def moe_ffn_blocks(xs, block_e, n_used, w1, b1, w2, b2):
    n_rows, d = xs.shape
    tm = MOE_TILE
    grid_spec = pltpu.PrefetchScalarGridSpec(
        num_scalar_prefetch=2,
        grid=(n_rows // tm,),
        in_specs=[pl.BlockSpec((tm, d), lambda i, be, nu: (i, 0)),
                  pl.BlockSpec((1, d, 2 * D_FF), lambda i, be, nu: (be[i], 0, 0)),
                  pl.BlockSpec((1, 1, 2 * D_FF), lambda i, be, nu: (be[i], 0, 0)),
                  pl.BlockSpec((1, D_FF, d), lambda i, be, nu: (be[i], 0, 0)),
                  pl.BlockSpec((1, 1, d), lambda i, be, nu: (be[i], 0, 0))],
        out_specs=pl.BlockSpec((tm, d), lambda i, be, nu: (i, 0)),
        scratch_shapes=[pltpu.VMEM((d, 2 * D_FF), BF16), pltpu.VMEM((D_FF, d), BF16)],
    )
    return pl.pallas_call(
        _ffn_kernel,
        grid_spec=grid_spec,
        out_shape=jax.ShapeDtypeStruct((n_rows, d), BF16),
        compiler_params=_cparams(("arbitrary",)),
        name="moe_ffn",
    )(block_e, n_used, xs, w1, b1, w2, b2)


def _combine_kernel(y_ref, gt_ref, x_ref, g_ref, o_ref):
    tm = x_ref.shape[0]
    eye = lax.broadcasted_iota(jnp.int32, (tm, tm), 0) == lax.broadcasted_iota(jnp.int32, (tm, tm), 1)
    ff = None
    for k in range(TOP_K):
        g_col = jnp.sum(jnp.where(eye, gt_ref[k:k + 1, :], 0.0), axis=1, keepdims=True)
        term = g_col * y_ref[k].astype(F32)
        ff = term if ff is None else ff + term
    o_ref[...] = x_ref[...] + g_ref[0] * ff


def moe_combine(ysg, gates, x, g3, rows_per_group):
    t, d = x.shape
    tm = min(TOKEN_TILE, t)
    return pl.pallas_call(
        _combine_kernel,
        grid=(t // tm,),
        in_specs=[pl.BlockSpec((TOP_K, tm, d), lambda i: (0, i, 0)),
                  pl.BlockSpec((TOP_K, tm), lambda i: (0, i)),
                  pl.BlockSpec((tm, d), lambda i: (i, 0)),
                  _mod_spec(g3, tm, rows_per_group)],
        out_specs=pl.BlockSpec((tm, d), lambda i: (i, 0)),
        out_shape=jax.ShapeDtypeStruct((t, d), F32),
        compiler_params=_cparams(("parallel",)),
        name="moe_combine",
    )(ysg, gates, x, g3)


def _take_rows(a, idx):
    return a.at[idx].get(mode="promise_in_bounds")


def moe_layer(x, nw, sh3, sc3, g3, rwt, rb, su, w1, b1, w2, b2, rows_per_group):
    t, d = x.shape
    h, top_e, gates, rank, cnt = moe_router(x, nw, sh3, sc3, rwt, rb, su, rows_per_group)
    tm = MOE_TILE
    n_assign = t * TOP_K
    n_blocks = (n_assign + N_EXPERTS * (tm - 1) + tm - 1) // tm
    counts = cnt[:, 0].astype(jnp.int32)
    padded = (counts + tm - 1) // tm * tm
    start = jnp.cumsum(counts) - counts
    pend = jnp.cumsum(padded)
    pstart = pend - padded
    block_e = jnp.minimum(jnp.searchsorted(pend, jnp.arange(n_blocks) * tm, side='right'),
                          N_EXPERTS - 1).astype(jnp.int32)
    n_used = (pend[-1] // tm).astype(jnp.int32).reshape(1)
    experts = jnp.arange(N_EXPERTS, dtype=jnp.int32)
    dest = rank + jnp.sum(jnp.where(top_e[:, :, None] == experts, pstart.astype(jnp.int32), 0), axis=-1)
    order = jnp.argsort(top_e.T.reshape(-1)).astype(jnp.int32)
    in_e = (jnp.arange(n_blocks, dtype=jnp.int32) * tm - pstart[block_e].astype(jnp.int32))[:, None] \
        + jnp.arange(tm, dtype=jnp.int32)[None, :]
    src = jnp.where(in_e < counts[block_e][:, None], start[block_e].astype(jnp.int32)[:, None] + in_e, 0)
    row_token = _take_rows(order, src.reshape(-1)) // TOP_K
    xs = _take_rows(h, row_token)
    ys = moe_ffn_blocks(xs, block_e, n_used, w1, b1, w2, b2)
    ysg = _take_rows(ys, dest.reshape(-1)).reshape(TOP_K, t, d)
    return moe_combine(ysg, gates, x, g3, rows_per_group)


def _split3(x):
    hi = x.astype(BF16)
    r1 = x - hi.astype(F32)
    mid = r1.astype(BF16)
    lo = (r1 - mid.astype(F32)).astype(BF16)
    return hi, mid, lo


def _ssd_kernel(z_ref, xbc_ref, dt_ref, cw_ref, cb_ref, dtb_ref, alog_ref, dx_ref, nw_ref, e_ref, tri_ref, shift_ref,
                ic_ref, is_ref, g_ref, so_ref, tail_scr, s_scr, y_scr, *, valid_rows):
    q = SSD_Q
    c = pl.program_id(1)

    @pl.when(c == 0)
    def _():
        tail_scr[...] = ic_ref[0]
        s_scr[...] = is_ref[0]

    xb = xbc_ref[...]
    acc = cb_ref[...] + xb.astype(F32) * cw_ref[CONV_W - 1:CONV_W, :]
    for tap in range(CONV_W - 1):
        shifted = jnp.dot(shift_ref[tap], xb, preferred_element_type=F32)
        acc = acc + shifted * cw_ref[tap:tap + 1, :]
    t0, t1, t2 = (tail_scr[8 - (CONV_W - 1) + j:8 - (CONV_W - 1) + j + 1, :] for j in range(CONV_W - 1))
    w0, w1, w2 = (cw_ref[j:j + 1, :] for j in range(CONV_W - 1))
    row8 = lax.broadcasted_iota(jnp.int32, (8, CONV_DIM), 0)
    head = jnp.where(row8 == 0, t0 * w0 + t1 * w1 + t2 * w2,
                     jnp.where(row8 == 1, t1 * w0 + t2 * w1, jnp.where(row8 == 2, t2 * w0, 0.0)))
    tail_scr[...] = xbc_ref[q - 16:q, :].astype(F32)[8:16, :]
    xc = _silu(jnp.concatenate([acc[0:8, :] + head, acc[8:q, :]], axis=0))
    xs = xc[:, :D_INNER]

    dt = _softplus(dt_ref[...] + dtb_ref[...])
    if valid_rows < q:
        rows = lax.broadcasted_iota(jnp.int32, dt.shape, 0)
        dt = jnp.where(rows < valid_rows, dt, 0.0)
    a = dt * (-jnp.exp(alog_ref[...]))
    a_cs = jnp.dot(tri_ref[...], jnp.concatenate(_split3(a), axis=0), preferred_element_type=F32)
    dtx = jnp.dot(jnp.concatenate(_split3(dt), axis=1), e_ref[...], preferred_element_type=F32)
    acsx = jnp.dot(jnp.concatenate(_split3(a_cs), axis=1), e_ref[...], preferred_element_type=F32)

    row = lax.broadcasted_iota(jnp.int32, (q, D_INNER), 0)
    col = lax.broadcasted_iota(jnp.int32, (q, D_INNER), 1) & (HEAD_DIM_A - 1)
    acs_row = jnp.sum(jnp.where(row == col, acsx, 0.0), axis=0, keepdims=True)
    decay_in = jnp.where(row >= col, jnp.exp(jnp.minimum(acsx - acs_row, 0.0)), 0.0)
    last = acsx[q - 1:q, :]
    e_in = jnp.exp(acsx)
    e_last = jnp.exp(last)
    xd = xs * dtx
    xdd = xd * jnp.exp(last - acsx)
    lane = lax.broadcasted_iota(jnp.int32, (q, LANE), 1)
    lo = lane < HEAD_DIM_A

    for g in range(N_GROUPS):
        gsl = slice(g * GROUP_W, (g + 1) * GROUP_W)
        b_off = D_INNER + g * D_STATE
        c_off = D_INNER + N_GROUPS * D_STATE + g * D_STATE
        bg = xc[:, b_off:b_off + D_STATE]
        cgb = xc[:, c_off:c_off + D_STATE].astype(BF16)
        bgb = bg.astype(BF16)
        b2 = jnp.concatenate([bgb, bgb], axis=0)
        cb2 = lax.dot_general(cgb, b2, (((1,), (1,)), ((), ())), preferred_element_type=F32)
        s_g = s_scr[:, gsl]
        y_off = jnp.dot(cgb, s_g.astype(BF16), preferred_element_type=F32) * e_in[:, gsl]
        for pp in range(GROUP_W // LANE):
            sl = slice(g * GROUP_W + pp * LANE, g * GROUP_W + (pp + 1) * LANE)
            m = (decay_in[:, sl] * cb2).astype(BF16)
            xp = xd[:, sl]
            bd = jnp.concatenate([jnp.where(lo, xp, 0.0), jnp.where(lo, 0.0, xp)], axis=0).astype(BF16)
            y_diag = jnp.dot(m, bd, preferred_element_type=F32)
            y_scr[:, sl] = y_diag + y_off[:, pp * LANE:(pp + 1) * LANE] + xs[:, sl] * dx_ref[:, sl]
        s_new = jnp.dot(bg.T.astype(BF16), xdd[:, gsl].astype(BF16), preferred_element_type=F32)
        s_scr[:, gsl] = s_g * e_last[:, gsl] + s_new

    for g in range(N_GROUPS):
        gsl = slice(g * GROUP_W, (g + 1) * GROUP_W)
        gz = y_scr[:, gsl] * _silu(z_ref[:, gsl].astype(F32))
        ms = jnp.mean(gz * gz, axis=-1, keepdims=True)
        g_ref[:, gsl] = (gz * lax.rsqrt(ms + NORM_EPS) * nw_ref[:, gsl]).astype(g_ref.dtype)

    @pl.when(c == pl.num_programs(1) - 1)
    def _():
        so_ref[0] = s_scr[...]


def ssd_mixer(z, xbc, dt, cw, cb, dtb, alog, dx, nw, e_mat, tri, shift, init_conv, init_ssm, nb, nc, valid_rows):
    q = SSD_Q
    const = lambda a: pl.BlockSpec(a.shape, lambda b, c: (0,) * a.ndim)
    return pl.pallas_call(
        functools.partial(_ssd_kernel, valid_rows=valid_rows),
        grid=(nb, nc),
        in_specs=[pl.BlockSpec((q, D_INNER), lambda b, c: (b * nc + c, 0)),
                  pl.BlockSpec((q, CONV_DIM), lambda b, c: (b * nc + c, 0)),
                  pl.BlockSpec((q, LANE), lambda b, c: (b * nc + c, 0)),
                  const(cw), const(cb), const(dtb), const(alog), const(dx), const(nw), const(e_mat), const(tri),
                  const(shift),
                  pl.BlockSpec((1, 8, CONV_DIM), lambda b, c: (b, 0, 0)),
                  pl.BlockSpec((1, D_STATE, D_INNER), lambda b, c: (b, 0, 0))],
        out_specs=[pl.BlockSpec((q, D_INNER), lambda b, c: (b * nc + c, 0)),
                   pl.BlockSpec((1, D_STATE, D_INNER), lambda b, c: (b, 0, 0))],
        out_shape=[jax.ShapeDtypeStruct((nb * nc * q, D_INNER), BF16),
                   jax.ShapeDtypeStruct((nb, D_STATE, D_INNER), F32)],
        scratch_shapes=[pltpu.VMEM((8, CONV_DIM), F32),
                        pltpu.VMEM((D_STATE, D_INNER), F32),
                        pltpu.VMEM((q, D_INNER), F32)],
        compiler_params=_cparams(("parallel", "arbitrary")),
        name="ssd_mixer",
    )(z, xbc, dt, cw, cb, dtb, alog, dx, nw, e_mat, tri, shift, init_conv, init_ssm)


def _attn_kernel(q_ref, k_ref, v_ref, u_ref, o_ref, qs_scr, acc_scr, c_scr, y_scr, t_scr, tot_scr,
                 *, tq, tk, pos0, n_kblocks):
    qi = pl.program_id(2)
    qv = q_ref[0]
    lane_q = lax.broadcasted_iota(jnp.int32, qv.shape, 1)
    zero_q = jnp.zeros_like(qv)
    qs_scr[0:tq, :] = jnp.where(lane_q < HEAD_DIM_B, qv, zero_q)
    qs_scr[tq:2 * tq, :] = jnp.where(lane_q < HEAD_DIM_B, zero_q, qv)
    acc_scr[...] = jnp.zeros_like(acc_scr)
    c_scr[...] = jnp.zeros_like(c_scr)
    y_scale = LOG2_E / math.sqrt(HEAD_DIM_B)
    q_idx = lax.broadcasted_iota(jnp.int32, (2 * tq, tk), 0) & (tq - 1)
    pos_gap = pos0 + qi * tq + q_idx - lax.broadcasted_iota(jnp.int32, (2 * tq, tk), 1)
    n_vis = jnp.minimum((pos0 + (qi + 1) * tq - 2) // tk + 1, n_kblocks)
    n_full = jnp.minimum((pos0 + qi * tq) // tk, n_kblocks)
    n_masked = n_vis - n_full

    def kblock(i):
        return jnp.maximum(n_vis - 1 - i, 0)

    def scores(i, dst):
        start = pl.multiple_of(kblock(i) * tk, tk)
        kb = k_ref[0, pl.ds(start, tk), :]
        y = lax.dot_general(qs_scr[...], kb, (((1,), (1,)), ((), ())), preferred_element_type=F32)
        y_scr[dst] = y * y_scale

    def suffix(i, masked, src, dst):
        y = y_scr[src]
        sp = jnp.maximum(y, jnp.log(1.0 + jnp.exp2(jnp.minimum(y, 126.0))) * LOG2_E)
        if masked:
            vis = pos_gap > kblock(i) * tk
            sp = jnp.where(vis, sp, 0.0)
        cs = jnp.dot(sp.astype(BF16), u_ref[...], preferred_element_type=F32)
        t = y - cs
        if masked:
            t = jnp.where(vis, t, HIDDEN)
        t_scr[dst] = t
        tot_scr[dst] = cs[:, 0:1]

    def weigh(i, src):
        start = pl.multiple_of(kblock(i) * tk, tk)
        vb = v_ref[0, pl.ds(start, tk), :]
        c = c_scr[...]
        w = jnp.exp2(t_scr[src] - c)
        acc_scr[...] += jnp.dot(w.astype(BF16), vb, preferred_element_type=F32)
        c_scr[...] = c + tot_scr[src]

    scores(0, 0)
    suffix(0, True, 0, 0)
    scores(1, 0)

    def trip(i, masked):
        weigh(i, 0)
        suffix(i + 1, masked, 0, 0)
        scores(i + 2, 0)

    def masked_trip(i, carry):
        trip(i, True)
        return carry

    def full_trip(i, carry):
        trip(i, False)
        return carry

    def full_pair(p, carry):
        i = first_pair + 2 * p
        scores(i + 2, 1)
        suffix(i + 1, False, 0, 1)
        weigh(i, 0)
        scores(i + 3, 0)
        suffix(i + 2, False, 1, 0)
        weigh(i + 1, 1)
        return carry

    n_mt = jnp.maximum(n_masked - 1, 0)
    first_pair = n_mt + ((n_vis - n_mt) & 1)
    lax.fori_loop(0, n_mt, masked_trip, 0)
    lax.fori_loop(n_mt, first_pair, full_trip, 0)
    lax.fori_loop(0, (n_vis - first_pair) // 2, full_pair, 0)
    lane_o = lax.broadcasted_iota(jnp.int32, (tq, LANE), 1)
    o_ref[0] = jnp.where(lane_o < HEAD_DIM_B, acc_scr[0:tq, :], acc_scr[tq:2 * tq, :]).astype(o_ref.dtype)


def stick_breaking(q, k, v, u, pos0, tq):
    b, l, hd = q.shape
    kp = k.shape[1]
    tk = ATTN_TK
    assert tq & (tq - 1) == 0 and l % tq == 0 and kp % tk == 0
    return pl.pallas_call(
        functools.partial(_attn_kernel, tq=tq, tk=tk, pos0=pos0, n_kblocks=kp // tk),
        grid=(b, hd // LANE, l // tq),
        in_specs=[pl.BlockSpec((1, tq, LANE), lambda bi, p, i: (bi, i, p)),
                  pl.BlockSpec((1, kp, LANE), lambda bi, p, i: (bi, 0, p)),
                  pl.BlockSpec((1, kp, LANE), lambda bi, p, i: (bi, 0, p)),
                  pl.BlockSpec(u.shape, lambda bi, p, i: (0, 0))],
        out_specs=pl.BlockSpec((1, tq, LANE), lambda bi, p, i: (bi, i, p)),
        out_shape=jax.ShapeDtypeStruct((b, l, hd), BF16),
        scratch_shapes=[pltpu.VMEM((2 * tq, LANE), BF16), pltpu.VMEM((2 * tq, LANE), F32),
                        pltpu.VMEM((2 * tq, 1), F32), pltpu.VMEM((2, 2 * tq, tk), F32),
                        pltpu.VMEM((2, 2 * tq, tk), F32), pltpu.VMEM((2, 2 * tq, 1), F32)],
        compiler_params=_cparams(("parallel", "parallel", "arbitrary")),
        name="stick_breaking",
    )(q, k, v, u)


def _trunk(x3, mods, kv_mod, conv_state, ssm_state, past_k, past_v, p):
    b, l, d = x3.shape
    t = b * l
    x = x3.reshape(t, d)
    per_token = l < TOKEN_TILE

    def mod3(v):
        if per_token:
            return jnp.repeat(v, l, axis=0).reshape(t // min(TOKEN_TILE, t), min(TOKEN_TILE, t), d)
        return v[:, None, :]

    rows_per_group = min(TOKEN_TILE, t) if per_token else l
    pos0 = past_k.shape[1]
    conv_out, ssm_out = [], []
    k_new = v_new = kb = vb = None
    for layer in range(DEPTH):
        sh1, sc1, g1, sh2, sc2, g2 = jnp.split(mods[layer], 6, axis=-1)
        nw1 = p['mix_norm_w'][layer][None, :]
        if layer < N_A_LAYERS:
            i = layer
            z, xbc, dt = norm_mod_matmul(x, nw1, mod3(sh1), mod3(sc1), [p['w_z'][i], p['w_xbc'][i], p['w_dt'][i]],
                                         [BF16, BF16, F32], rows_per_group)
            xbc3 = xbc.reshape(b, l, CONV_DIM)
            conv_out.append(xbc3[:, l - (CONV_W - 1):].astype(F32))
            nc = max(l // SSD_Q, 1)
            if l < SSD_Q:
                pad = lambda a: jnp.pad(a.reshape(b, l, -1), ((0, 0), (0, SSD_Q - l), (0, 0))).reshape(b * SSD_Q, -1)
                z, xbc, dt = pad(z), pad(xbc), pad(dt)
            ic = jnp.pad(conv_state[i].astype(F32), ((0, 0), (8 - (CONV_W - 1), 0), (0, 0)))
            iss = ssm_state[i].astype(F32).transpose(0, 3, 1, 2).reshape(b, D_STATE, D_INNER)
            gm, s_fin = ssd_mixer(z, xbc, dt, p['a_conv_w'][i], p['a_conv_b'][i][None, :], p['dt_bias'][i],
                                  p['a_log'][i], p['d_x'][i], p['a_norm_w'][i][None, :], p['e_mat'], p['tri'], p['shift'],
                                  ic, iss, b, nc, min(l, SSD_Q))
            if l < SSD_Q:
                gm = gm.reshape(b, SSD_Q, D_INNER)[:, :l].reshape(t, D_INNER)
            ssm_out.append(s_fin.reshape(b, D_STATE, N_HEADS_A, HEAD_DIM_A).transpose(0, 2, 3, 1))
            x = matmul_residual(gm, p['a_out_proj'][i], x, mod3(g1), rows_per_group)
        else:
            j = layer - N_A_LAYERS
            (qp,) = norm_mod_matmul(x, nw1, mod3(sh1), mod3(sc1), [p['b_w_q'][j]], [BF16], rows_per_group)
            o = stick_breaking(qp.reshape(b, l, d), kb, vb, p['u_tri'], pos0, min(l, 256))
            x = matmul_residual(o.reshape(t, d), p['b_w_o'][j], x, mod3(g1), rows_per_group)
        x = moe_layer(x, p['ffn_norm_w'][layer][None, :], mod3(sh2), mod3(sc2), mod3(g2),
                      p['router_wt'][layer], p['router_b'][layer], p['su_tri'], p['moe_w1'][layer],
                      p['moe_b1'][layer], p['moe_w2'][layer], p['moe_b2'][layer], rows_per_group)
        if layer == N_A_LAYERS - 1:
            sh_kv, sc_kv = jnp.split(kv_mod, 2, axis=-1)
            k_new, v_new = norm_mod_matmul(x, p['kv_norm_w'][None, :], mod3(sh_kv), mod3(sc_kv),
                                           [p['w_k'], p['w_v']], [F32, F32], rows_per_group)
            k_all = jnp.concatenate([past_k.reshape(b, pos0, d), k_new.reshape(b, l, d)], axis=1)
            v_all = jnp.concatenate([past_v.reshape(b, pos0, d), v_new.reshape(b, l, d)], axis=1)
            kpad = (-k_all.shape[1]) % ATTN_TK
            kb = jnp.pad(k_all, ((0, 0), (0, kpad), (0, 0))).astype(BF16)
            vb = jnp.pad(v_all, ((0, 0), (0, kpad), (0, 0))).astype(BF16)
    y = final_rmsnorm(x, p['final_norm_w'][None, :])
    return (y.reshape(b, l, d), jnp.stack(conv_out), jnp.stack(ssm_out),
            k_new.reshape(b, l, N_HEADS_B, HEAD_DIM_B), v_new.reshape(b, l, N_HEADS_B, HEAD_DIM_B))


def kernel(x_prompt, x_sample, state_conv, state_ssm, cache_k, cache_v, c_prompt, c_sample, mod_w, mod_b,
           mix_norm_w, ffn_norm_w, a_in_proj, a_conv_w, a_conv_b, a_dt_bias, a_A_log, a_D, a_norm_w, a_out_proj,
           kv_mod_w, kv_mod_b, kv_norm_w, w_kv, b_w_q, b_w_o, router_w, router_b, moe_w1, moe_b1, moe_w2, moe_b2,
           final_norm_w):
    bp = x_prompt.shape[0]
    hb = N_HEADS_B * HEAD_DIM_B
    pad_heads = lambda a: jnp.pad(a, ((0, 0), (0, LANE - N_HEADS_A)))[:, None, :]
    head_of_lane = jnp.arange(D_INNER) // HEAD_DIM_A
    p = dict(
        mix_norm_w=mix_norm_w, ffn_norm_w=ffn_norm_w, kv_norm_w=kv_norm_w, final_norm_w=final_norm_w,
        w_z=a_in_proj[:, :, :D_INNER].astype(BF16),
        w_xbc=a_in_proj[:, :, D_INNER:D_INNER + CONV_DIM].astype(BF16),
        w_dt=jnp.pad(a_in_proj[:, :, D_INNER + CONV_DIM:], ((0, 0), (0, 0), (0, LANE - N_HEADS_A))).astype(BF16),
        a_conv_w=a_conv_w, a_conv_b=a_conv_b, dt_bias=pad_heads(a_dt_bias), a_log=pad_heads(a_A_log),
        d_x=a_D[:, head_of_lane][:, None, :], a_norm_w=a_norm_w, a_out_proj=a_out_proj.astype(BF16),
        e_mat=jnp.tile((jnp.arange(LANE)[:, None] == head_of_lane[None, :]).astype(BF16), (3, 1)),
        tri=jnp.tile((jnp.arange(SSD_Q)[:, None] >= jnp.arange(SSD_Q)[None, :]).astype(BF16), (1, 3)),
        shift=jnp.stack([(jnp.arange(SSD_Q)[:, None] + tap - (CONV_W - 1) == jnp.arange(SSD_Q)[None, :])
                         for tap in range(CONV_W - 1)]).astype(BF16),
        u_tri=(jnp.arange(ATTN_TK)[:, None] >= jnp.arange(ATTN_TK)[None, :]).astype(BF16),
        w_k=w_kv[:, :hb].astype(BF16), w_v=w_kv[:, hb:].astype(BF16),
        b_w_q=b_w_q.astype(BF16), b_w_o=b_w_o.astype(BF16),
        router_wt=router_w.transpose(0, 2, 1), router_b=router_b[:, :, None],
        su_tri=(jnp.arange(TOKEN_TILE)[:, None] < jnp.arange(TOKEN_TILE)[None, :]).astype(BF16),
        moe_w1=moe_w1, moe_b1=moe_b1[:, :, None, :], moe_w2=moe_w2, moe_b2=moe_b2[:, :, None, :],
    )
    c_all = jnp.concatenate([c_prompt, c_sample], axis=0)
    mods = mod_vectors(c_all, mod_w, mod_b[:, None, :])
    kv_mod = mod_vectors(c_all, kv_mod_w[None], kv_mod_b[None, None, :])[0]

    conv0 = jnp.zeros((N_A_LAYERS, bp, CONV_W - 1, CONV_DIM), F32)
    ssm0 = jnp.zeros((N_A_LAYERS, bp, N_HEADS_A, HEAD_DIM_A, D_STATE), F32)
    kv0 = jnp.zeros((bp, 0, N_HEADS_B, HEAD_DIM_B), F32)
    out_p = _trunk(x_prompt, mods[:, :bp], kv_mod[:bp], conv0, ssm0, kv0, kv0, p)
    out_s = _trunk(x_sample, mods[:, bp:], kv_mod[bp:], state_conv, state_ssm, cache_k, cache_v, p)
    return (out_p[0], out_s[0]) + out_p[1:] + out_s[1:]
```

```python
import functools
import math

import jax
import jax.numpy as jnp
from jax import lax
from jax.experimental import pallas as pl
from jax.experimental.pallas import tpu as pltpu

F32 = jnp.float32
BF16 = jnp.bfloat16
HIGHEST = lax.Precision.HIGHEST

D_MODEL = 1024
DEPTH = 4
N_A_LAYERS = 2
D_INNER = 2048
HEAD_DIM_A = 64
N_HEADS_A = 32
D_STATE = 128
N_GROUPS = 4
GROUP_W = D_INNER // N_GROUPS
CONV_W = 4
CONV_DIM = D_INNER + 2 * N_GROUPS * D_STATE
HEAD_DIM_B = 64
N_HEADS_B = 16
N_EXPERTS = 32
TOP_K = 4
D_FF = 1024
SWIGLU_ALPHA = 1.702
SWIGLU_LIMIT = 7.0
NORM_EPS = 1e-6
LOG2_E = 1.4426950408889634
HIDDEN = -1e30

LANE = 128
SSD_Q = 64
TOKEN_TILE = 512
MOE_TILE = 512
ATTN_TK = 256
ATTN_UNROLL = 4
PROMPT_GROUPS = 2
VMEM_LIMIT = 56 * 1024 * 1024


def _cparams(sem):
    return pltpu.CompilerParams(dimension_semantics=sem, vmem_limit_bytes=VMEM_LIMIT)


def _softplus(x):
    return jnp.maximum(x, 0.0) + jnp.log1p(jnp.exp(-jnp.abs(x)))


def _silu(x):
    return x * jax.nn.sigmoid(x)


def _norm_mod(x, nw, sh, sc):
    ms = jnp.mean(x * x, axis=-1, keepdims=True)
    return (x * lax.rsqrt(ms + NORM_EPS) * nw) * (1.0 + sc) + sh


def _mod_spec(mod3, tile, rows_per_group):
    steps = max(rows_per_group // tile, 1)
    g = mod3.shape[1]
    return pl.BlockSpec((1, g, mod3.shape[2]), lambda i: (i // steps, 0, 0))


def _mod_kernel(c_ref, w_ref, b_ref, o_ref):
    cs = _silu(c_ref[...])
    o_ref[0] = jnp.dot(cs, w_ref[0], precision=HIGHEST, preferred_element_type=F32) + b_ref[0]


def mod_vectors(c, w, b):
    nl, d, n = w.shape
    tn = 1024
    return pl.pallas_call(
        _mod_kernel,
        grid=(nl, n // tn),
        in_specs=[pl.BlockSpec(c.shape, lambda l, j: (0, 0)),
                  pl.BlockSpec((1, d, tn), lambda l, j: (l, 0, j)),
                  pl.BlockSpec((1, 1, tn), lambda l, j: (l, 0, j))],
        out_specs=pl.BlockSpec((1, c.shape[0], tn), lambda l, j: (l, 0, j)),
        out_shape=jax.ShapeDtypeStruct((nl, c.shape[0], n), F32),
        compiler_params=_cparams(("arbitrary", "arbitrary")),
        name="mod_vectors",
    )(c, w, b)


def _nmm_kernel(x_ref, nw_ref, sh_ref, sc_ref, *refs, n_w):
    h = _norm_mod(x_ref[...], nw_ref[...], sh_ref[0], sc_ref[0]).astype(BF16)
    for w_ref, o_ref in zip(refs[:n_w], refs[n_w:]):
        o_ref[...] = jnp.dot(h, w_ref[...], preferred_element_type=F32).astype(o_ref.dtype)


def norm_mod_matmul(x, nw, sh3, sc3, ws, out_dtypes, rows_per_group):
    t, d = x.shape
    tm = min(TOKEN_TILE, t)
    in_specs = [pl.BlockSpec((tm, d), lambda i: (i, 0)),
                pl.BlockSpec((1, d), lambda i: (0, 0)),
                _mod_spec(sh3, tm, rows_per_group), _mod_spec(sc3, tm, rows_per_group)]
    in_specs += [pl.BlockSpec(w.shape, lambda i: (0, 0), pipeline_mode=pl.Buffered(1)) for w in ws]
    return pl.pallas_call(
        functools.partial(_nmm_kernel, n_w=len(ws)),
        grid=(t // tm,),
        in_specs=in_specs,
        out_specs=[pl.BlockSpec((tm, w.shape[1]), lambda i: (i, 0)) for w in ws],
        out_shape=[jax.ShapeDtypeStruct((t, w.shape[1]), dt) for w, dt in zip(ws, out_dtypes)],
        compiler_params=_cparams(("parallel",)),
        name="norm_mod_matmul",
    )(x, nw, sh3, sc3, *ws)


def _mmres_kernel(a_ref, w_ref, x_ref, g_ref, o_ref):
    acc = jnp.dot(a_ref[...], w_ref[...], preferred_element_type=F32)
    o_ref[...] = x_ref[...] + g_ref[0] * acc


def matmul_residual(a, w, x, g3, rows_per_group):
    t, d = x.shape
    tm = min(TOKEN_TILE, t)
    return pl.pallas_call(
        _mmres_kernel,
        grid=(t // tm,),
        in_specs=[pl.BlockSpec((tm, a.shape[1]), lambda i: (i, 0)),
                  pl.BlockSpec(w.shape, lambda i: (0, 0), pipeline_mode=pl.Buffered(1)),
                  pl.BlockSpec((tm, d), lambda i: (i, 0)),
                  _mod_spec(g3, tm, rows_per_group)],
        out_specs=pl.BlockSpec((tm, d), lambda i: (i, 0)),
        out_shape=jax.ShapeDtypeStruct((t, d), F32),
        compiler_params=_cparams(("parallel",)),
        name="matmul_residual",
    )(a, w, x, g3)


def _rms_kernel(x_ref, nw_ref, o_ref):
    x = x_ref[...]
    ms = jnp.mean(x * x, axis=-1, keepdims=True)
    o_ref[...] = x * lax.rsqrt(ms + NORM_EPS) * nw_ref[...]


def final_rmsnorm(x, nw):
    t, d = x.shape
    tm = min(TOKEN_TILE, t)
    return pl.pallas_call(
        _rms_kernel,
        grid=(t // tm,),
        in_specs=[pl.BlockSpec((tm, d), lambda i: (i, 0)), pl.BlockSpec((1, d), lambda i: (0, 0))],
        out_specs=pl.BlockSpec((tm, d), lambda i: (i, 0)),
        out_shape=jax.ShapeDtypeStruct((t, d), F32),
        compiler_params=_cparams(("parallel",)),
        name="final_rmsnorm",
    )(x, nw)


def _router_kernel(x_ref, nw_ref, sh_ref, sc_ref, rwt_ref, rb_ref, su_ref, h_ref, e_ref, g_ref, r_ref, cnt_ref,
                   run_scr):
    @pl.when(pl.program_id(0) == 0)
    def _():
        run_scr[...] = jnp.zeros_like(run_scr)

    h = _norm_mod(x_ref[...], nw_ref[...], sh_ref[0], sc_ref[0])
    h_ref[...] = h.astype(BF16)
    logits = lax.dot_general(rwt_ref[...], h, (((1,), (1,)), ((), ())),
                             precision=HIGHEST, preferred_element_type=F32) + rb_ref[...]
    eid = lax.broadcasted_iota(jnp.int32, logits.shape, 0)
    vals, idxs, hits = [], [], []
    for _ in range(TOP_K):
        m = jnp.max(logits, axis=0, keepdims=True)
        idx = jnp.min(jnp.where(logits == m, eid, N_EXPERTS), axis=0, keepdims=True)
        hit = eid == idx
        vals.append(m)
        idxs.append(idx)
        hits.append(hit)
        logits = jnp.where(hit, -jnp.inf, logits)
    ex = [jnp.exp(v - vals[0]) for v in vals]
    den = ex[0] + ex[1] + ex[2] + ex[3]
    cnt = (hits[0] | hits[1] | hits[2] | hits[3]).astype(F32)
    before = jnp.dot(cnt.astype(BF16), su_ref[...], preferred_element_type=F32) + run_scr[...]
    for k in range(TOP_K):
        e_ref[k:k + 1, :] = idxs[k]
        g_ref[k:k + 1, :] = ex[k] / den
        r_ref[k:k + 1, :] = jnp.sum(jnp.where(hits[k], before, 0.0), axis=0, keepdims=True).astype(jnp.int32)
    run_scr[...] += jnp.sum(cnt, axis=1, keepdims=True)
    cnt_ref[...] = run_scr[...]


def moe_router(x, nw, sh3, sc3, rwt, rb, su, rows_per_group):
    t, d = x.shape
    tm = min(TOKEN_TILE, t)
    return pl.pallas_call(
        _router_kernel,
        grid=(t // tm,),
        in_specs=[pl.BlockSpec((tm, d), lambda i: (i, 0)),
                  pl.BlockSpec((1, d), lambda i: (0, 0)),
                  _mod_spec(sh3, tm, rows_per_group), _mod_spec(sc3, tm, rows_per_group),
                  pl.BlockSpec(rwt.shape, lambda i: (0, 0)),
                  pl.BlockSpec(rb.shape, lambda i: (0, 0)),
                  pl.BlockSpec(su.shape, lambda i: (0, 0))],
        out_specs=[pl.BlockSpec((tm, d), lambda i: (i, 0)),
                   pl.BlockSpec((TOP_K, tm), lambda i: (0, i)),
                   pl.BlockSpec((TOP_K, tm), lambda i: (0, i)),
                   pl.BlockSpec((TOP_K, tm), lambda i: (0, i)),
                   pl.BlockSpec((N_EXPERTS, 1), lambda i: (0, 0))],
        out_shape=[jax.ShapeDtypeStruct((t, d), BF16),
                   jax.ShapeDtypeStruct((TOP_K, t), jnp.int32),
                   jax.ShapeDtypeStruct((TOP_K, t), F32),
                   jax.ShapeDtypeStruct((TOP_K, t), jnp.int32),
                   jax.ShapeDtypeStruct((N_EXPERTS, 1), F32)],
        scratch_shapes=[pltpu.VMEM((N_EXPERTS, 1), F32)],
        compiler_params=_cparams(("arbitrary",)),
        name="moe_router",
    )(x, nw, sh3, sc3, rwt, rb, su)


def _ffn_kernel(be_ref, nu_ref, xs_ref, w1_ref, b1_ref, w2_ref, b2_ref, o_ref, w1_scr, w2_scr):
    i = pl.program_id(0)

    @pl.when(i < nu_ref[0])
    def _():
        @pl.when((i == 0) | (be_ref[i] != be_ref[jnp.maximum(i - 1, 0)]))
        def _():
            w1_scr[...] = w1_ref[0, 0].astype(BF16)
            w2_scr[...] = w2_ref[0, 0].astype(BF16)

        hid = jnp.dot(xs_ref[...], w1_scr[...], preferred_element_type=F32) + b1_ref[0, 0]
        glu = jnp.minimum(hid[:, :D_FF], SWIGLU_LIMIT)
        lin = jnp.clip(hid[:, D_FF:], -SWIGLU_LIMIT, SWIGLU_LIMIT)
        act = glu * jax.nn.sigmoid(SWIGLU_ALPHA * glu) * (lin + 1.0)
        out = jnp.dot(act.astype(BF16), w2_scr[...], preferred_element_type=F32) + b2_ref[0, 0]
        o_ref[...] = out.astype(o_ref.dtype)


def moe_ffn_blocks(xs, block_e, n_used, w1, b1, w2, b2, layer):
    n_rows, d = xs.shape
    tm = MOE_TILE
    grid_spec = pltpu.PrefetchScalarGridSpec(
        num_scalar_prefetch=2,
        grid=(n_rows // tm,),
        in_specs=[pl.BlockSpec((tm, d), lambda i, be, nu: (i, 0)),
                  pl.BlockSpec((1, 1, d, 2 * D_FF), lambda i, be, nu: (layer, be[i], 0, 0)),
                  pl.BlockSpec((1, 1, 1, 2 * D_FF), lambda i, be, nu: (layer, be[i], 0, 0)),
                  pl.BlockSpec((1, 1, D_FF, d), lambda i, be, nu: (layer, be[i], 0, 0)),
                  pl.BlockSpec((1, 1, 1, d), lambda i, be, nu: (layer, be[i], 0, 0))],
        out_specs=pl.BlockSpec((tm, d), lambda i, be, nu: (i, 0)),
        scratch_shapes=[pltpu.VMEM((d, 2 * D_FF), BF16), pltpu.VMEM((D_FF, d), BF16)],
    )
    return pl.pallas_call(
        _ffn_kernel,
        grid_spec=grid_spec,
        out_shape=jax.ShapeDtypeStruct((n_rows, d), BF16),
        compiler_params=_cparams(("arbitrary",)),
        name="moe_ffn",
    )(block_e, n_used, xs, w1, b1, w2, b2)


def _combine_kernel(y_ref, gt_ref, x_ref, g_ref, o_ref):
    tm = x_ref.shape[0]
    eye = lax.broadcasted_iota(jnp.int32, (tm, tm), 0) == lax.broadcasted_iota(jnp.int32, (tm, tm), 1)
    ff = None
    for k in range(TOP_K):
        g_col = jnp.sum(jnp.where(eye, gt_ref[k:k + 1, :], 0.0), axis=1, keepdims=True)
        term = g_col * y_ref[k].astype(F32)
        ff = term if ff is None else ff + term
    o_ref[...] = x_ref[...] + g_ref[0] * ff


def moe_combine(ysg, gates, x, g3, rows_per_group):
    t, d = x.shape
    tm = min(TOKEN_TILE, t)
    return pl.pallas_call(
        _combine_kernel,
        grid=(t // tm,),
        in_specs=[pl.BlockSpec((TOP_K, tm, d), lambda i: (0, i, 0)),
                  pl.BlockSpec((TOP_K, tm), lambda i: (0, i)),
                  pl.BlockSpec((tm, d), lambda i: (i, 0)),
                  _mod_spec(g3, tm, rows_per_group)],
        out_specs=pl.BlockSpec((tm, d), lambda i: (i, 0)),
        out_shape=jax.ShapeDtypeStruct((t, d), F32),
        compiler_params=_cparams(("parallel",)),
        name="moe_combine",
    )(ysg, gates, x, g3)


def _take_rows(a, idx):
    return a.at[idx].get(mode="promise_in_bounds")


def moe_layer(x, nw, sh3, sc3, g3, rwt, rb, su, w1, b1, w2, b2, layer, rows_per_group):
    t, d = x.shape
    h, top_e, gates, rank, cnt = moe_router(x, nw, sh3, sc3, rwt, rb, su, rows_per_group)
    tm = MOE_TILE
    n_assign = t * TOP_K
    n_blocks = (n_assign + N_EXPERTS * (tm - 1) + tm - 1) // tm
    counts = cnt[:, 0].astype(jnp.int32)
    padded = (counts + tm - 1) // tm * tm
    start = jnp.cumsum(counts) - counts
    pend = jnp.cumsum(padded)
    pstart = pend - padded
    block_e = jnp.minimum(jnp.searchsorted(pend, jnp.arange(n_blocks) * tm, side='right'),
                          N_EXPERTS - 1).astype(jnp.int32)
    n_used = (pend[-1] // tm).astype(jnp.int32).reshape(1)
    experts = jnp.arange(N_EXPERTS, dtype=jnp.int32)
    dest = rank + jnp.sum(jnp.where(top_e[:, :, None] == experts, pstart.astype(jnp.int32), 0), axis=-1)
    order = jnp.argsort(top_e.T.reshape(-1)).astype(jnp.int32)
    in_e = (jnp.arange(n_blocks, dtype=jnp.int32) * tm - pstart[block_e].astype(jnp.int32))[:, None] \
        + jnp.arange(tm, dtype=jnp.int32)[None, :]
    src = jnp.where(in_e < counts[block_e][:, None], start[block_e].astype(jnp.int32)[:, None] + in_e, 0)
    row_token = _take_rows(order, src.reshape(-1)) // TOP_K
    xs = _take_rows(h, row_token)
    ys = moe_ffn_blocks(xs, block_e, n_used, w1, b1, w2, b2, layer)
    ysg = _take_rows(ys, dest.reshape(-1)).reshape(TOP_K, t, d)
    return moe_combine(ysg, gates, x, g3, rows_per_group)


def _split3(x):
    hi = x.astype(BF16)
    r1 = x - hi.astype(F32)
    mid = r1.astype(BF16)
    lo = (r1 - mid.astype(F32)).astype(BF16)
    return hi, mid, lo


def _ssd_kernel(z_ref, xbc_ref, dt_ref, cw_ref, cb_ref, dtb_ref, alog_ref, dx_ref, nw_ref, e_ref, tri_ref, shift_ref,
                ic_ref, is_ref, g_ref, so_ref, tail_scr, s_scr, y_scr, *, valid_rows):
    q = SSD_Q
    c = pl.program_id(1)

    @pl.when(c == 0)
    def _():
        tail_scr[...] = ic_ref[0]
        s_scr[...] = is_ref[0]

    xb = xbc_ref[...]
    acc = cb_ref[...] + xb.astype(F32) * cw_ref[CONV_W - 1:CONV_W, :]
    for tap in range(CONV_W - 1):
        shifted = jnp.dot(shift_ref[tap], xb, preferred_element_type=F32)
        acc = acc + shifted * cw_ref[tap:tap + 1, :]
    t0, t1, t2 = (tail_scr[8 - (CONV_W - 1) + j:8 - (CONV_W - 1) + j + 1, :] for j in range(CONV_W - 1))
    w0, w1, w2 = (cw_ref[j:j + 1, :] for j in range(CONV_W - 1))
    row8 = lax.broadcasted_iota(jnp.int32, (8, CONV_DIM), 0)
    head = jnp.where(row8 == 0, t0 * w0 + t1 * w1 + t2 * w2,
                     jnp.where(row8 == 1, t1 * w0 + t2 * w1, jnp.where(row8 == 2, t2 * w0, 0.0)))
    tail_scr[...] = xbc_ref[q - 16:q, :].astype(F32)[8:16, :]
    xc = _silu(jnp.concatenate([acc[0:8, :] + head, acc[8:q, :]], axis=0))
    xs = xc[:, :D_INNER]

    dt = _softplus(dt_ref[...] + dtb_ref[...])
    if valid_rows < q:
        rows = lax.broadcasted_iota(jnp.int32, dt.shape, 0)
        dt = jnp.where(rows < valid_rows, dt, 0.0)
    a = dt * (-jnp.exp(alog_ref[...]))
    a_cs = jnp.dot(tri_ref[...], jnp.concatenate(_split3(a), axis=0), preferred_element_type=F32)
    dtx = jnp.dot(jnp.concatenate(_split3(dt), axis=1), e_ref[...], preferred_element_type=F32)
    acsx = jnp.dot(jnp.concatenate(_split3(a_cs), axis=1), e_ref[...], preferred_element_type=F32)

    row = lax.broadcasted_iota(jnp.int32, (q, D_INNER), 0)
    col = lax.broadcasted_iota(jnp.int32, (q, D_INNER), 1) & (HEAD_DIM_A - 1)
    acs_row = jnp.sum(jnp.where(row == col, acsx, 0.0), axis=0, keepdims=True)
    decay_in = jnp.where(row >= col, jnp.exp(jnp.minimum(acsx - acs_row, 0.0)), 0.0)
    last = acsx[q - 1:q, :]
    e_in = jnp.exp(acsx)
    e_last = jnp.exp(last)
    xd = xs * dtx
    xdd = xd * jnp.exp(last - acsx)
    lane = lax.broadcasted_iota(jnp.int32, (q, LANE), 1)
    lo = lane < HEAD_DIM_A

    for g in range(N_GROUPS):
        gsl = slice(g * GROUP_W, (g + 1) * GROUP_W)
        b_off = D_INNER + g * D_STATE
        c_off = D_INNER + N_GROUPS * D_STATE + g * D_STATE
        bg = xc[:, b_off:b_off + D_STATE]
        cgb = xc[:, c_off:c_off + D_STATE].astype(BF16)
        bgb = bg.astype(BF16)
        b2 = jnp.concatenate([bgb, bgb], axis=0)
        cb2 = lax.dot_general(cgb, b2, (((1,), (1,)), ((), ())), preferred_element_type=F32)
        s_g = s_scr[:, gsl]
        y_off = jnp.dot(cgb, s_g.astype(BF16), preferred_element_type=F32) * e_in[:, gsl]
        for pp in range(GROUP_W // LANE):
            sl = slice(g * GROUP_W + pp * LANE, g * GROUP_W + (pp + 1) * LANE)
            m = (decay_in[:, sl] * cb2).astype(BF16)
            xp = xd[:, sl]
            bd = jnp.concatenate([jnp.where(lo, xp, 0.0), jnp.where(lo, 0.0, xp)], axis=0).astype(BF16)
            y_diag = jnp.dot(m, bd, preferred_element_type=F32)
            y_scr[:, sl] = y_diag + y_off[:, pp * LANE:(pp + 1) * LANE] + xs[:, sl] * dx_ref[:, sl]
        s_new = jnp.dot(bg.T.astype(BF16), xdd[:, gsl].astype(BF16), preferred_element_type=F32)
        s_scr[:, gsl] = s_g * e_last[:, gsl] + s_new

    for g in range(N_GROUPS):
        gsl = slice(g * GROUP_W, (g + 1) * GROUP_W)
        gz = y_scr[:, gsl] * _silu(z_ref[:, gsl].astype(F32))
        ms = jnp.mean(gz * gz, axis=-1, keepdims=True)
        g_ref[:, gsl] = (gz * lax.rsqrt(ms + NORM_EPS) * nw_ref[:, gsl]).astype(g_ref.dtype)

    @pl.when(c == pl.num_programs(1) - 1)
    def _():
        so_ref[0] = s_scr[...]


def ssd_mixer(z, xbc, dt, cw, cb, dtb, alog, dx, nw, e_mat, tri, shift, init_conv, init_ssm, nb, nc, valid_rows):
    q = SSD_Q
    const = lambda a: pl.BlockSpec(a.shape, lambda b, c: (0,) * a.ndim)
    return pl.pallas_call(
        functools.partial(_ssd_kernel, valid_rows=valid_rows),
        grid=(nb, nc),
        in_specs=[pl.BlockSpec((q, D_INNER), lambda b, c: (b * nc + c, 0)),
                  pl.BlockSpec((q, CONV_DIM), lambda b, c: (b * nc + c, 0)),
                  pl.BlockSpec((q, LANE), lambda b, c: (b * nc + c, 0)),
                  const(cw), const(cb), const(dtb), const(alog), const(dx), const(nw), const(e_mat), const(tri),
                  const(shift),
                  pl.BlockSpec((1, 8, CONV_DIM), lambda b, c: (b, 0, 0)),
                  pl.BlockSpec((1, D_STATE, D_INNER), lambda b, c: (b, 0, 0))],
        out_specs=[pl.BlockSpec((q, D_INNER), lambda b, c: (b * nc + c, 0)),
                   pl.BlockSpec((1, D_STATE, D_INNER), lambda b, c: (b, 0, 0))],
        out_shape=[jax.ShapeDtypeStruct((nb * nc * q, D_INNER), BF16),
                   jax.ShapeDtypeStruct((nb, D_STATE, D_INNER), F32)],
        scratch_shapes=[pltpu.VMEM((8, CONV_DIM), F32),
                        pltpu.VMEM((D_STATE, D_INNER), F32),
                        pltpu.VMEM((q, D_INNER), F32)],
        compiler_params=_cparams(("parallel", "arbitrary")),
        name="ssd_mixer",
    )(z, xbc, dt, cw, cb, dtb, alog, dx, nw, e_mat, tri, shift, init_conv, init_ssm)


def _attn_kernel(q_ref, k_ref, v_ref, u_ref, o_ref, qs_scr, acc_scr, c_scr, y_scr, t_scr, tot_scr,
                 *, tq, tk, pos0, n_kblocks):
    qi = pl.program_id(2)
    qv = q_ref[0]
    lane_q = lax.broadcasted_iota(jnp.int32, qv.shape, 1)
    zero_q = jnp.zeros_like(qv)
    qs_scr[0:tq, :] = jnp.where(lane_q < HEAD_DIM_B, qv, zero_q)
    qs_scr[tq:2 * tq, :] = jnp.where(lane_q < HEAD_DIM_B, zero_q, qv)
    acc_scr[...] = jnp.zeros_like(acc_scr)
    c_scr[...] = jnp.zeros_like(c_scr)
    y_scale = LOG2_E / math.sqrt(HEAD_DIM_B)
    q_idx = lax.broadcasted_iota(jnp.int32, (2 * tq, tk), 0) & (tq - 1)
    pos_gap = pos0 + qi * tq + q_idx - lax.broadcasted_iota(jnp.int32, (2 * tq, tk), 1)
    n_vis = jnp.minimum((pos0 + (qi + 1) * tq - 2) // tk + 1, n_kblocks)
    n_full = jnp.minimum((pos0 + qi * tq) // tk, n_kblocks)
    n_masked = n_vis - n_full

    def kblock(i):
        return jnp.maximum(n_vis - 1 - i, 0)

    def scores(i, dst):
        start = pl.multiple_of(kblock(i) * tk, tk)
        kb = k_ref[0, pl.ds(start, tk), :]
        y = lax.dot_general(qs_scr[...], kb, (((1,), (1,)), ((), ())), preferred_element_type=F32)
        y_scr[dst] = y * y_scale

    def suffix(i, masked, src, dst):
        y = y_scr[src]
        sp = jnp.maximum(y, jnp.log(1.0 + jnp.exp2(jnp.minimum(y, 126.0))) * LOG2_E)
        if masked:
            vis = pos_gap > kblock(i) * tk
            sp = jnp.where(vis, sp, 0.0)
        cs = jnp.dot(sp.astype(BF16), u_ref[...], preferred_element_type=F32)
        t = y - cs
        if masked:
            t = jnp.where(vis, t, HIDDEN)
        t_scr[dst] = t
        tot_scr[dst] = cs[:, 0:1]

    def weigh(i, src):
        start = pl.multiple_of(kblock(i) * tk, tk)
        vb = v_ref[0, pl.ds(start, tk), :]
        c = c_scr[...]
        w = jnp.exp2(t_scr[src] - c)
        acc_scr[...] += jnp.dot(w.astype(BF16), vb, preferred_element_type=F32)
        c_scr[...] = c + tot_scr[src]

    scores(0, 0)
    suffix(0, True, 0, 0)
    scores(1, 0)

    def trip(i, masked):
        weigh(i, 0)
        suffix(i + 1, masked, 0, 0)
        scores(i + 2, 0)

    def masked_trip(i, carry):
        trip(i, True)
        return carry

    def full_trip(i, carry):
        trip(i, False)
        return carry

    def group_body(first, size):
        def body(p, carry):
            i = first + size * p
            for j in range(size):
                cur, nxt = j & 1, (j + 1) & 1
                scores(i + j + 2, nxt)
                suffix(i + j + 1, False, cur, nxt)
                weigh(i + j, cur)
            return carry
        return body

    n_mt = jnp.maximum(n_masked - 1, 0)
    first_pair = n_mt + ((n_vis - n_mt) & 1)
    first_group = first_pair + ((n_vis - first_pair) % ATTN_UNROLL)
    lax.fori_loop(0, n_mt, masked_trip, 0)
    lax.fori_loop(n_mt, first_pair, full_trip, 0)
    lax.fori_loop(0, (first_group - first_pair) // 2, group_body(first_pair, 2), 0)
    lax.fori_loop(0, (n_vis - first_group) // ATTN_UNROLL, group_body(first_group, ATTN_UNROLL), 0)
    lane_o = lax.broadcasted_iota(jnp.int32, (tq, LANE), 1)
    o_ref[0] = jnp.where(lane_o < HEAD_DIM_B, acc_scr[0:tq, :], acc_scr[tq:2 * tq, :]).astype(o_ref.dtype)


def stick_breaking(q, k, v, u, pos0, tq):
    b, l, hd = q.shape
    kp = k.shape[1]
    tk = ATTN_TK
    assert tq & (tq - 1) == 0 and l % tq == 0 and kp % tk == 0
    return pl.pallas_call(
        functools.partial(_attn_kernel, tq=tq, tk=tk, pos0=pos0, n_kblocks=kp // tk),
        grid=(b, hd // LANE, l // tq),
        in_specs=[pl.BlockSpec((1, tq, LANE), lambda bi, p, i: (bi, i, p)),
                  pl.BlockSpec((1, kp, LANE), lambda bi, p, i: (bi, 0, p)),
                  pl.BlockSpec((1, kp, LANE), lambda bi, p, i: (bi, 0, p)),
                  pl.BlockSpec(u.shape, lambda bi, p, i: (0, 0))],
        out_specs=pl.BlockSpec((1, tq, LANE), lambda bi, p, i: (bi, i, p)),
        out_shape=jax.ShapeDtypeStruct((b, l, hd), BF16),
        scratch_shapes=[pltpu.VMEM((2 * tq, LANE), BF16), pltpu.VMEM((2 * tq, LANE), F32),
                        pltpu.VMEM((2 * tq, 1), F32), pltpu.VMEM((2, 2 * tq, tk), F32),
                        pltpu.VMEM((2, 2 * tq, tk), F32), pltpu.VMEM((2, 2 * tq, 1), F32)],
        compiler_params=_cparams(("parallel", "parallel", "arbitrary")),
        name="stick_breaking",
    )(q, k, v, u)


def _trunk(x3, mods, kv_mod, conv_state, ssm_state, past_k, past_v, p):
    b, l, d = x3.shape
    t = b * l
    x = x3.reshape(t, d)
    per_token = l < TOKEN_TILE

    def mod3(v):
        if per_token:
            return jnp.repeat(v, l, axis=0).reshape(t // min(TOKEN_TILE, t), min(TOKEN_TILE, t), d)
        return v[:, None, :]

    rows_per_group = min(TOKEN_TILE, t) if per_token else l
    pos0 = past_k.shape[1]
    conv_out, ssm_out = [], []
    k_new = v_new = kb = vb = None
    for layer in range(DEPTH):
        sh1, sc1, g1, sh2, sc2, g2 = jnp.split(mods[layer], 6, axis=-1)
        nw1 = p['mix_norm_w'][layer][None, :]
        if layer < N_A_LAYERS:
            i = layer
            z, xbc, dt = norm_mod_matmul(x, nw1, mod3(sh1), mod3(sc1), [p['w_z'][i], p['w_xbc'][i], p['w_dt'][i]],
                                         [BF16, BF16, F32], rows_per_group)
            xbc3 = xbc.reshape(b, l, CONV_DIM)
            conv_out.append(xbc3[:, l - (CONV_W - 1):].astype(F32))
            nc = max(l // SSD_Q, 1)
            if l < SSD_Q:
                pad = lambda a: jnp.pad(a.reshape(b, l, -1), ((0, 0), (0, SSD_Q - l), (0, 0))).reshape(b * SSD_Q, -1)
                z, xbc, dt = pad(z), pad(xbc), pad(dt)
            ic = jnp.pad(conv_state[i].astype(F32), ((0, 0), (8 - (CONV_W - 1), 0), (0, 0)))
            iss = ssm_state[i].astype(F32).transpose(0, 3, 1, 2).reshape(b, D_STATE, D_INNER)
            gm, s_fin = ssd_mixer(z, xbc, dt, p['a_conv_w'][i], p['a_conv_b'][i][None, :], p['dt_bias'][i],
                                  p['a_log'][i], p['d_x'][i], p['a_norm_w'][i][None, :], p['e_mat'], p['tri'], p['shift'],
                                  ic, iss, b, nc, min(l, SSD_Q))
            if l < SSD_Q:
                gm = gm.reshape(b, SSD_Q, D_INNER)[:, :l].reshape(t, D_INNER)
            ssm_out.append(s_fin.reshape(b, D_STATE, N_HEADS_A, HEAD_DIM_A).transpose(0, 2, 3, 1))
            x = matmul_residual(gm, p['a_out_proj'][i], x, mod3(g1), rows_per_group)
        else:
            j = layer - N_A_LAYERS
            (qp,) = norm_mod_matmul(x, nw1, mod3(sh1), mod3(sc1), [p['b_w_q'][j]], [BF16], rows_per_group)
            o = stick_breaking(qp.reshape(b, l, d), kb, vb, p['u_tri'], pos0, min(l, 256))
            x = matmul_residual(o.reshape(t, d), p['b_w_o'][j], x, mod3(g1), rows_per_group)
        x = moe_layer(x, p['ffn_norm_w'][layer][None, :], mod3(sh2), mod3(sc2), mod3(g2),
                      p['router_wt'][layer], p['router_b'][layer], p['su_tri'], p['moe_w1'], p['moe_b1'],
                      p['moe_w2'], p['moe_b2'], layer, rows_per_group)
        if layer == N_A_LAYERS - 1:
            sh_kv, sc_kv = jnp.split(kv_mod, 2, axis=-1)
            k_new, v_new = norm_mod_matmul(x, p['kv_norm_w'][None, :], mod3(sh_kv), mod3(sc_kv),
                                           [p['w_k'], p['w_v']], [F32, F32], rows_per_group)
            k_all = jnp.concatenate([past_k.reshape(b, pos0, d), k_new.reshape(b, l, d)], axis=1)
            v_all = jnp.concatenate([past_v.reshape(b, pos0, d), v_new.reshape(b, l, d)], axis=1)
            kpad = (-k_all.shape[1]) % ATTN_TK
            kb = jnp.pad(k_all, ((0, 0), (0, kpad), (0, 0))).astype(BF16)
            vb = jnp.pad(v_all, ((0, 0), (0, kpad), (0, 0))).astype(BF16)
    y = final_rmsnorm(x, p['final_norm_w'][None, :])
    return (y.reshape(b, l, d), jnp.stack(conv_out), jnp.stack(ssm_out),
            k_new.reshape(b, l, N_HEADS_B, HEAD_DIM_B), v_new.reshape(b, l, N_HEADS_B, HEAD_DIM_B))


def kernel(x_prompt, x_sample, state_conv, state_ssm, cache_k, cache_v, c_prompt, c_sample, mod_w, mod_b,
           mix_norm_w, ffn_norm_w, a_in_proj, a_conv_w, a_conv_b, a_dt_bias, a_A_log, a_D, a_norm_w, a_out_proj,
           kv_mod_w, kv_mod_b, kv_norm_w, w_kv, b_w_q, b_w_o, router_w, router_b, moe_w1, moe_b1, moe_w2, moe_b2,
           final_norm_w):
    bp = x_prompt.shape[0]
    hb = N_HEADS_B * HEAD_DIM_B
    pad_heads = lambda a: jnp.pad(a, ((0, 0), (0, LANE - N_HEADS_A)))[:, None, :]
    head_of_lane = jnp.arange(D_INNER) // HEAD_DIM_A
    p = dict(
        mix_norm_w=mix_norm_w, ffn_norm_w=ffn_norm_w, kv_norm_w=kv_norm_w, final_norm_w=final_norm_w,
        w_z=a_in_proj[:, :, :D_INNER].astype(BF16),
        w_xbc=a_in_proj[:, :, D_INNER:D_INNER + CONV_DIM].astype(BF16),
        w_dt=jnp.pad(a_in_proj[:, :, D_INNER + CONV_DIM:], ((0, 0), (0, 0), (0, LANE - N_HEADS_A))).astype(BF16),
        a_conv_w=a_conv_w, a_conv_b=a_conv_b, dt_bias=pad_heads(a_dt_bias), a_log=pad_heads(a_A_log),
        d_x=a_D[:, head_of_lane][:, None, :], a_norm_w=a_norm_w, a_out_proj=a_out_proj.astype(BF16),
        e_mat=jnp.tile((jnp.arange(LANE)[:, None] == head_of_lane[None, :]).astype(BF16), (3, 1)),
        tri=jnp.tile((jnp.arange(SSD_Q)[:, None] >= jnp.arange(SSD_Q)[None, :]).astype(BF16), (1, 3)),
        shift=jnp.stack([(jnp.arange(SSD_Q)[:, None] + tap - (CONV_W - 1) == jnp.arange(SSD_Q)[None, :])
                         for tap in range(CONV_W - 1)]).astype(BF16),
        u_tri=(jnp.arange(ATTN_TK)[:, None] >= jnp.arange(ATTN_TK)[None, :]).astype(BF16),
        w_k=w_kv[:, :hb].astype(BF16), w_v=w_kv[:, hb:].astype(BF16),
        b_w_q=b_w_q.astype(BF16), b_w_o=b_w_o.astype(BF16),
        router_wt=router_w.transpose(0, 2, 1), router_b=router_b[:, :, None],
        su_tri=(jnp.arange(TOKEN_TILE)[:, None] < jnp.arange(TOKEN_TILE)[None, :]).astype(BF16),
        moe_w1=moe_w1, moe_b1=moe_b1[:, :, None, :], moe_w2=moe_w2, moe_b2=moe_b2[:, :, None, :],
    )
    c_all = jnp.concatenate([c_prompt, c_sample], axis=0)
    mods = mod_vectors(c_all, mod_w, mod_b[:, None, :])
    kv_mod = mod_vectors(c_all, kv_mod_w[None], kv_mod_b[None, None, :])[0]

    bg = bp // PROMPT_GROUPS
    conv0 = jnp.zeros((N_A_LAYERS, bg, CONV_W - 1, CONV_DIM), F32)
    ssm0 = jnp.zeros((N_A_LAYERS, bg, N_HEADS_A, HEAD_DIM_A, D_STATE), F32)
    kv0 = jnp.zeros((bg, 0, N_HEADS_B, HEAD_DIM_B), F32)
    groups = [_trunk(x_prompt[g * bg:(g + 1) * bg], mods[:, g * bg:(g + 1) * bg], kv_mod[g * bg:(g + 1) * bg],
                     conv0, ssm0, kv0, kv0, p) for g in range(PROMPT_GROUPS)]
    out_p = tuple(jnp.concatenate([grp[i] for grp in groups], axis=ax) for i, ax in enumerate((0, 1, 1, 0, 0)))
    out_s = _trunk(x_sample, mods[:, bp:], kv_mod[bp:], state_conv, state_ssm, cache_k, cache_v, p)
    return (out_p[0], out_s[0]) + out_p[1:] + out_s[1:]
```

```python
import functools
import math

import jax
import jax.numpy as jnp
from jax import lax
from jax.experimental import pallas as pl
from jax.experimental.pallas import tpu as pltpu

F32 = jnp.float32
BF16 = jnp.bfloat16
HIGHEST = lax.Precision.HIGHEST

D_MODEL = 1024
DEPTH = 4
N_A_LAYERS = 2
D_INNER = 2048
HEAD_DIM_A = 64
N_HEADS_A = 32
D_STATE = 128
N_GROUPS = 4
GROUP_W = D_INNER // N_GROUPS
CONV_W = 4
CONV_DIM = D_INNER + 2 * N_GROUPS * D_STATE
HEAD_DIM_B = 64
N_HEADS_B = 16
N_EXPERTS = 32
TOP_K = 4
D_FF = 1024
SWIGLU_ALPHA = 1.702
SWIGLU_LIMIT = 7.0
NORM_EPS = 1e-6
LOG2_E = 1.4426950408889634
HIDDEN = -1e30

LANE = 128
SSD_Q = 64
TOKEN_TILE = 512
MOE_TILE = 512
ATTN_TK = 256
ATTN_UNROLL = 4
PROMPT_GROUPS = 2
VMEM_LIMIT = 56 * 1024 * 1024


def _cparams(sem):
    return pltpu.CompilerParams(dimension_semantics=sem, vmem_limit_bytes=VMEM_LIMIT)


def _softplus(x):
    return jnp.maximum(x, 0.0) + jnp.log1p(jnp.exp(-jnp.abs(x)))


def _silu(x):
    return x * jax.nn.sigmoid(x)


def _norm_mod(x, nw, sh, sc):
    ms = jnp.mean(x * x, axis=-1, keepdims=True)
    return (x * lax.rsqrt(ms + NORM_EPS) * nw) * (1.0 + sc) + sh


def _mod_spec(mod3, tile, rows_per_group):
    steps = max(rows_per_group // tile, 1)
    g = mod3.shape[1]
    return pl.BlockSpec((1, g, mod3.shape[2]), lambda i: (i // steps, 0, 0))


def _mod_kernel(c_ref, w_ref, b_ref, o_ref):
    cs = _silu(c_ref[...])
    o_ref[0] = jnp.dot(cs, w_ref[0], precision=HIGHEST, preferred_element_type=F32) + b_ref[0]


def mod_vectors(c, w, b):
    nl, d, n = w.shape
    tn = 1024
    return pl.pallas_call(
        _mod_kernel,
        grid=(nl, n // tn),
        in_specs=[pl.BlockSpec(c.shape, lambda l, j: (0, 0)),
                  pl.BlockSpec((1, d, tn), lambda l, j: (l, 0, j)),
                  pl.BlockSpec((1, 1, tn), lambda l, j: (l, 0, j))],
        out_specs=pl.BlockSpec((1, c.shape[0], tn), lambda l, j: (l, 0, j)),
        out_shape=jax.ShapeDtypeStruct((nl, c.shape[0], n), F32),
        compiler_params=_cparams(("arbitrary", "arbitrary")),
        name="mod_vectors",
    )(c, w, b)


def _nmm_kernel(x_ref, nw_ref, sh_ref, sc_ref, *refs, n_w):
    h = _norm_mod(x_ref[...], nw_ref[...], sh_ref[0], sc_ref[0]).astype(BF16)
    for w_ref, o_ref in zip(refs[:n_w], refs[n_w:]):
        o_ref[...] = jnp.dot(h, w_ref[...], preferred_element_type=F32).astype(o_ref.dtype)


def norm_mod_matmul(x, nw, sh3, sc3, ws, out_dtypes, rows_per_group):
    t, d = x.shape
    tm = min(TOKEN_TILE, t)
    in_specs = [pl.BlockSpec((tm, d), lambda i: (i, 0)),
                pl.BlockSpec((1, d), lambda i: (0, 0)),
                _mod_spec(sh3, tm, rows_per_group), _mod_spec(sc3, tm, rows_per_group)]
    in_specs += [pl.BlockSpec(w.shape, lambda i: (0, 0), pipeline_mode=pl.Buffered(1)) for w in ws]
    return pl.pallas_call(
        functools.partial(_nmm_kernel, n_w=len(ws)),
        grid=(t // tm,),
        in_specs=in_specs,
        out_specs=[pl.BlockSpec((tm, w.shape[1]), lambda i: (i, 0)) for w in ws],
        out_shape=[jax.ShapeDtypeStruct((t, w.shape[1]), dt) for w, dt in zip(ws, out_dtypes)],
        compiler_params=_cparams(("parallel",)),
        name="norm_mod_matmul",
    )(x, nw, sh3, sc3, *ws)


def _mmres_kernel(a_ref, w_ref, x_ref, g_ref, o_ref):
    acc = jnp.dot(a_ref[...], w_ref[...], preferred_element_type=F32)
    o_ref[...] = x_ref[...] + g_ref[0] * acc


def matmul_residual(a, w, x, g3, rows_per_group):
    t, d = x.shape
    tm = min(TOKEN_TILE, t)
    return pl.pallas_call(
        _mmres_kernel,
        grid=(t // tm,),
        in_specs=[pl.BlockSpec((tm, a.shape[1]), lambda i: (i, 0)),
                  pl.BlockSpec(w.shape, lambda i: (0, 0), pipeline_mode=pl.Buffered(1)),
                  pl.BlockSpec((tm, d), lambda i: (i, 0)),
                  _mod_spec(g3, tm, rows_per_group)],
        out_specs=pl.BlockSpec((tm, d), lambda i: (i, 0)),
        out_shape=jax.ShapeDtypeStruct((t, d), F32),
        compiler_params=_cparams(("parallel",)),
        name="matmul_residual",
    )(a, w, x, g3)


def _rms_kernel(x_ref, nw_ref, o_ref):
    x = x_ref[...]
    ms = jnp.mean(x * x, axis=-1, keepdims=True)
    o_ref[...] = x * lax.rsqrt(ms + NORM_EPS) * nw_ref[...]


def final_rmsnorm(x, nw):
    t, d = x.shape
    tm = min(TOKEN_TILE, t)
    return pl.pallas_call(
        _rms_kernel,
        grid=(t // tm,),
        in_specs=[pl.BlockSpec((tm, d), lambda i: (i, 0)), pl.BlockSpec((1, d), lambda i: (0, 0))],
        out_specs=pl.BlockSpec((tm, d), lambda i: (i, 0)),
        out_shape=jax.ShapeDtypeStruct((t, d), F32),
        compiler_params=_cparams(("parallel",)),
        name="final_rmsnorm",
    )(x, nw)


def _router_kernel(x_ref, nw_ref, sh_ref, sc_ref, rwt_ref, rb_ref, su_ref, sl_ref, hs_ref, g_ref, p_ref, cnt_ref,
                   run_ref, tot_ref, run_scr):
    @pl.when(pl.program_id(0) == 0)
    def _():
        run_scr[...] = jnp.zeros_like(run_scr)

    h = _norm_mod(x_ref[...], nw_ref[...], sh_ref[0], sc_ref[0])
    logits = lax.dot_general(rwt_ref[...], h, (((1,), (1,)), ((), ())),
                             precision=HIGHEST, preferred_element_type=F32) + rb_ref[...]
    eid = lax.broadcasted_iota(jnp.int32, logits.shape, 0)
    vals, idxs, hits = [], [], []
    for _ in range(TOP_K):
        m = jnp.max(logits, axis=0, keepdims=True)
        idx = jnp.min(jnp.where(logits == m, eid, N_EXPERTS), axis=0, keepdims=True)
        hit = eid == idx
        vals.append(m)
        idxs.append(idx)
        hits.append(hit)
        logits = jnp.where(hit, -jnp.inf, logits)
    ex = [jnp.exp(v - vals[0]) for v in vals]
    den = ex[0] + ex[1] + ex[2] + ex[3]
    cnt = (hits[0] | hits[1] | hits[2] | hits[3]).astype(F32)
    cnt_b = cnt.astype(BF16)
    earlier = jnp.dot(cnt_b, su_ref[...], preferred_element_type=F32)
    lower = jnp.dot(sl_ref[...], cnt_b, preferred_element_type=F32)
    cnt_tile = jnp.sum(cnt, axis=1, keepdims=True)
    where_to = earlier + jnp.sum(lower, axis=1, keepdims=True)
    tm = x_ref.shape[0]
    slot = lax.broadcasted_iota(jnp.int32, (TOP_K * tm, tm), 0)
    place = None
    for k in range(TOP_K):
        pos = jnp.sum(jnp.where(hits[k], where_to, 0.0), axis=0, keepdims=True).astype(jnp.int32)
        g_ref[k:k + 1, :] = ex[k] / den
        p_ref[k:k + 1, :] = pos
        place = (slot == pos) if place is None else place | (slot == pos)
    hs_ref[...] = jnp.dot(place.astype(BF16), h.astype(BF16), preferred_element_type=F32).astype(BF16)
    cnt_ref[0] = cnt_tile
    run_ref[0] = run_scr[...]
    run_scr[...] += cnt_tile
    tot_ref[...] = run_scr[...]


def moe_router(x, nw, sh3, sc3, rwt, rb, su, sl, rows_per_group):
    t, d = x.shape
    tm = min(TOKEN_TILE, t)
    n_tiles = t // tm
    return pl.pallas_call(
        _router_kernel,
        grid=(n_tiles,),
        in_specs=[pl.BlockSpec((tm, d), lambda i: (i, 0)),
                  pl.BlockSpec((1, d), lambda i: (0, 0)),
                  _mod_spec(sh3, tm, rows_per_group), _mod_spec(sc3, tm, rows_per_group),
                  pl.BlockSpec(rwt.shape, lambda i: (0, 0)),
                  pl.BlockSpec(rb.shape, lambda i: (0, 0)),
                  pl.BlockSpec(su.shape, lambda i: (0, 0)),
                  pl.BlockSpec(sl.shape, lambda i: (0, 0))],
        out_specs=[pl.BlockSpec((TOP_K * tm, d), lambda i: (i, 0)),
                   pl.BlockSpec((TOP_K, tm), lambda i: (0, i)),
                   pl.BlockSpec((TOP_K, tm), lambda i: (0, i)),
                   pl.BlockSpec((1, N_EXPERTS, 1), lambda i: (i, 0, 0)),
                   pl.BlockSpec((1, N_EXPERTS, 1), lambda i: (i, 0, 0)),
                   pl.BlockSpec((N_EXPERTS, 1), lambda i: (0, 0))],
        out_shape=[jax.ShapeDtypeStruct((TOP_K * t, d), BF16),
                   jax.ShapeDtypeStruct((TOP_K, t), F32),
                   jax.ShapeDtypeStruct((TOP_K, t), jnp.int32),
                   jax.ShapeDtypeStruct((n_tiles, N_EXPERTS, 1), F32),
                   jax.ShapeDtypeStruct((n_tiles, N_EXPERTS, 1), F32),
                   jax.ShapeDtypeStruct((N_EXPERTS, 1), F32)],
        scratch_shapes=[pltpu.VMEM((N_EXPERTS, 1), F32)],
        compiler_params=_cparams(("arbitrary",)),
        name="moe_router",
    )(x, nw, sh3, sc3, rwt, rb, su, sl)


def _ffn_kernel(be_ref, nu_ref, xs_ref, w1_ref, b1_ref, w2_ref, b2_ref, o_ref, w1_scr, w2_scr):
    i = pl.program_id(0)

    @pl.when(i < nu_ref[0])
    def _():
        @pl.when((i == 0) | (be_ref[i] != be_ref[jnp.maximum(i - 1, 0)]))
        def _():
            w1_scr[...] = w1_ref[0, 0].astype(BF16)
            w2_scr[...] = w2_ref[0, 0].astype(BF16)

        hid = jnp.dot(xs_ref[...], w1_scr[...], preferred_element_type=F32) + b1_ref[0, 0]
        glu = jnp.minimum(hid[:, :D_FF], SWIGLU_LIMIT)
        lin = jnp.clip(hid[:, D_FF:], -SWIGLU_LIMIT, SWIGLU_LIMIT)
        act = glu * jax.nn.sigmoid(SWIGLU_ALPHA * glu) * (lin + 1.0)
        out = jnp.dot(act.astype(BF16), w2_scr[...], preferred_element_type=F32) + b2_ref[0, 0]
        o_ref[...] = out.astype(o_ref.dtype)


def moe_ffn_blocks(xs, block_e, n_used, w1, b1, w2, b2, layer):
    n_rows, d = xs.shape
    tm = MOE_TILE
    grid_spec = pltpu.PrefetchScalarGridSpec(
        num_scalar_prefetch=2,
        grid=(n_rows // tm,),
        in_specs=[pl.BlockSpec((tm, d), lambda i, be, nu: (i, 0)),
                  pl.BlockSpec((1, 1, d, 2 * D_FF), lambda i, be, nu: (layer, be[i], 0, 0)),
                  pl.BlockSpec((1, 1, 1, 2 * D_FF), lambda i, be, nu: (layer, be[i], 0, 0)),
                  pl.BlockSpec((1, 1, D_FF, d), lambda i, be, nu: (layer, be[i], 0, 0)),
                  pl.BlockSpec((1, 1, 1, d), lambda i, be, nu: (layer, be[i], 0, 0))],
        out_specs=pl.BlockSpec((tm, d), lambda i, be, nu: (i, 0)),
        scratch_shapes=[pltpu.VMEM((d, 2 * D_FF), BF16), pltpu.VMEM((D_FF, d), BF16)],
    )
    return pl.pallas_call(
        _ffn_kernel,
        grid_spec=grid_spec,
        out_shape=jax.ShapeDtypeStruct((n_rows, d), BF16),
        compiler_params=_cparams(("arbitrary",)),
        name="moe_ffn",
    )(block_e, n_used, xs, w1, b1, w2, b2)


def _combine_kernel(y_ref, gt_ref, p_ref, x_ref, g_ref, o_ref):
    tm = x_ref.shape[0]
    eye = lax.broadcasted_iota(jnp.int32, (tm, tm), 0) == lax.broadcasted_iota(jnp.int32, (tm, tm), 1)
    col = lax.broadcasted_iota(jnp.int32, (tm, TOP_K * tm), 1)
    pick = jnp.zeros((tm, TOP_K * tm), F32)
    for k in range(TOP_K):
        g_col = jnp.sum(jnp.where(eye, gt_ref[k:k + 1, :], 0.0), axis=1, keepdims=True)
        p_col = jnp.sum(jnp.where(eye, p_ref[k:k + 1, :], 0), axis=1, keepdims=True)
        pick = jnp.where(col == p_col, g_col, pick)
    ff = jnp.dot(pick.astype(BF16), y_ref[...], preferred_element_type=F32)
    o_ref[...] = x_ref[...] + g_ref[0] * ff


def moe_combine(ysg, gates, pos, x, g3, rows_per_group):
    t, d = x.shape
    tm = min(TOKEN_TILE, t)
    return pl.pallas_call(
        _combine_kernel,
        grid=(t // tm,),
        in_specs=[pl.BlockSpec((TOP_K * tm, d), lambda i: (i, 0)),
                  pl.BlockSpec((TOP_K, tm), lambda i: (0, i)),
                  pl.BlockSpec((TOP_K, tm), lambda i: (0, i)),
                  pl.BlockSpec((tm, d), lambda i: (i, 0)),
                  _mod_spec(g3, tm, rows_per_group)],
        out_specs=pl.BlockSpec((tm, d), lambda i: (i, 0)),
        out_shape=jax.ShapeDtypeStruct((t, d), F32),
        compiler_params=_cparams(("parallel",)),
        name="moe_combine",
    )(ysg, gates, pos, x, g3)


def _take_rows(a, idx):
    return a.at[idx].get(mode="promise_in_bounds")


def moe_layer(x, nw, sh3, sc3, g3, rwt, rb, su, sl, w1, b1, w2, b2, layer, rows_per_group):
    t, d = x.shape
    tt = min(TOKEN_TILE, t)
    n_tiles, tl = t // tt, TOP_K * tt
    hs, gates, pos, cnt, run, tot = moe_router(x, nw, sh3, sc3, rwt, rb, su, sl, rows_per_group)
    tm = MOE_TILE
    n_blocks = (t * TOP_K + N_EXPERTS * (tm - 1) + tm - 1) // tm
    i32 = jnp.int32
    counts = tot[:, 0].astype(i32)
    padded = (counts + tm - 1) // tm * tm
    pend = jnp.cumsum(padded)
    pstart = (pend - padded).astype(i32)
    block_e = jnp.minimum(jnp.searchsorted(pend, jnp.arange(n_blocks) * tm, side='right'),
                          N_EXPERTS - 1).astype(i32)
    n_used = (pend[-1] // tm).astype(i32).reshape(1)
    cnt = cnt[:, :, 0].astype(i32)
    run = run[:, :, 0].astype(i32)
    eoff = jnp.cumsum(cnt, axis=1) - cnt
    tiles = jnp.arange(n_tiles, dtype=i32)
    in_e = (jnp.arange(n_blocks, dtype=i32) * tm - pstart[block_e])[:, None] + jnp.arange(tm, dtype=i32)[None, :]
    run_e = run.T[block_e]
    tile = jnp.sum((run_e[:, None, :] <= in_e[:, :, None]).astype(i32), axis=-1) - 1
    shift = jnp.sum(jnp.where(tiles[None, None, :] == tile[:, :, None], (eoff.T[block_e] - run_e)[:, None, :], 0),
                    axis=-1)
    src = jnp.where(in_e < counts[block_e][:, None], tile * tl + shift + in_e, 0)
    xs = _take_rows(hs, src.reshape(-1))
    ys = moe_ffn_blocks(xs, block_e, n_used, w1, b1, w2, b2, layer)
    j = jnp.arange(tl, dtype=i32)
    e_j = jnp.sum(((eoff + cnt)[:, None, :] <= j[None, :, None]).astype(i32), axis=-1)
    base = pstart[None, :] + run - eoff
    back = jnp.sum(jnp.where(jnp.arange(N_EXPERTS, dtype=i32)[None, None, :] == e_j[:, :, None], base[:, None, :], 0),
                   axis=-1) + j[None, :]
    ysg = _take_rows(ys, back.reshape(-1))
    return moe_combine(ysg, gates, pos, x, g3, rows_per_group)


def _split3(x):
    hi = x.astype(BF16)
    r1 = x - hi.astype(F32)
    mid = r1.astype(BF16)
    lo = (r1 - mid.astype(F32)).astype(BF16)
    return hi, mid, lo


def _ssd_kernel(z_ref, xbc_ref, dt_ref, cw_ref, cb_ref, dtb_ref, alog_ref, dx_ref, nw_ref, e_ref, tri_ref, shift_ref,
                ic_ref, is_ref, g_ref, so_ref, tail_scr, s_scr, y_scr, *, valid_rows):
    q = SSD_Q
    c = pl.program_id(1)

    @pl.when(c == 0)
    def _():
        tail_scr[...] = ic_ref[0]
        s_scr[...] = is_ref[0]

    xb = xbc_ref[...]
    acc = cb_ref[...] + xb.astype(F32) * cw_ref[CONV_W - 1:CONV_W, :]
    for tap in range(CONV_W - 1):
        shifted = jnp.dot(shift_ref[tap], xb, preferred_element_type=F32)
        acc = acc + shifted * cw_ref[tap:tap + 1, :]
    t0, t1, t2 = (tail_scr[8 - (CONV_W - 1) + j:8 - (CONV_W - 1) + j + 1, :] for j in range(CONV_W - 1))
    w0, w1, w2 = (cw_ref[j:j + 1, :] for j in range(CONV_W - 1))
    row8 = lax.broadcasted_iota(jnp.int32, (8, CONV_DIM), 0)
    head = jnp.where(row8 == 0, t0 * w0 + t1 * w1 + t2 * w2,
                     jnp.where(row8 == 1, t1 * w0 + t2 * w1, jnp.where(row8 == 2, t2 * w0, 0.0)))
    tail_scr[...] = xbc_ref[q - 16:q, :].astype(F32)[8:16, :]
    xc = _silu(jnp.concatenate([acc[0:8, :] + head, acc[8:q, :]], axis=0))
    xs = xc[:, :D_INNER]

    dt = _softplus(dt_ref[...] + dtb_ref[...])
    if valid_rows < q:
        rows = lax.broadcasted_iota(jnp.int32, dt.shape, 0)
        dt = jnp.where(rows < valid_rows, dt, 0.0)
    a = dt * (-jnp.exp(alog_ref[...]))
    a_cs = jnp.dot(tri_ref[...], jnp.concatenate(_split3(a), axis=0), preferred_element_type=F32)
    dtx = jnp.dot(jnp.concatenate(_split3(dt), axis=1), e_ref[...], preferred_element_type=F32)
    acsx = jnp.dot(jnp.concatenate(_split3(a_cs), axis=1), e_ref[...], preferred_element_type=F32)

    row = lax.broadcasted_iota(jnp.int32, (q, D_INNER), 0)
    col = lax.broadcasted_iota(jnp.int32, (q, D_INNER), 1) & (HEAD_DIM_A - 1)
    acs_row = jnp.sum(jnp.where(row == col, acsx, 0.0), axis=0, keepdims=True)
    decay_in = jnp.where(row >= col, jnp.exp(jnp.minimum(acsx - acs_row, 0.0)), 0.0)
    last = acsx[q - 1:q, :]
    e_in = jnp.exp(acsx)
    e_last = jnp.exp(last)
    xd = xs * dtx
    xdd = xd * jnp.exp(last - acsx)
    lane = lax.broadcasted_iota(jnp.int32, (q, LANE), 1)
    lo = lane < HEAD_DIM_A

    for g in range(N_GROUPS):
        gsl = slice(g * GROUP_W, (g + 1) * GROUP_W)
        b_off = D_INNER + g * D_STATE
        c_off = D_INNER + N_GROUPS * D_STATE + g * D_STATE
        bg = xc[:, b_off:b_off + D_STATE]
        cgb = xc[:, c_off:c_off + D_STATE].astype(BF16)
        bgb = bg.astype(BF16)
        b2 = jnp.concatenate([bgb, bgb], axis=0)
        cb2 = lax.dot_general(cgb, b2, (((1,), (1,)), ((), ())), preferred_element_type=F32)
        s_g = s_scr[:, gsl]
        y_off = jnp.dot(cgb, s_g.astype(BF16), preferred_element_type=F32) * e_in[:, gsl]
        for pp in range(GROUP_W // LANE):
            sl = slice(g * GROUP_W + pp * LANE, g * GROUP_W + (pp + 1) * LANE)
            m = (decay_in[:, sl] * cb2).astype(BF16)
            xp = xd[:, sl]
            bd = jnp.concatenate([jnp.where(lo, xp, 0.0), jnp.where(lo, 0.0, xp)], axis=0).astype(BF16)
            y_diag = jnp.dot(m, bd, preferred_element_type=F32)
            y_scr[:, sl] = y_diag + y_off[:, pp * LANE:(pp + 1) * LANE] + xs[:, sl] * dx_ref[:, sl]
        s_new = jnp.dot(bg.T.astype(BF16), xdd[:, gsl].astype(BF16), preferred_element_type=F32)
        s_scr[:, gsl] = s_g * e_last[:, gsl] + s_new

    for g in range(N_GROUPS):
        gsl = slice(g * GROUP_W, (g + 1) * GROUP_W)
        gz = y_scr[:, gsl] * _silu(z_ref[:, gsl].astype(F32))
        ms = jnp.mean(gz * gz, axis=-1, keepdims=True)
        g_ref[:, gsl] = (gz * lax.rsqrt(ms + NORM_EPS) * nw_ref[:, gsl]).astype(g_ref.dtype)

    @pl.when(c == pl.num_programs(1) - 1)
    def _():
        so_ref[0] = s_scr[...]


def ssd_mixer(z, xbc, dt, cw, cb, dtb, alog, dx, nw, e_mat, tri, shift, init_conv, init_ssm, nb, nc, valid_rows):
    q = SSD_Q
    const = lambda a: pl.BlockSpec(a.shape, lambda b, c: (0,) * a.ndim)
    return pl.pallas_call(
        functools.partial(_ssd_kernel, valid_rows=valid_rows),
        grid=(nb, nc),
        in_specs=[pl.BlockSpec((q, D_INNER), lambda b, c: (b * nc + c, 0)),
                  pl.BlockSpec((q, CONV_DIM), lambda b, c: (b * nc + c, 0)),
                  pl.BlockSpec((q, LANE), lambda b, c: (b * nc + c, 0)),
                  const(cw), const(cb), const(dtb), const(alog), const(dx), const(nw), const(e_mat), const(tri),
                  const(shift),
                  pl.BlockSpec((1, 8, CONV_DIM), lambda b, c: (b, 0, 0)),
                  pl.BlockSpec((1, D_STATE, D_INNER), lambda b, c: (b, 0, 0))],
        out_specs=[pl.BlockSpec((q, D_INNER), lambda b, c: (b * nc + c, 0)),
                   pl.BlockSpec((1, D_STATE, D_INNER), lambda b, c: (b, 0, 0))],
        out_shape=[jax.ShapeDtypeStruct((nb * nc * q, D_INNER), BF16),
                   jax.ShapeDtypeStruct((nb, D_STATE, D_INNER), F32)],
        scratch_shapes=[pltpu.VMEM((8, CONV_DIM), F32),
                        pltpu.VMEM((D_STATE, D_INNER), F32),
                        pltpu.VMEM((q, D_INNER), F32)],
        compiler_params=_cparams(("parallel", "arbitrary")),
        name="ssd_mixer",
    )(z, xbc, dt, cw, cb, dtb, alog, dx, nw, e_mat, tri, shift, init_conv, init_ssm)


def _attn_kernel(q_ref, k_ref, v_ref, u_ref, o_ref, qs_scr, acc_scr, c_scr, y_scr, t_scr, tot_scr,
                 *, tq, tk, pos0, n_kblocks):
    qi = pl.program_id(2)
    qv = q_ref[0]
    lane_q = lax.broadcasted_iota(jnp.int32, qv.shape, 1)
    zero_q = jnp.zeros_like(qv)
    qs_scr[0:tq, :] = jnp.where(lane_q < HEAD_DIM_B, qv, zero_q)
    qs_scr[tq:2 * tq, :] = jnp.where(lane_q < HEAD_DIM_B, zero_q, qv)
    acc_scr[...] = jnp.zeros_like(acc_scr)
    c_scr[...] = jnp.zeros_like(c_scr)
    y_scale = LOG2_E / math.sqrt(HEAD_DIM_B)
    q_idx = lax.broadcasted_iota(jnp.int32, (2 * tq, tk), 0) & (tq - 1)
    pos_gap = pos0 + qi * tq + q_idx - lax.broadcasted_iota(jnp.int32, (2 * tq, tk), 1)
    n_vis = jnp.minimum((pos0 + (qi + 1) * tq - 2) // tk + 1, n_kblocks)
    n_full = jnp.minimum((pos0 + qi * tq) // tk, n_kblocks)
    n_masked = n_vis - n_full

    def kblock(i):
        return jnp.maximum(n_vis - 1 - i, 0)

    def scores(i, dst):
        start = pl.multiple_of(kblock(i) * tk, tk)
        kb = k_ref[0, pl.ds(start, tk), :]
        y = lax.dot_general(qs_scr[...], kb, (((1,), (1,)), ((), ())), preferred_element_type=F32)
        y_scr[dst] = y * y_scale

    def suffix(i, masked, src, dst):
        y = y_scr[src]
        sp = jnp.maximum(y, jnp.log(1.0 + jnp.exp2(jnp.minimum(y, 126.0))) * LOG2_E)
        if masked:
            vis = pos_gap > kblock(i) * tk
            sp = jnp.where(vis, sp, 0.0)
        cs = jnp.dot(sp.astype(BF16), u_ref[...], preferred_element_type=F32)
        t = y - cs
        if masked:
            t = jnp.where(vis, t, HIDDEN)
        t_scr[dst] = t
        tot_scr[dst] = cs[:, 0:1]

    def weigh(i, src):
        start = pl.multiple_of(kblock(i) * tk, tk)
        vb = v_ref[0, pl.ds(start, tk), :]
        c = c_scr[...]
        w = jnp.exp2(t_scr[src] - c)
        acc_scr[...] += jnp.dot(w.astype(BF16), vb, preferred_element_type=F32)
        c_scr[...] = c + tot_scr[src]

    scores(0, 0)
    suffix(0, True, 0, 0)
    scores(1, 0)

    def trip(i, masked):
        weigh(i, 0)
        suffix(i + 1, masked, 0, 0)
        scores(i + 2, 0)

    def masked_trip(i, carry):
        trip(i, True)
        return carry

    def full_trip(i, carry):
        trip(i, False)
        return carry

    def group_body(first, size):
        def body(p, carry):
            i = first + size * p
            for j in range(size):
                cur, nxt = j & 1, (j + 1) & 1
                scores(i + j + 2, nxt)
                suffix(i + j + 1, False, cur, nxt)
                weigh(i + j, cur)
            return carry
        return body

    n_mt = jnp.maximum(n_masked - 1, 0)
    first_pair = n_mt + ((n_vis - n_mt) & 1)
    first_group = first_pair + ((n_vis - first_pair) % ATTN_UNROLL)
    lax.fori_loop(0, n_mt, masked_trip, 0)
    lax.fori_loop(n_mt, first_pair, full_trip, 0)
    lax.fori_loop(0, (first_group - first_pair) // 2, group_body(first_pair, 2), 0)
    lax.fori_loop(0, (n_vis - first_group) // ATTN_UNROLL, group_body(first_group, ATTN_UNROLL), 0)
    lane_o = lax.broadcasted_iota(jnp.int32, (tq, LANE), 1)
    o_ref[0] = jnp.where(lane_o < HEAD_DIM_B, acc_scr[0:tq, :], acc_scr[tq:2 * tq, :]).astype(o_ref.dtype)


def stick_breaking(q, k, v, u, pos0, tq):
    b, l, hd = q.shape
    kp = k.shape[1]
    tk = ATTN_TK
    assert tq & (tq - 1) == 0 and l % tq == 0 and kp % tk == 0
    return pl.pallas_call(
        functools.partial(_attn_kernel, tq=tq, tk=tk, pos0=pos0, n_kblocks=kp // tk),
        grid=(b, hd // LANE, l // tq),
        in_specs=[pl.BlockSpec((1, tq, LANE), lambda bi, p, i: (bi, i, p)),
                  pl.BlockSpec((1, kp, LANE), lambda bi, p, i: (bi, 0, p)),
                  pl.BlockSpec((1, kp, LANE), lambda bi, p, i: (bi, 0, p)),
                  pl.BlockSpec(u.shape, lambda bi, p, i: (0, 0))],
        out_specs=pl.BlockSpec((1, tq, LANE), lambda bi, p, i: (bi, i, p)),
        out_shape=jax.ShapeDtypeStruct((b, l, hd), BF16),
        scratch_shapes=[pltpu.VMEM((2 * tq, LANE), BF16), pltpu.VMEM((2 * tq, LANE), F32),
                        pltpu.VMEM((2 * tq, 1), F32), pltpu.VMEM((2, 2 * tq, tk), F32),
                        pltpu.VMEM((2, 2 * tq, tk), F32), pltpu.VMEM((2, 2 * tq, 1), F32)],
        compiler_params=_cparams(("parallel", "parallel", "arbitrary")),
        name="stick_breaking",
    )(q, k, v, u)


def _trunk(x3, mods, kv_mod, conv_state, ssm_state, past_k, past_v, p):
    b, l, d = x3.shape
    t = b * l
    x = x3.reshape(t, d)
    per_token = l < TOKEN_TILE

    def mod3(v):
        if per_token:
            return jnp.repeat(v, l, axis=0).reshape(t // min(TOKEN_TILE, t), min(TOKEN_TILE, t), d)
        return v[:, None, :]

    rows_per_group = min(TOKEN_TILE, t) if per_token else l
    pos0 = past_k.shape[1]
    conv_out, ssm_out = [], []
    k_new = v_new = kb = vb = None
    for layer in range(DEPTH):
        sh1, sc1, g1, sh2, sc2, g2 = jnp.split(mods[layer], 6, axis=-1)
        nw1 = p['mix_norm_w'][layer][None, :]
        if layer < N_A_LAYERS:
            i = layer
            z, xbc, dt = norm_mod_matmul(x, nw1, mod3(sh1), mod3(sc1), [p['w_z'][i], p['w_xbc'][i], p['w_dt'][i]],
                                         [BF16, BF16, F32], rows_per_group)
            xbc3 = xbc.reshape(b, l, CONV_DIM)
            conv_out.append(xbc3[:, l - (CONV_W - 1):].astype(F32))
            nc = max(l // SSD_Q, 1)
            if l < SSD_Q:
                pad = lambda a: jnp.pad(a.reshape(b, l, -1), ((0, 0), (0, SSD_Q - l), (0, 0))).reshape(b * SSD_Q, -1)
                z, xbc, dt = pad(z), pad(xbc), pad(dt)
            ic = jnp.pad(conv_state[i].astype(F32), ((0, 0), (8 - (CONV_W - 1), 0), (0, 0)))
            iss = ssm_state[i].astype(F32).transpose(0, 3, 1, 2).reshape(b, D_STATE, D_INNER)
            gm, s_fin = ssd_mixer(z, xbc, dt, p['a_conv_w'][i], p['a_conv_b'][i][None, :], p['dt_bias'][i],
                                  p['a_log'][i], p['d_x'][i], p['a_norm_w'][i][None, :], p['e_mat'], p['tri'], p['shift'],
                                  ic, iss, b, nc, min(l, SSD_Q))
            if l < SSD_Q:
                gm = gm.reshape(b, SSD_Q, D_INNER)[:, :l].reshape(t, D_INNER)
            ssm_out.append(s_fin.reshape(b, D_STATE, N_HEADS_A, HEAD_DIM_A).transpose(0, 2, 3, 1))
            x = matmul_residual(gm, p['a_out_proj'][i], x, mod3(g1), rows_per_group)
        else:
            j = layer - N_A_LAYERS
            (qp,) = norm_mod_matmul(x, nw1, mod3(sh1), mod3(sc1), [p['b_w_q'][j]], [BF16], rows_per_group)
            o = stick_breaking(qp.reshape(b, l, d), kb, vb, p['u_tri'], pos0, min(l, 256))
            x = matmul_residual(o.reshape(t, d), p['b_w_o'][j], x, mod3(g1), rows_per_group)
        x = moe_layer(x, p['ffn_norm_w'][layer][None, :], mod3(sh2), mod3(sc2), mod3(g2),
                      p['router_wt'][layer], p['router_b'][layer], p['su_tri'], p['sl_tri'], p['moe_w1'], p['moe_b1'],
                      p['moe_w2'], p['moe_b2'], layer, rows_per_group)
        if layer == N_A_LAYERS - 1:
            sh_kv, sc_kv = jnp.split(kv_mod, 2, axis=-1)
            k_new, v_new = norm_mod_matmul(x, p['kv_norm_w'][None, :], mod3(sh_kv), mod3(sc_kv),
                                           [p['w_k'], p['w_v']], [F32, F32], rows_per_group)
            k_all = jnp.concatenate([past_k.reshape(b, pos0, d), k_new.reshape(b, l, d)], axis=1)
            v_all = jnp.concatenate([past_v.reshape(b, pos0, d), v_new.reshape(b, l, d)], axis=1)
            kpad = (-k_all.shape[1]) % ATTN_TK
            kb = jnp.pad(k_all, ((0, 0), (0, kpad), (0, 0))).astype(BF16)
            vb = jnp.pad(v_all, ((0, 0), (0, kpad), (0, 0))).astype(BF16)
    y = final_rmsnorm(x, p['final_norm_w'][None, :])
    return (y.reshape(b, l, d), jnp.stack(conv_out), jnp.stack(ssm_out),
            k_new.reshape(b, l, N_HEADS_B, HEAD_DIM_B), v_new.reshape(b, l, N_HEADS_B, HEAD_DIM_B))


def kernel(x_prompt, x_sample, state_conv, state_ssm, cache_k, cache_v, c_prompt, c_sample, mod_w, mod_b,
           mix_norm_w, ffn_norm_w, a_in_proj, a_conv_w, a_conv_b, a_dt_bias, a_A_log, a_D, a_norm_w, a_out_proj,
           kv_mod_w, kv_mod_b, kv_norm_w, w_kv, b_w_q, b_w_o, router_w, router_b, moe_w1, moe_b1, moe_w2, moe_b2,
           final_norm_w):
    bp = x_prompt.shape[0]
    hb = N_HEADS_B * HEAD_DIM_B
    pad_heads = lambda a: jnp.pad(a, ((0, 0), (0, LANE - N_HEADS_A)))[:, None, :]
    head_of_lane = jnp.arange(D_INNER) // HEAD_DIM_A
    p = dict(
        mix_norm_w=mix_norm_w, ffn_norm_w=ffn_norm_w, kv_norm_w=kv_norm_w, final_norm_w=final_norm_w,
        w_z=a_in_proj[:, :, :D_INNER].astype(BF16),
        w_xbc=a_in_proj[:, :, D_INNER:D_INNER + CONV_DIM].astype(BF16),
        w_dt=jnp.pad(a_in_proj[:, :, D_INNER + CONV_DIM:], ((0, 0), (0, 0), (0, LANE - N_HEADS_A))).astype(BF16),
        a_conv_w=a_conv_w, a_conv_b=a_conv_b, dt_bias=pad_heads(a_dt_bias), a_log=pad_heads(a_A_log),
        d_x=a_D[:, head_of_lane][:, None, :], a_norm_w=a_norm_w, a_out_proj=a_out_proj.astype(BF16),
        e_mat=jnp.tile((jnp.arange(LANE)[:, None] == head_of_lane[None, :]).astype(BF16), (3, 1)),
        tri=jnp.tile((jnp.arange(SSD_Q)[:, None] >= jnp.arange(SSD_Q)[None, :]).astype(BF16), (1, 3)),
        shift=jnp.stack([(jnp.arange(SSD_Q)[:, None] + tap - (CONV_W - 1) == jnp.arange(SSD_Q)[None, :])
                         for tap in range(CONV_W - 1)]).astype(BF16),
        u_tri=(jnp.arange(ATTN_TK)[:, None] >= jnp.arange(ATTN_TK)[None, :]).astype(BF16),
        w_k=w_kv[:, :hb].astype(BF16), w_v=w_kv[:, hb:].astype(BF16),
        b_w_q=b_w_q.astype(BF16), b_w_o=b_w_o.astype(BF16),
        router_wt=router_w.transpose(0, 2, 1), router_b=router_b[:, :, None],
        su_tri=(jnp.arange(TOKEN_TILE)[:, None] < jnp.arange(TOKEN_TILE)[None, :]).astype(BF16),
        sl_tri=(jnp.arange(N_EXPERTS)[:, None] > jnp.arange(N_EXPERTS)[None, :]).astype(BF16),
        moe_w1=moe_w1, moe_b1=moe_b1[:, :, None, :], moe_w2=moe_w2, moe_b2=moe_b2[:, :, None, :],
    )
    c_all = jnp.concatenate([c_prompt, c_sample], axis=0)
    mods = mod_vectors(c_all, mod_w, mod_b[:, None, :])
    kv_mod = mod_vectors(c_all, kv_mod_w[None], kv_mod_b[None, None, :])[0]

    bg = bp // PROMPT_GROUPS
    conv0 = jnp.zeros((N_A_LAYERS, bg, CONV_W - 1, CONV_DIM), F32)
    ssm0 = jnp.zeros((N_A_LAYERS, bg, N_HEADS_A, HEAD_DIM_A, D_STATE), F32)
    kv0 = jnp.zeros((bg, 0, N_HEADS_B, HEAD_DIM_B), F32)
    groups = [_trunk(x_prompt[g * bg:(g + 1) * bg], mods[:, g * bg:(g + 1) * bg], kv_mod[g * bg:(g + 1) * bg],
                     conv0, ssm0, kv0, kv0, p) for g in range(PROMPT_GROUPS)]
    out_p = tuple(jnp.concatenate([grp[i] for grp in groups], axis=ax) for i, ax in enumerate((0, 1, 1, 0, 0)))
    out_s = _trunk(x_sample, mods[:, bp:], kv_mod[bp:], state_conv, state_ssm, cache_k, cache_v, p)
    return (out_p[0], out_s[0]) + out_p[1:] + out_s[1:]
```

```python
import functools
import math

import jax
import jax.numpy as jnp
from jax import lax
from jax.experimental import pallas as pl
from jax.experimental.pallas import tpu as pltpu

F32 = jnp.float32
BF16 = jnp.bfloat16
HIGHEST = lax.Precision.HIGHEST

D_MODEL = 1024
DEPTH = 4
N_A_LAYERS = 2
D_INNER = 2048
HEAD_DIM_A = 64
N_HEADS_A = 32
D_STATE = 128
N_GROUPS = 4
GROUP_W = D_INNER // N_GROUPS
CONV_W = 4
CONV_DIM = D_INNER + 2 * N_GROUPS * D_STATE
HEAD_DIM_B = 64
N_HEADS_B = 16
N_EXPERTS = 32
TOP_K = 4
D_FF = 1024
SWIGLU_ALPHA = 1.702
SWIGLU_LIMIT = 7.0
NORM_EPS = 1e-6
LOG2_E = 1.4426950408889634
HIDDEN = -1e30
SKIP_BITS = 160.0

LANE = 128
SSD_Q = 64
TOKEN_TILE = 512
MOE_TILE = 512
ATTN_TK = 256
ATTN_UNROLL = 4
PROMPT_GROUPS = 2
VMEM_LIMIT = 56 * 1024 * 1024


def _cparams(sem):
    return pltpu.CompilerParams(dimension_semantics=sem, vmem_limit_bytes=VMEM_LIMIT)


def _softplus(x):
    return jnp.maximum(x, 0.0) + jnp.log1p(jnp.exp(-jnp.abs(x)))


def _silu(x):
    return x * jax.nn.sigmoid(x)


def _norm_mod(x, nw, sh, sc):
    ms = jnp.mean(x * x, axis=-1, keepdims=True)
    return (x * lax.rsqrt(ms + NORM_EPS) * nw) * (1.0 + sc) + sh


def _mod_spec(mod3, tile, rows_per_group):
    steps = max(rows_per_group // tile, 1)
    g = mod3.shape[1]
    return pl.BlockSpec((1, g, mod3.shape[2]), lambda i: (i // steps, 0, 0))


def _mod_kernel(c_ref, w_ref, b_ref, o_ref):
    cs = _silu(c_ref[...])
    o_ref[0] = jnp.dot(cs, w_ref[0], precision=HIGHEST, preferred_element_type=F32) + b_ref[0]


def mod_vectors(c, w, b):
    nl, d, n = w.shape
    tn = 1024
    return pl.pallas_call(
        _mod_kernel,
        grid=(nl, n // tn),
        in_specs=[pl.BlockSpec(c.shape, lambda l, j: (0, 0)),
                  pl.BlockSpec((1, d, tn), lambda l, j: (l, 0, j)),
                  pl.BlockSpec((1, 1, tn), lambda l, j: (l, 0, j))],
        out_specs=pl.BlockSpec((1, c.shape[0], tn), lambda l, j: (l, 0, j)),
        out_shape=jax.ShapeDtypeStruct((nl, c.shape[0], n), F32),
        compiler_params=_cparams(("arbitrary", "arbitrary")),
        name="mod_vectors",
    )(c, w, b)


def _nmm_kernel(x_ref, nw_ref, sh_ref, sc_ref, *refs, n_w):
    h = _norm_mod(x_ref[...], nw_ref[...], sh_ref[0], sc_ref[0]).astype(BF16)
    for w_ref, o_ref in zip(refs[:n_w], refs[n_w:]):
        o_ref[...] = jnp.dot(h, w_ref[...], preferred_element_type=F32).astype(o_ref.dtype)


def norm_mod_matmul(x, nw, sh3, sc3, ws, out_dtypes, rows_per_group):
    t, d = x.shape
    tm = min(TOKEN_TILE, t)
    in_specs = [pl.BlockSpec((tm, d), lambda i: (i, 0)),
                pl.BlockSpec((1, d), lambda i: (0, 0)),
                _mod_spec(sh3, tm, rows_per_group), _mod_spec(sc3, tm, rows_per_group)]
    in_specs += [pl.BlockSpec(w.shape, lambda i: (0, 0), pipeline_mode=pl.Buffered(1)) for w in ws]
    return pl.pallas_call(
        functools.partial(_nmm_kernel, n_w=len(ws)),
        grid=(t // tm,),
        in_specs=in_specs,
        out_specs=[pl.BlockSpec((tm, w.shape[1]), lambda i: (i, 0)) for w in ws],
        out_shape=[jax.ShapeDtypeStruct((t, w.shape[1]), dt) for w, dt in zip(ws, out_dtypes)],
        compiler_params=_cparams(("parallel",)),
        name="norm_mod_matmul",
    )(x, nw, sh3, sc3, *ws)


def _mmres_kernel(a_ref, w_ref, x_ref, g_ref, o_ref):
    acc = jnp.dot(a_ref[...], w_ref[...], preferred_element_type=F32)
    o_ref[...] = x_ref[...] + g_ref[0] * acc


def matmul_residual(a, w, x, g3, rows_per_group):
    t, d = x.shape
    tm = min(TOKEN_TILE, t)
    return pl.pallas_call(
        _mmres_kernel,
        grid=(t // tm,),
        in_specs=[pl.BlockSpec((tm, a.shape[1]), lambda i: (i, 0)),
                  pl.BlockSpec(w.shape, lambda i: (0, 0), pipeline_mode=pl.Buffered(1)),
                  pl.BlockSpec((tm, d), lambda i: (i, 0)),
                  _mod_spec(g3, tm, rows_per_group)],
        out_specs=pl.BlockSpec((tm, d), lambda i: (i, 0)),
        out_shape=jax.ShapeDtypeStruct((t, d), F32),
        compiler_params=_cparams(("parallel",)),
        name="matmul_residual",
    )(a, w, x, g3)


def _rms_kernel(x_ref, nw_ref, o_ref):
    x = x_ref[...]
    ms = jnp.mean(x * x, axis=-1, keepdims=True)
    o_ref[...] = x * lax.rsqrt(ms + NORM_EPS) * nw_ref[...]


def final_rmsnorm(x, nw):
    t, d = x.shape
    tm = min(TOKEN_TILE, t)
    return pl.pallas_call(
        _rms_kernel,
        grid=(t // tm,),
        in_specs=[pl.BlockSpec((tm, d), lambda i: (i, 0)), pl.BlockSpec((1, d), lambda i: (0, 0))],
        out_specs=pl.BlockSpec((tm, d), lambda i: (i, 0)),
        out_shape=jax.ShapeDtypeStruct((t, d), F32),
        compiler_params=_cparams(("parallel",)),
        name="final_rmsnorm",
    )(x, nw)


def _router_kernel(x_ref, nw_ref, sh_ref, sc_ref, rwt_ref, rb_ref, su_ref, sl_ref, hs_ref, g_ref, p_ref, cnt_ref,
                   run_ref, tot_ref, run_scr):
    @pl.when(pl.program_id(0) == 0)
    def _():
        run_scr[...] = jnp.zeros_like(run_scr)

    h = _norm_mod(x_ref[...], nw_ref[...], sh_ref[0], sc_ref[0])
    logits = lax.dot_general(rwt_ref[...], h, (((1,), (1,)), ((), ())),
                             precision=HIGHEST, preferred_element_type=F32) + rb_ref[...]
    eid = lax.broadcasted_iota(jnp.int32, logits.shape, 0)
    vals, idxs, hits = [], [], []
    for _ in range(TOP_K):
        m = jnp.max(logits, axis=0, keepdims=True)
        idx = jnp.min(jnp.where(logits == m, eid, N_EXPERTS), axis=0, keepdims=True)
        hit = eid == idx
        vals.append(m)
        idxs.append(idx)
        hits.append(hit)
        logits = jnp.where(hit, -jnp.inf, logits)
    ex = [jnp.exp(v - vals[0]) for v in vals]
    den = ex[0] + ex[1] + ex[2] + ex[3]
    cnt = (hits[0] | hits[1] | hits[2] | hits[3]).astype(F32)
    cnt_b = cnt.astype(BF16)
    earlier = jnp.dot(cnt_b, su_ref[...], preferred_element_type=F32)
    lower = jnp.dot(sl_ref[...], cnt_b, preferred_element_type=F32)
    cnt_tile = jnp.sum(cnt, axis=1, keepdims=True)
    where_to = earlier + jnp.sum(lower, axis=1, keepdims=True)
    tm = x_ref.shape[0]
    slot = lax.broadcasted_iota(jnp.int32, (TOP_K * tm, tm), 0)
    place = None
    for k in range(TOP_K):
        pos = jnp.sum(jnp.where(hits[k], where_to, 0.0), axis=0, keepdims=True).astype(jnp.int32)
        g_ref[k:k + 1, :] = ex[k] / den
        p_ref[k:k + 1, :] = pos
        place = (slot == pos) if place is None else place | (slot == pos)
    hs_ref[...] = jnp.dot(place.astype(BF16), h.astype(BF16), preferred_element_type=F32).astype(BF16)
    cnt_ref[0] = cnt_tile
    run_ref[0] = run_scr[...]
    run_scr[...] += cnt_tile
    tot_ref[...] = run_scr[...]


def moe_router(x, nw, sh3, sc3, rwt, rb, su, sl, rows_per_group):
    t, d = x.shape
    tm = min(TOKEN_TILE, t)
    n_tiles = t // tm
    return pl.pallas_call(
        _router_kernel,
        grid=(n_tiles,),
        in_specs=[pl.BlockSpec((tm, d), lambda i: (i, 0)),
                  pl.BlockSpec((1, d), lambda i: (0, 0)),
                  _mod_spec(sh3, tm, rows_per_group), _mod_spec(sc3, tm, rows_per_group),
                  pl.BlockSpec(rwt.shape, lambda i: (0, 0)),
                  pl.BlockSpec(rb.shape, lambda i: (0, 0)),
                  pl.BlockSpec(su.shape, lambda i: (0, 0)),
                  pl.BlockSpec(sl.shape, lambda i: (0, 0))],
        out_specs=[pl.BlockSpec((TOP_K * tm, d), lambda i: (i, 0)),
                   pl.BlockSpec((TOP_K, tm), lambda i: (0, i)),
                   pl.BlockSpec((TOP_K, tm), lambda i: (0, i)),
                   pl.BlockSpec((1, N_EXPERTS, 1), lambda i: (i, 0, 0)),
                   pl.BlockSpec((1, N_EXPERTS, 1), lambda i: (i, 0, 0)),
                   pl.BlockSpec((N_EXPERTS, 1), lambda i: (0, 0))],
        out_shape=[jax.ShapeDtypeStruct((TOP_K * t, d), BF16),
                   jax.ShapeDtypeStruct((TOP_K, t), F32),
                   jax.ShapeDtypeStruct((TOP_K, t), jnp.int32),
                   jax.ShapeDtypeStruct((n_tiles, N_EXPERTS, 1), F32),
                   jax.ShapeDtypeStruct((n_tiles, N_EXPERTS, 1), F32),
                   jax.ShapeDtypeStruct((N_EXPERTS, 1), F32)],
        scratch_shapes=[pltpu.VMEM((N_EXPERTS, 1), F32)],
        compiler_params=_cparams(("arbitrary",)),
        name="moe_router",
    )(x, nw, sh3, sc3, rwt, rb, su, sl)


def _ffn_kernel(be_ref, nu_ref, xs_ref, w1_ref, b1_ref, w2_ref, b2_ref, o_ref, w1_scr, w2_scr):
    i = pl.program_id(0)

    @pl.when(i < nu_ref[0])
    def _():
        @pl.when((i == 0) | (be_ref[i] != be_ref[jnp.maximum(i - 1, 0)]))
        def _():
            w1_scr[...] = w1_ref[0, 0].astype(BF16)
            w2_scr[...] = w2_ref[0, 0].astype(BF16)

        hid = jnp.dot(xs_ref[...], w1_scr[...], preferred_element_type=F32) + b1_ref[0, 0]
        glu = jnp.minimum(hid[:, :D_FF], SWIGLU_LIMIT)
        lin = jnp.clip(hid[:, D_FF:], -SWIGLU_LIMIT, SWIGLU_LIMIT)
        act = glu * jax.nn.sigmoid(SWIGLU_ALPHA * glu) * (lin + 1.0)
        out = jnp.dot(act.astype(BF16), w2_scr[...], preferred_element_type=F32) + b2_ref[0, 0]
        o_ref[...] = out.astype(o_ref.dtype)


def moe_ffn_blocks(xs, block_e, n_used, w1, b1, w2, b2, layer):
    n_rows, d = xs.shape
    tm = MOE_TILE
    grid_spec = pltpu.PrefetchScalarGridSpec(
        num_scalar_prefetch=2,
        grid=(n_rows // tm,),
        in_specs=[pl.BlockSpec((tm, d), lambda i, be, nu: (i, 0)),
                  pl.BlockSpec((1, 1, d, 2 * D_FF), lambda i, be, nu: (layer, be[i], 0, 0)),
                  pl.BlockSpec((1, 1, 1, 2 * D_FF), lambda i, be, nu: (layer, be[i], 0, 0)),
                  pl.BlockSpec((1, 1, D_FF, d), lambda i, be, nu: (layer, be[i], 0, 0)),
                  pl.BlockSpec((1, 1, 1, d), lambda i, be, nu: (layer, be[i], 0, 0))],
        out_specs=pl.BlockSpec((tm, d), lambda i, be, nu: (i, 0)),
        scratch_shapes=[pltpu.VMEM((d, 2 * D_FF), BF16), pltpu.VMEM((D_FF, d), BF16)],
    )
    return pl.pallas_call(
        _ffn_kernel,
        grid_spec=grid_spec,
        out_shape=jax.ShapeDtypeStruct((n_rows, d), BF16),
        compiler_params=_cparams(("arbitrary",)),
        name="moe_ffn",
    )(block_e, n_used, xs, w1, b1, w2, b2)


def _combine_kernel(y_ref, gt_ref, p_ref, x_ref, g_ref, o_ref):
    tm = x_ref.shape[0]
    eye = lax.broadcasted_iota(jnp.int32, (tm, tm), 0) == lax.broadcasted_iota(jnp.int32, (tm, tm), 1)
    col = lax.broadcasted_iota(jnp.int32, (tm, TOP_K * tm), 1)
    pick = jnp.zeros((tm, TOP_K * tm), F32)
    for k in range(TOP_K):
        g_col = jnp.sum(jnp.where(eye, gt_ref[k:k + 1, :], 0.0), axis=1, keepdims=True)
        p_col = jnp.sum(jnp.where(eye, p_ref[k:k + 1, :], 0), axis=1, keepdims=True)
        pick = jnp.where(col == p_col, g_col, pick)
    ff = jnp.dot(pick.astype(BF16), y_ref[...], preferred_element_type=F32)
    o_ref[...] = x_ref[...] + g_ref[0] * ff


def moe_combine(ysg, gates, pos, x, g3, rows_per_group):
    t, d = x.shape
    tm = min(TOKEN_TILE, t)
    return pl.pallas_call(
        _combine_kernel,
        grid=(t // tm,),
        in_specs=[pl.BlockSpec((TOP_K * tm, d), lambda i: (i, 0)),
                  pl.BlockSpec((TOP_K, tm), lambda i: (0, i)),
                  pl.BlockSpec((TOP_K, tm), lambda i: (0, i)),
                  pl.BlockSpec((tm, d), lambda i: (i, 0)),
                  _mod_spec(g3, tm, rows_per_group)],
        out_specs=pl.BlockSpec((tm, d), lambda i: (i, 0)),
        out_shape=jax.ShapeDtypeStruct((t, d), F32),
        compiler_params=_cparams(("parallel",)),
        name="moe_combine",
    )(ysg, gates, pos, x, g3)


def _take_rows(a, idx):
    return a.at[idx].get(mode="promise_in_bounds")


def moe_layer(x, nw, sh3, sc3, g3, rwt, rb, su, sl, w1, b1, w2, b2, layer, rows_per_group):
    t, d = x.shape
    tt = min(TOKEN_TILE, t)
    n_tiles, tl = t // tt, TOP_K * tt
    hs, gates, pos, cnt, run, tot = moe_router(x, nw, sh3, sc3, rwt, rb, su, sl, rows_per_group)
    tm = MOE_TILE
    n_blocks = (t * TOP_K + N_EXPERTS * (tm - 1) + tm - 1) // tm
    i32 = jnp.int32
    counts = tot[:, 0].astype(i32)
    padded = (counts + tm - 1) // tm * tm
    pend = jnp.cumsum(padded)
    pstart = (pend - padded).astype(i32)
    block_e = jnp.minimum(jnp.sum((pend[None, :] <= (jnp.arange(n_blocks) * tm)[:, None]).astype(i32), axis=1),
                          N_EXPERTS - 1)
    n_used = (pend[-1] // tm).astype(i32).reshape(1)
    cnt = cnt[:, :, 0].astype(i32)
    run = run[:, :, 0].astype(i32)
    eoff = jnp.cumsum(cnt, axis=1) - cnt
    tiles = jnp.arange(n_tiles, dtype=i32)
    in_e = (jnp.arange(n_blocks, dtype=i32) * tm - pstart[block_e])[:, None] + jnp.arange(tm, dtype=i32)[None, :]
    run_e = run.T[block_e]
    tile = jnp.sum((run_e[:, None, :] <= in_e[:, :, None]).astype(i32), axis=-1) - 1
    shift = jnp.sum(jnp.where(tiles[None, None, :] == tile[:, :, None], (eoff.T[block_e] - run_e)[:, None, :], 0),
                    axis=-1)
    src = jnp.where(in_e < counts[block_e][:, None], tile * tl + shift + in_e, 0)
    xs = _take_rows(hs, src.reshape(-1))
    ys = moe_ffn_blocks(xs, block_e, n_used, w1, b1, w2, b2, layer)
    j = jnp.arange(tl, dtype=i32)
    e_j = jnp.sum(((eoff + cnt)[:, None, :] <= j[None, :, None]).astype(i32), axis=-1)
    base = pstart[None, :] + run - eoff
    back = jnp.sum(jnp.where(jnp.arange(N_EXPERTS, dtype=i32)[None, None, :] == e_j[:, :, None], base[:, None, :], 0),
                   axis=-1) + j[None, :]
    ysg = _take_rows(ys, back.reshape(-1))
    return moe_combine(ysg, gates, pos, x, g3, rows_per_group)


def _split3(x):
    hi = x.astype(BF16)
    r1 = x - hi.astype(F32)
    mid = r1.astype(BF16)
    lo = (r1 - mid.astype(F32)).astype(BF16)
    return hi, mid, lo


def _ssd_kernel(z_ref, xbc_ref, dt_ref, cw_ref, cb_ref, dtb_ref, alog_ref, dx_ref, nw_ref, e_ref, tri_ref, shift_ref,
                ic_ref, is_ref, g_ref, so_ref, tail_scr, s_scr, y_scr, *, valid_rows):
    q = SSD_Q
    c = pl.program_id(1)

    @pl.when(c == 0)
    def _():
        tail_scr[...] = ic_ref[0]
        s_scr[...] = is_ref[0]

    xb = xbc_ref[...]
    acc = cb_ref[...] + xb.astype(F32) * cw_ref[CONV_W - 1:CONV_W, :]
    for tap in range(CONV_W - 1):
        shifted = jnp.dot(shift_ref[tap], xb, preferred_element_type=F32)
        acc = acc + shifted * cw_ref[tap:tap + 1, :]
    t0, t1, t2 = (tail_scr[8 - (CONV_W - 1) + j:8 - (CONV_W - 1) + j + 1, :] for j in range(CONV_W - 1))
    w0, w1, w2 = (cw_ref[j:j + 1, :] for j in range(CONV_W - 1))
    row8 = lax.broadcasted_iota(jnp.int32, (8, CONV_DIM), 0)
    head = jnp.where(row8 == 0, t0 * w0 + t1 * w1 + t2 * w2,
                     jnp.where(row8 == 1, t1 * w0 + t2 * w1, jnp.where(row8 == 2, t2 * w0, 0.0)))
    tail_scr[...] = xbc_ref[q - 16:q, :].astype(F32)[8:16, :]
    xc = _silu(jnp.concatenate([acc[0:8, :] + head, acc[8:q, :]], axis=0))
    xs = xc[:, :D_INNER]

    dt = _softplus(dt_ref[...] + dtb_ref[...])
    if valid_rows < q:
        rows = lax.broadcasted_iota(jnp.int32, dt.shape, 0)
        dt = jnp.where(rows < valid_rows, dt, 0.0)
    a = dt * (-jnp.exp(alog_ref[...]))
    a_cs = jnp.dot(tri_ref[...], jnp.concatenate(_split3(a), axis=0), preferred_element_type=F32)
    dtx = jnp.dot(jnp.concatenate(_split3(dt), axis=1), e_ref[...], preferred_element_type=F32)
    acsx = jnp.dot(jnp.concatenate(_split3(a_cs), axis=1), e_ref[...], preferred_element_type=F32)

    row = lax.broadcasted_iota(jnp.int32, (q, D_INNER), 0)
    col = lax.broadcasted_iota(jnp.int32, (q, D_INNER), 1) & (HEAD_DIM_A - 1)
    acs_row = jnp.sum(jnp.where(row == col, acsx, 0.0), axis=0, keepdims=True)
    decay_in = jnp.where(row >= col, jnp.exp(jnp.minimum(acsx - acs_row, 0.0)), 0.0)
    last = acsx[q - 1:q, :]
    e_in = jnp.exp(acsx)
    e_last = jnp.exp(last)
    xd = xs * dtx
    xdd = xd * jnp.exp(last - acsx)
    lane = lax.broadcasted_iota(jnp.int32, (q, LANE), 1)
    lo = lane < HEAD_DIM_A

    for g in range(N_GROUPS):
        gsl = slice(g * GROUP_W, (g + 1) * GROUP_W)
        b_off = D_INNER + g * D_STATE
        c_off = D_INNER + N_GROUPS * D_STATE + g * D_STATE
        bg = xc[:, b_off:b_off + D_STATE]
        cgb = xc[:, c_off:c_off + D_STATE].astype(BF16)
        bgb = bg.astype(BF16)
        b2 = jnp.concatenate([bgb, bgb], axis=0)
        cb2 = lax.dot_general(cgb, b2, (((1,), (1,)), ((), ())), preferred_element_type=F32)
        s_g = s_scr[:, gsl]
        y_off = jnp.dot(cgb, s_g.astype(BF16), preferred_element_type=F32) * e_in[:, gsl]
        for pp in range(GROUP_W // LANE):
            sl = slice(g * GROUP_W + pp * LANE, g * GROUP_W + (pp + 1) * LANE)
            m = (decay_in[:, sl] * cb2).astype(BF16)
            xp = xd[:, sl]
            bd = jnp.concatenate([jnp.where(lo, xp, 0.0), jnp.where(lo, 0.0, xp)], axis=0).astype(BF16)
            y_diag = jnp.dot(m, bd, preferred_element_type=F32)
            y_scr[:, sl] = y_diag + y_off[:, pp * LANE:(pp + 1) * LANE] + xs[:, sl] * dx_ref[:, sl]
        s_new = jnp.dot(bg.T.astype(BF16), xdd[:, gsl].astype(BF16), preferred_element_type=F32)
        s_scr[:, gsl] = s_g * e_last[:, gsl] + s_new

    for g in range(N_GROUPS):
        gsl = slice(g * GROUP_W, (g + 1) * GROUP_W)
        gz = y_scr[:, gsl] * _silu(z_ref[:, gsl].astype(F32))
        ms = jnp.mean(gz * gz, axis=-1, keepdims=True)
        g_ref[:, gsl] = (gz * lax.rsqrt(ms + NORM_EPS) * nw_ref[:, gsl]).astype(g_ref.dtype)

    @pl.when(c == pl.num_programs(1) - 1)
    def _():
        so_ref[0] = s_scr[...]


def ssd_mixer(z, xbc, dt, cw, cb, dtb, alog, dx, nw, e_mat, tri, shift, init_conv, init_ssm, nb, nc, valid_rows):
    q = SSD_Q
    const = lambda a: pl.BlockSpec(a.shape, lambda b, c: (0,) * a.ndim)
    return pl.pallas_call(
        functools.partial(_ssd_kernel, valid_rows=valid_rows),
        grid=(nb, nc),
        in_specs=[pl.BlockSpec((q, D_INNER), lambda b, c: (b * nc + c, 0)),
                  pl.BlockSpec((q, CONV_DIM), lambda b, c: (b * nc + c, 0)),
                  pl.BlockSpec((q, LANE), lambda b, c: (b * nc + c, 0)),
                  const(cw), const(cb), const(dtb), const(alog), const(dx), const(nw), const(e_mat), const(tri),
                  const(shift),
                  pl.BlockSpec((1, 8, CONV_DIM), lambda b, c: (b, 0, 0)),
                  pl.BlockSpec((1, D_STATE, D_INNER), lambda b, c: (b, 0, 0))],
        out_specs=[pl.BlockSpec((q, D_INNER), lambda b, c: (b * nc + c, 0)),
                   pl.BlockSpec((1, D_STATE, D_INNER), lambda b, c: (b, 0, 0))],
        out_shape=[jax.ShapeDtypeStruct((nb * nc * q, D_INNER), BF16),
                   jax.ShapeDtypeStruct((nb, D_STATE, D_INNER), F32)],
        scratch_shapes=[pltpu.VMEM((8, CONV_DIM), F32),
                        pltpu.VMEM((D_STATE, D_INNER), F32),
                        pltpu.VMEM((q, D_INNER), F32)],
        compiler_params=_cparams(("parallel", "arbitrary")),
        name="ssd_mixer",
    )(z, xbc, dt, cw, cb, dtb, alog, dx, nw, e_mat, tri, shift, init_conv, init_ssm)


def _attn_kernel(q_ref, k_ref, v_ref, u_ref, o_ref, qs_scr, acc_scr, c_scr, y_scr, t_scr, tot_scr,
                 *, tq, tk, pos0, n_kblocks):
    qi = pl.program_id(2)
    qv = q_ref[0]
    lane_q = lax.broadcasted_iota(jnp.int32, qv.shape, 1)
    zero_q = jnp.zeros_like(qv)
    qs_scr[0:tq, :] = jnp.where(lane_q < HEAD_DIM_B, qv, zero_q)
    qs_scr[tq:2 * tq, :] = jnp.where(lane_q < HEAD_DIM_B, zero_q, qv)
    acc_scr[...] = jnp.zeros_like(acc_scr)
    c_scr[...] = jnp.zeros_like(c_scr)
    y_scale = LOG2_E / math.sqrt(HEAD_DIM_B)
    q_idx = lax.broadcasted_iota(jnp.int32, (2 * tq, tk), 0) & (tq - 1)
    pos_gap = pos0 + qi * tq + q_idx - lax.broadcasted_iota(jnp.int32, (2 * tq, tk), 1)
    n_vis = jnp.minimum((pos0 + (qi + 1) * tq - 2) // tk + 1, n_kblocks)
    n_full = jnp.minimum((pos0 + qi * tq) // tk, n_kblocks)
    n_masked = n_vis - n_full

    def kblock(i):
        return jnp.maximum(n_vis - 1 - i, 0)

    def scores(i, dst):
        start = pl.multiple_of(kblock(i) * tk, tk)
        kb = k_ref[0, pl.ds(start, tk), :]
        y = lax.dot_general(qs_scr[...], kb, (((1,), (1,)), ((), ())), preferred_element_type=F32)
        y_scr[dst] = y * y_scale

    def suffix(i, masked, src, dst):
        y = y_scr[src]
        sp = jnp.maximum(y, jnp.log(1.0 + jnp.exp2(jnp.minimum(y, 126.0))) * LOG2_E)
        if masked:
            vis = pos_gap > kblock(i) * tk
            sp = jnp.where(vis, sp, 0.0)
        cs = jnp.dot(sp.astype(BF16), u_ref[...], preferred_element_type=F32)
        t = jnp.minimum(y - cs, 0.0)
        if masked:
            t = jnp.where(vis, t, HIDDEN)
        t_scr[dst] = t
        tot_scr[dst] = cs[:, 0:1]

    def weigh(i, src):
        start = pl.multiple_of(kblock(i) * tk, tk)
        vb = v_ref[0, pl.ds(start, tk), :]
        c = c_scr[...]
        w = jnp.exp2(t_scr[src] - c)
        acc_scr[...] += jnp.dot(w.astype(BF16), vb, preferred_element_type=F32)
        c_scr[...] = c + tot_scr[src]

    scores(0, 0)
    suffix(0, True, 0, 0)
    scores(1, 0)

    def trip(i, masked):
        weigh(i, 0)
        suffix(i + 1, masked, 0, 0)
        scores(i + 2, 0)

    def group(size):
        def body(i):
            for j in range(size):
                cur, nxt = j & 1, (j + 1) & 1
                scores(i + j + 2, nxt)
                suffix(i + j + 1, False, cur, nxt)
                weigh(i + j, cur)
        return body

    def run(start, stop, step, body):
        def cond(i):
            return (i < stop) & (jnp.min(c_scr[...]) < SKIP_BITS)

        def step_body(i):
            body(i)
            return i + step

        lax.while_loop(cond, step_body, start)

    n_mt = jnp.maximum(n_masked - 1, 0)
    first_pair = n_mt + ((n_vis - n_mt) & 1)
    first_group = first_pair + ((n_vis - first_pair) % ATTN_UNROLL)
    run(0, n_mt, 1, lambda i: trip(i, True))
    run(n_mt, first_pair, 1, lambda i: trip(i, False))
    run(first_pair, first_group, 2, group(2))
    run(first_group, n_vis, ATTN_UNROLL, group(ATTN_UNROLL))
    lane_o = lax.broadcasted_iota(jnp.int32, (tq, LANE), 1)
    o_ref[0] = jnp.where(lane_o < HEAD_DIM_B, acc_scr[0:tq, :], acc_scr[tq:2 * tq, :]).astype(o_ref.dtype)


def stick_breaking(q, k, v, u, pos0, tq):
    b, l, hd = q.shape
    kp = k.shape[1]
    tk = ATTN_TK
    assert tq & (tq - 1) == 0 and l % tq == 0 and kp % tk == 0
    return pl.pallas_call(
        functools.partial(_attn_kernel, tq=tq, tk=tk, pos0=pos0, n_kblocks=kp // tk),
        grid=(b, hd // LANE, l // tq),
        in_specs=[pl.BlockSpec((1, tq, LANE), lambda bi, p, i: (bi, i, p)),
                  pl.BlockSpec((1, kp, LANE), lambda bi, p, i: (bi, 0, p)),
                  pl.BlockSpec((1, kp, LANE), lambda bi, p, i: (bi, 0, p)),
                  pl.BlockSpec(u.shape, lambda bi, p, i: (0, 0))],
        out_specs=pl.BlockSpec((1, tq, LANE), lambda bi, p, i: (bi, i, p)),
        out_shape=jax.ShapeDtypeStruct((b, l, hd), BF16),
        scratch_shapes=[pltpu.VMEM((2 * tq, LANE), BF16), pltpu.VMEM((2 * tq, LANE), F32),
                        pltpu.VMEM((2 * tq, 1), F32), pltpu.VMEM((2, 2 * tq, tk), F32),
                        pltpu.VMEM((2, 2 * tq, tk), F32), pltpu.VMEM((2, 2 * tq, 1), F32)],
        compiler_params=_cparams(("parallel", "parallel", "arbitrary")),
        name="stick_breaking",
    )(q, k, v, u)


def _trunk(x3, mods, kv_mod, conv_state, ssm_state, past_k, past_v, p):
    b, l, d = x3.shape
    t = b * l
    x = x3.reshape(t, d)
    per_token = l < TOKEN_TILE

    def mod3(v):
        if per_token:
            return jnp.repeat(v, l, axis=0).reshape(t // min(TOKEN_TILE, t), min(TOKEN_TILE, t), d)
        return v[:, None, :]

    rows_per_group = min(TOKEN_TILE, t) if per_token else l
    pos0 = past_k.shape[1]
    conv_out, ssm_out = [], []
    k_new = v_new = kb = vb = None
    for layer in range(DEPTH):
        sh1, sc1, g1, sh2, sc2, g2 = jnp.split(mods[layer], 6, axis=-1)
        nw1 = p['mix_norm_w'][layer][None, :]
        if layer < N_A_LAYERS:
            i = layer
            z, xbc, dt = norm_mod_matmul(x, nw1, mod3(sh1), mod3(sc1), [p['w_z'][i], p['w_xbc'][i], p['w_dt'][i]],
                                         [BF16, BF16, F32], rows_per_group)
            xbc3 = xbc.reshape(b, l, CONV_DIM)
            conv_out.append(xbc3[:, l - (CONV_W - 1):].astype(F32))
            nc = max(l // SSD_Q, 1)
            if l < SSD_Q:
                pad = lambda a: jnp.pad(a.reshape(b, l, -1), ((0, 0), (0, SSD_Q - l), (0, 0))).reshape(b * SSD_Q, -1)
                z, xbc, dt = pad(z), pad(xbc), pad(dt)
            ic = jnp.pad(conv_state[i].astype(F32), ((0, 0), (8 - (CONV_W - 1), 0), (0, 0)))
            iss = ssm_state[i].astype(F32).transpose(0, 3, 1, 2).reshape(b, D_STATE, D_INNER)
            gm, s_fin = ssd_mixer(z, xbc, dt, p['a_conv_w'][i], p['a_conv_b'][i][None, :], p['dt_bias'][i],
                                  p['a_log'][i], p['d_x'][i], p['a_norm_w'][i][None, :], p['e_mat'], p['tri'], p['shift'],
                                  ic, iss, b, nc, min(l, SSD_Q))
            if l < SSD_Q:
                gm = gm.reshape(b, SSD_Q, D_INNER)[:, :l].reshape(t, D_INNER)
            ssm_out.append(s_fin.reshape(b, D_STATE, N_HEADS_A, HEAD_DIM_A).transpose(0, 2, 3, 1))
            x = matmul_residual(gm, p['a_out_proj'][i], x, mod3(g1), rows_per_group)
        else:
            j = layer - N_A_LAYERS
            (qp,) = norm_mod_matmul(x, nw1, mod3(sh1), mod3(sc1), [p['b_w_q'][j]], [BF16], rows_per_group)
            o = stick_breaking(qp.reshape(b, l, d), kb, vb, p['u_tri'], pos0, min(l, 256))
            x = matmul_residual(o.reshape(t, d), p['b_w_o'][j], x, mod3(g1), rows_per_group)
        x = moe_layer(x, p['ffn_norm_w'][layer][None, :], mod3(sh2), mod3(sc2), mod3(g2),
                      p['router_wt'][layer], p['router_b'][layer], p['su_tri'], p['sl_tri'], p['moe_w1'], p['moe_b1'],
                      p['moe_w2'], p['moe_b2'], layer, rows_per_group)
        if layer == N_A_LAYERS - 1:
            sh_kv, sc_kv = jnp.split(kv_mod, 2, axis=-1)
            k_new, v_new = norm_mod_matmul(x, p['kv_norm_w'][None, :], mod3(sh_kv), mod3(sc_kv),
                                           [p['w_k'], p['w_v']], [F32, F32], rows_per_group)
            k_all = jnp.concatenate([past_k.reshape(b, pos0, d), k_new.reshape(b, l, d)], axis=1)
            v_all = jnp.concatenate([past_v.reshape(b, pos0, d), v_new.reshape(b, l, d)], axis=1)
            kpad = (-k_all.shape[1]) % ATTN_TK
            kb = jnp.pad(k_all, ((0, 0), (0, kpad), (0, 0))).astype(BF16)
            vb = jnp.pad(v_all, ((0, 0), (0, kpad), (0, 0))).astype(BF16)
    y = final_rmsnorm(x, p['final_norm_w'][None, :])
    return (y.reshape(b, l, d), jnp.stack(conv_out), jnp.stack(ssm_out),
            k_new.reshape(b, l, N_HEADS_B, HEAD_DIM_B), v_new.reshape(b, l, N_HEADS_B, HEAD_DIM_B))


def kernel(x_prompt, x_sample, state_conv, state_ssm, cache_k, cache_v, c_prompt, c_sample, mod_w, mod_b,
           mix_norm_w, ffn_norm_w, a_in_proj, a_conv_w, a_conv_b, a_dt_bias, a_A_log, a_D, a_norm_w, a_out_proj,
           kv_mod_w, kv_mod_b, kv_norm_w, w_kv, b_w_q, b_w_o, router_w, router_b, moe_w1, moe_b1, moe_w2, moe_b2,
           final_norm_w):
    bp = x_prompt.shape[0]
    hb = N_HEADS_B * HEAD_DIM_B
    pad_heads = lambda a: jnp.pad(a, ((0, 0), (0, LANE - N_HEADS_A)))[:, None, :]
    head_of_lane = jnp.arange(D_INNER) // HEAD_DIM_A
    p = dict(
        mix_norm_w=mix_norm_w, ffn_norm_w=ffn_norm_w, kv_norm_w=kv_norm_w, final_norm_w=final_norm_w,
        w_z=a_in_proj[:, :, :D_INNER].astype(BF16),
        w_xbc=a_in_proj[:, :, D_INNER:D_INNER + CONV_DIM].astype(BF16),
        w_dt=jnp.pad(a_in_proj[:, :, D_INNER + CONV_DIM:], ((0, 0), (0, 0), (0, LANE - N_HEADS_A))).astype(BF16),
        a_conv_w=a_conv_w, a_conv_b=a_conv_b, dt_bias=pad_heads(a_dt_bias), a_log=pad_heads(a_A_log),
        d_x=a_D[:, head_of_lane][:, None, :], a_norm_w=a_norm_w, a_out_proj=a_out_proj.astype(BF16),
        e_mat=jnp.tile((jnp.arange(LANE)[:, None] == head_of_lane[None, :]).astype(BF16), (3, 1)),
        tri=jnp.tile((jnp.arange(SSD_Q)[:, None] >= jnp.arange(SSD_Q)[None, :]).astype(BF16), (1, 3)),
        shift=jnp.stack([(jnp.arange(SSD_Q)[:, None] + tap - (CONV_W - 1) == jnp.arange(SSD_Q)[None, :])
                         for tap in range(CONV_W - 1)]).astype(BF16),
        u_tri=(jnp.arange(ATTN_TK)[:, None] >= jnp.arange(ATTN_TK)[None, :]).astype(BF16),
        w_k=w_kv[:, :hb].astype(BF16), w_v=w_kv[:, hb:].astype(BF16),
        b_w_q=b_w_q.astype(BF16), b_w_o=b_w_o.astype(BF16),
        router_wt=router_w.transpose(0, 2, 1), router_b=router_b[:, :, None],
        su_tri=(jnp.arange(TOKEN_TILE)[:, None] < jnp.arange(TOKEN_TILE)[None, :]).astype(BF16),
        sl_tri=(jnp.arange(N_EXPERTS)[:, None] > jnp.arange(N_EXPERTS)[None, :]).astype(BF16),
        moe_w1=moe_w1, moe_b1=moe_b1[:, :, None, :], moe_w2=moe_w2, moe_b2=moe_b2[:, :, None, :],
    )
    c_all = jnp.concatenate([c_prompt, c_sample], axis=0)
    mods = mod_vectors(c_all, mod_w, mod_b[:, None, :])
    kv_mod = mod_vectors(c_all, kv_mod_w[None], kv_mod_b[None, None, :])[0]

    bg = bp // PROMPT_GROUPS
    conv0 = jnp.zeros((N_A_LAYERS, bg, CONV_W - 1, CONV_DIM), F32)
    ssm0 = jnp.zeros((N_A_LAYERS, bg, N_HEADS_A, HEAD_DIM_A, D_STATE), F32)
    kv0 = jnp.zeros((bg, 0, N_HEADS_B, HEAD_DIM_B), F32)
    groups = [_trunk(x_prompt[g * bg:(g + 1) * bg], mods[:, g * bg:(g + 1) * bg], kv_mod[g * bg:(g + 1) * bg],
                     conv0, ssm0, kv0, kv0, p) for g in range(PROMPT_GROUPS)]
    out_p = tuple(jnp.concatenate([grp[i] for grp in groups], axis=ax) for i, ax in enumerate((0, 1, 1, 0, 0)))
    out_s = _trunk(x_sample, mods[:, bp:], kv_mod[bp:], state_conv, state_ssm, cache_k, cache_v, p)
    return (out_p[0], out_s[0]) + out_p[1:] + out_s[1:]
```

```python
import functools
import math

import jax
import jax.numpy as jnp
from jax import lax
from jax.experimental import pallas as pl
from jax.experimental.pallas import tpu as pltpu

F32 = jnp.float32
BF16 = jnp.bfloat16
HIGHEST = lax.Precision.HIGHEST

D_MODEL = 1024
DEPTH = 4
N_A_LAYERS = 2
D_INNER = 2048
HEAD_DIM_A = 64
N_HEADS_A = 32
D_STATE = 128
N_GROUPS = 4
GROUP_W = D_INNER // N_GROUPS
CONV_W = 4
CONV_DIM = D_INNER + 2 * N_GROUPS * D_STATE
HEAD_DIM_B = 64
N_HEADS_B = 16
N_EXPERTS = 32
TOP_K = 4
D_FF = 1024
SWIGLU_ALPHA = 1.702
SWIGLU_LIMIT = 7.0
NORM_EPS = 1e-6
LOG2_E = 1.4426950408889634
HIDDEN = -1e30
SKIP_BITS = 160.0

LANE = 128
SSD_Q = 64
TOKEN_TILE = 512
MOE_TILE = 512
ATTN_TK = 256
ATTN_UNROLL = 4
PROMPT_GROUPS = 2
VMEM_LIMIT = 56 * 1024 * 1024


def _cparams(sem):
    return pltpu.CompilerParams(dimension_semantics=sem, vmem_limit_bytes=VMEM_LIMIT)


def _softplus(x):
    return jnp.maximum(x, 0.0) + jnp.log1p(jnp.exp(-jnp.abs(x)))


def _silu(x):
    return x * jax.nn.sigmoid(x)


def _norm_mod(x, nw, sh, sc):
    ms = jnp.mean(x * x, axis=-1, keepdims=True)
    return (x * lax.rsqrt(ms + NORM_EPS) * nw) * (1.0 + sc) + sh


def _mod_spec(mod3, tile, rows_per_group):
    steps = max(rows_per_group // tile, 1)
    g = mod3.shape[1]
    return pl.BlockSpec((1, g, mod3.shape[2]), lambda i: (i // steps, 0, 0))


def _mod_kernel(c_ref, w_ref, b_ref, o_ref):
    cs = _silu(c_ref[...])
    o_ref[0] = jnp.dot(cs, w_ref[0], precision=HIGHEST, preferred_element_type=F32) + b_ref[0]


def mod_vectors(c, w, b):
    nl, d, n = w.shape
    tn = 1024
    return pl.pallas_call(
        _mod_kernel,
        grid=(nl, n // tn),
        in_specs=[pl.BlockSpec(c.shape, lambda l, j: (0, 0)),
                  pl.BlockSpec((1, d, tn), lambda l, j: (l, 0, j)),
                  pl.BlockSpec((1, 1, tn), lambda l, j: (l, 0, j))],
        out_specs=pl.BlockSpec((1, c.shape[0], tn), lambda l, j: (l, 0, j)),
        out_shape=jax.ShapeDtypeStruct((nl, c.shape[0], n), F32),
        compiler_params=_cparams(("arbitrary", "arbitrary")),
        name="mod_vectors",
    )(c, w, b)


def _nmm_kernel(x_ref, nw_ref, sh_ref, sc_ref, *refs, n_w):
    h = _norm_mod(x_ref[...], nw_ref[...], sh_ref[0], sc_ref[0]).astype(BF16)
    for w_ref, o_ref in zip(refs[:n_w], refs[n_w:]):
        o_ref[...] = jnp.dot(h, w_ref[...], preferred_element_type=F32).astype(o_ref.dtype)


def norm_mod_matmul(x, nw, sh3, sc3, ws, out_dtypes, rows_per_group):
    t, d = x.shape
    tm = min(TOKEN_TILE, t)
    in_specs = [pl.BlockSpec((tm, d), lambda i: (i, 0)),
                pl.BlockSpec((1, d), lambda i: (0, 0)),
                _mod_spec(sh3, tm, rows_per_group), _mod_spec(sc3, tm, rows_per_group)]
    in_specs += [pl.BlockSpec(w.shape, lambda i: (0, 0), pipeline_mode=pl.Buffered(1)) for w in ws]
    return pl.pallas_call(
        functools.partial(_nmm_kernel, n_w=len(ws)),
        grid=(t // tm,),
        in_specs=in_specs,
        out_specs=[pl.BlockSpec((tm, w.shape[1]), lambda i: (i, 0)) for w in ws],
        out_shape=[jax.ShapeDtypeStruct((t, w.shape[1]), dt) for w, dt in zip(ws, out_dtypes)],
        compiler_params=_cparams(("parallel",)),
        name="norm_mod_matmul",
    )(x, nw, sh3, sc3, *ws)


def _mmres_kernel(a_ref, w_ref, x_ref, g_ref, o_ref):
    acc = jnp.dot(a_ref[...], w_ref[...], preferred_element_type=F32)
    o_ref[...] = x_ref[...] + g_ref[0] * acc


def matmul_residual(a, w, x, g3, rows_per_group):
    t, d = x.shape
    tm = min(TOKEN_TILE, t)
    return pl.pallas_call(
        _mmres_kernel,
        grid=(t // tm,),
        in_specs=[pl.BlockSpec((tm, a.shape[1]), lambda i: (i, 0)),
                  pl.BlockSpec(w.shape, lambda i: (0, 0), pipeline_mode=pl.Buffered(1)),
                  pl.BlockSpec((tm, d), lambda i: (i, 0)),
                  _mod_spec(g3, tm, rows_per_group)],
        out_specs=pl.BlockSpec((tm, d), lambda i: (i, 0)),
        out_shape=jax.ShapeDtypeStruct((t, d), F32),
        compiler_params=_cparams(("parallel",)),
        name="matmul_residual",
    )(a, w, x, g3)


def _rms_kernel(x_ref, nw_ref, o_ref):
    x = x_ref[...]
    ms = jnp.mean(x * x, axis=-1, keepdims=True)
    o_ref[...] = x * lax.rsqrt(ms + NORM_EPS) * nw_ref[...]


def final_rmsnorm(x, nw):
    t, d = x.shape
    tm = min(TOKEN_TILE, t)
    return pl.pallas_call(
        _rms_kernel,
        grid=(t // tm,),
        in_specs=[pl.BlockSpec((tm, d), lambda i: (i, 0)), pl.BlockSpec((1, d), lambda i: (0, 0))],
        out_specs=pl.BlockSpec((tm, d), lambda i: (i, 0)),
        out_shape=jax.ShapeDtypeStruct((t, d), F32),
        compiler_params=_cparams(("parallel",)),
        name="final_rmsnorm",
    )(x, nw)


def _router_kernel(x_ref, nw_ref, sh_ref, sc_ref, rwt_ref, rb_ref, su_ref, sl_ref, hs_ref, g_ref, p_ref, cnt_ref,
                   run_ref, tot_ref, run_scr):
    @pl.when(pl.program_id(0) == 0)
    def _():
        run_scr[...] = jnp.zeros_like(run_scr)

    h = _norm_mod(x_ref[...], nw_ref[...], sh_ref[0], sc_ref[0])
    logits = lax.dot_general(rwt_ref[...], h, (((1,), (1,)), ((), ())),
                             precision=HIGHEST, preferred_element_type=F32) + rb_ref[...]
    eid = lax.broadcasted_iota(jnp.int32, logits.shape, 0)
    vals, idxs, hits = [], [], []
    for _ in range(TOP_K):
        m = jnp.max(logits, axis=0, keepdims=True)
        idx = jnp.min(jnp.where(logits == m, eid, N_EXPERTS), axis=0, keepdims=True)
        hit = eid == idx
        vals.append(m)
        idxs.append(idx)
        hits.append(hit)
        logits = jnp.where(hit, -jnp.inf, logits)
    ex = [jnp.exp(v - vals[0]) for v in vals]
    den = ex[0] + ex[1] + ex[2] + ex[3]
    cnt = (hits[0] | hits[1] | hits[2] | hits[3]).astype(F32)
    cnt_b = cnt.astype(BF16)
    earlier = jnp.dot(cnt_b, su_ref[...], preferred_element_type=F32)
    lower = jnp.dot(sl_ref[...], cnt_b, preferred_element_type=F32)
    cnt_tile = jnp.sum(cnt, axis=1, keepdims=True)
    where_to = earlier + jnp.sum(lower, axis=1, keepdims=True)
    tm = x_ref.shape[0]
    slot = lax.broadcasted_iota(jnp.int32, (TOP_K * tm, tm), 0)
    place = None
    for k in range(TOP_K):
        pos = jnp.sum(jnp.where(hits[k], where_to, 0.0), axis=0, keepdims=True).astype(jnp.int32)
        g_ref[k:k + 1, :] = ex[k] / den
        p_ref[k:k + 1, :] = pos
        place = (slot == pos) if place is None else place | (slot == pos)
    hs_ref[...] = jnp.dot(place.astype(BF16), h.astype(BF16), preferred_element_type=F32).astype(BF16)
    cnt_ref[0] = cnt_tile
    run_ref[0] = run_scr[...]
    run_scr[...] += cnt_tile
    tot_ref[...] = run_scr[...]


def moe_router(x, nw, sh3, sc3, rwt, rb, su, sl, rows_per_group):
    t, d = x.shape
    tm = min(TOKEN_TILE, t)
    n_tiles = t // tm
    return pl.pallas_call(
        _router_kernel,
        grid=(n_tiles,),
        in_specs=[pl.BlockSpec((tm, d), lambda i: (i, 0)),
                  pl.BlockSpec((1, d), lambda i: (0, 0)),
                  _mod_spec(sh3, tm, rows_per_group), _mod_spec(sc3, tm, rows_per_group),
                  pl.BlockSpec(rwt.shape, lambda i: (0, 0)),
                  pl.BlockSpec(rb.shape, lambda i: (0, 0)),
                  pl.BlockSpec(su.shape, lambda i: (0, 0)),
                  pl.BlockSpec(sl.shape, lambda i: (0, 0))],
        out_specs=[pl.BlockSpec((TOP_K * tm, d), lambda i: (i, 0)),
                   pl.BlockSpec((TOP_K, tm), lambda i: (0, i)),
                   pl.BlockSpec((TOP_K, tm), lambda i: (0, i)),
                   pl.BlockSpec((1, N_EXPERTS, 1), lambda i: (i, 0, 0)),
                   pl.BlockSpec((1, N_EXPERTS, 1), lambda i: (i, 0, 0)),
                   pl.BlockSpec((N_EXPERTS, 1), lambda i: (0, 0))],
        out_shape=[jax.ShapeDtypeStruct((TOP_K * t, d), BF16),
                   jax.ShapeDtypeStruct((TOP_K, t), F32),
                   jax.ShapeDtypeStruct((TOP_K, t), jnp.int32),
                   jax.ShapeDtypeStruct((n_tiles, N_EXPERTS, 1), F32),
                   jax.ShapeDtypeStruct((n_tiles, N_EXPERTS, 1), F32),
                   jax.ShapeDtypeStruct((N_EXPERTS, 1), F32)],
        scratch_shapes=[pltpu.VMEM((N_EXPERTS, 1), F32)],
        compiler_params=_cparams(("arbitrary",)),
        name="moe_router",
    )(x, nw, sh3, sc3, rwt, rb, su, sl)


def _ffn_kernel(be_ref, nu_ref, xs_ref, w1_ref, b1_ref, w2_ref, b2_ref, o_ref, w1_scr, w2_scr):
    i = pl.program_id(0)

    @pl.when(i < nu_ref[0])
    def _():
        @pl.when((i == 0) | (be_ref[i] != be_ref[jnp.maximum(i - 1, 0)]))
        def _():
            w1_scr[...] = w1_ref[0, 0].astype(BF16)
            w2_scr[...] = w2_ref[0, 0].astype(BF16)

        hid = jnp.dot(xs_ref[...], w1_scr[...], preferred_element_type=F32) + b1_ref[0, 0]
        glu = jnp.minimum(hid[:, :D_FF], SWIGLU_LIMIT)
        lin = jnp.clip(hid[:, D_FF:], -SWIGLU_LIMIT, SWIGLU_LIMIT)
        act = glu * jax.nn.sigmoid(SWIGLU_ALPHA * glu) * (lin + 1.0)
        out = jnp.dot(act.astype(BF16), w2_scr[...], preferred_element_type=F32) + b2_ref[0, 0]
        o_ref[...] = out.astype(o_ref.dtype)


def moe_ffn_blocks(xs, block_e, n_used, w1, b1, w2, b2, layer):
    n_rows, d = xs.shape
    tm = MOE_TILE
    grid_spec = pltpu.PrefetchScalarGridSpec(
        num_scalar_prefetch=2,
        grid=(n_rows // tm,),
        in_specs=[pl.BlockSpec((tm, d), lambda i, be, nu: (i, 0)),
                  pl.BlockSpec((1, 1, d, 2 * D_FF), lambda i, be, nu: (layer, be[i], 0, 0)),
                  pl.BlockSpec((1, 1, 1, 2 * D_FF), lambda i, be, nu: (layer, be[i], 0, 0)),
                  pl.BlockSpec((1, 1, D_FF, d), lambda i, be, nu: (layer, be[i], 0, 0)),
                  pl.BlockSpec((1, 1, 1, d), lambda i, be, nu: (layer, be[i], 0, 0))],
        out_specs=pl.BlockSpec((tm, d), lambda i, be, nu: (i, 0)),
        scratch_shapes=[pltpu.VMEM((d, 2 * D_FF), BF16), pltpu.VMEM((D_FF, d), BF16)],
    )
    return pl.pallas_call(
        _ffn_kernel,
        grid_spec=grid_spec,
        out_shape=jax.ShapeDtypeStruct((n_rows, d), BF16),
        compiler_params=_cparams(("arbitrary",)),
        name="moe_ffn",
    )(block_e, n_used, xs, w1, b1, w2, b2)


def _combine_kernel(y_ref, gt_ref, p_ref, x_ref, g_ref, o_ref):
    tm = x_ref.shape[0]
    eye = lax.broadcasted_iota(jnp.int32, (tm, tm), 0) == lax.broadcasted_iota(jnp.int32, (tm, tm), 1)
    col = lax.broadcasted_iota(jnp.int32, (tm, TOP_K * tm), 1)
    pick = jnp.zeros((tm, TOP_K * tm), F32)
    for k in range(TOP_K):
        g_col = jnp.sum(jnp.where(eye, gt_ref[k:k + 1, :], 0.0), axis=1, keepdims=True)
        p_col = jnp.sum(jnp.where(eye, p_ref[k:k + 1, :], 0), axis=1, keepdims=True)
        pick = jnp.where(col == p_col, g_col, pick)
    ff = jnp.dot(pick.astype(BF16), y_ref[...], preferred_element_type=F32)
    o_ref[...] = x_ref[...] + g_ref[0] * ff


def moe_combine(ysg, gates, pos, x, g3, rows_per_group):
    t, d = x.shape
    tm = min(TOKEN_TILE, t)
    return pl.pallas_call(
        _combine_kernel,
        grid=(t // tm,),
        in_specs=[pl.BlockSpec((TOP_K * tm, d), lambda i: (i, 0)),
                  pl.BlockSpec((TOP_K, tm), lambda i: (0, i)),
                  pl.BlockSpec((TOP_K, tm), lambda i: (0, i)),
                  pl.BlockSpec((tm, d), lambda i: (i, 0)),
                  _mod_spec(g3, tm, rows_per_group)],
        out_specs=pl.BlockSpec((tm, d), lambda i: (i, 0)),
        out_shape=jax.ShapeDtypeStruct((t, d), F32),
        compiler_params=_cparams(("parallel",)),
        name="moe_combine",
    )(ysg, gates, pos, x, g3)


def _take_rows(a, idx):
    return a.at[idx].get(mode="promise_in_bounds")


def moe_route(x, nw, sh3, sc3, rwt, rb, su, sl, rows_per_group):
    t, d = x.shape
    tt = min(TOKEN_TILE, t)
    n_tiles, tl = t // tt, TOP_K * tt
    hs, gates, pos, cnt, run, tot = moe_router(x, nw, sh3, sc3, rwt, rb, su, sl, rows_per_group)
    tm = MOE_TILE
    n_blocks = (t * TOP_K + N_EXPERTS * (tm - 1) + tm - 1) // tm
    i32 = jnp.int32
    counts = tot[:, 0].astype(i32)
    padded = (counts + tm - 1) // tm * tm
    pend = jnp.cumsum(padded)
    pstart = (pend - padded).astype(i32)
    block_e = jnp.minimum(jnp.sum((pend[None, :] <= (jnp.arange(n_blocks) * tm)[:, None]).astype(i32), axis=1),
                          N_EXPERTS - 1)
    n_used = (pend[-1] // tm).astype(i32).reshape(1)
    cnt = cnt[:, :, 0].astype(i32)
    run = run[:, :, 0].astype(i32)
    eoff = jnp.cumsum(cnt, axis=1) - cnt
    tiles = jnp.arange(n_tiles, dtype=i32)
    in_e = (jnp.arange(n_blocks, dtype=i32) * tm - pstart[block_e])[:, None] + jnp.arange(tm, dtype=i32)[None, :]
    run_e = run.T[block_e]
    tile = jnp.sum((run_e[:, None, :] <= in_e[:, :, None]).astype(i32), axis=-1) - 1
    shift = jnp.sum(jnp.where(tiles[None, None, :] == tile[:, :, None], (eoff.T[block_e] - run_e)[:, None, :], 0),
                    axis=-1)
    spread = (jnp.arange(n_blocks * tm, dtype=i32) % (TOP_K * t)).reshape(n_blocks, tm)
    src = jnp.where(in_e < counts[block_e][:, None], tile * tl + shift + in_e, spread)
    j = jnp.arange(tl, dtype=i32)
    e_j = jnp.sum(((eoff + cnt)[:, None, :] <= j[None, :, None]).astype(i32), axis=-1)
    base = pstart[None, :] + run - eoff
    back = jnp.sum(jnp.where(jnp.arange(N_EXPERTS, dtype=i32)[None, None, :] == e_j[:, :, None], base[:, None, :], 0),
                   axis=-1) + j[None, :]
    return dict(xs=_take_rows(hs, src.reshape(-1)), block_e=block_e, n_used=n_used, back=back.reshape(-1),
                gates=gates, pos=pos)


def moe_experts(m, w1, b1, w2, b2, layer):
    ys = moe_ffn_blocks(m['xs'], m['block_e'], m['n_used'], w1, b1, w2, b2, layer)
    return _take_rows(ys, m['back'])


def _split3(x):
    hi = x.astype(BF16)
    r1 = x - hi.astype(F32)
    mid = r1.astype(BF16)
    lo = (r1 - mid.astype(F32)).astype(BF16)
    return hi, mid, lo


def _ssd_kernel(z_ref, xbc_ref, dt_ref, cw_ref, cb_ref, dtb_ref, alog_ref, dx_ref, nw_ref, e_ref, tri_ref, shift_ref,
                ic_ref, is_ref, g_ref, so_ref, tail_scr, s_scr, y_scr, *, valid_rows):
    q = SSD_Q
    c = pl.program_id(1)

    @pl.when(c == 0)
    def _():
        tail_scr[...] = ic_ref[0]
        s_scr[...] = is_ref[0]

    xb = xbc_ref[...]
    acc = cb_ref[...] + xb.astype(F32) * cw_ref[CONV_W - 1:CONV_W, :]
    for tap in range(CONV_W - 1):
        shifted = jnp.dot(shift_ref[tap], xb, preferred_element_type=F32)
        acc = acc + shifted * cw_ref[tap:tap + 1, :]
    t0, t1, t2 = (tail_scr[8 - (CONV_W - 1) + j:8 - (CONV_W - 1) + j + 1, :] for j in range(CONV_W - 1))
    w0, w1, w2 = (cw_ref[j:j + 1, :] for j in range(CONV_W - 1))
    row8 = lax.broadcasted_iota(jnp.int32, (8, CONV_DIM), 0)
    head = jnp.where(row8 == 0, t0 * w0 + t1 * w1 + t2 * w2,
                     jnp.where(row8 == 1, t1 * w0 + t2 * w1, jnp.where(row8 == 2, t2 * w0, 0.0)))
    tail_scr[...] = xbc_ref[q - 16:q, :].astype(F32)[8:16, :]
    xc = _silu(jnp.concatenate([acc[0:8, :] + head, acc[8:q, :]], axis=0))
    xs = xc[:, :D_INNER]

    dt = _softplus(dt_ref[...] + dtb_ref[...])
    if valid_rows < q:
        rows = lax.broadcasted_iota(jnp.int32, dt.shape, 0)
        dt = jnp.where(rows < valid_rows, dt, 0.0)
    a = dt * (-jnp.exp(alog_ref[...]))
    a_cs = jnp.dot(tri_ref[...], jnp.concatenate(_split3(a), axis=0), preferred_element_type=F32)
    dtx = jnp.dot(jnp.concatenate(_split3(dt), axis=1), e_ref[...], preferred_element_type=F32)
    acsx = jnp.dot(jnp.concatenate(_split3(a_cs), axis=1), e_ref[...], preferred_element_type=F32)

    row = lax.broadcasted_iota(jnp.int32, (q, D_INNER), 0)
    col = lax.broadcasted_iota(jnp.int32, (q, D_INNER), 1) & (HEAD_DIM_A - 1)
    acs_row = jnp.sum(jnp.where(row == col, acsx, 0.0), axis=0, keepdims=True)
    decay_in = jnp.where(row >= col, jnp.exp(jnp.minimum(acsx - acs_row, 0.0)), 0.0)
    last = acsx[q - 1:q, :]
    e_in = jnp.exp(acsx)
    e_last = jnp.exp(last)
    xd = xs * dtx
    xdd = xd * jnp.exp(last - acsx)
    lane = lax.broadcasted_iota(jnp.int32, (q, LANE), 1)
    lo = lane < HEAD_DIM_A

    for g in range(N_GROUPS):
        gsl = slice(g * GROUP_W, (g + 1) * GROUP_W)
        b_off = D_INNER + g * D_STATE
        c_off = D_INNER + N_GROUPS * D_STATE + g * D_STATE
        bg = xc[:, b_off:b_off + D_STATE]
        cgb = xc[:, c_off:c_off + D_STATE].astype(BF16)
        bgb = bg.astype(BF16)
        b2 = jnp.concatenate([bgb, bgb], axis=0)
        cb2 = lax.dot_general(cgb, b2, (((1,), (1,)), ((), ())), preferred_element_type=F32)
        s_g = s_scr[:, gsl]
        y_off = jnp.dot(cgb, s_g.astype(BF16), preferred_element_type=F32) * e_in[:, gsl]
        for pp in range(GROUP_W // LANE):
            sl = slice(g * GROUP_W + pp * LANE, g * GROUP_W + (pp + 1) * LANE)
            m = (decay_in[:, sl] * cb2).astype(BF16)
            xp = xd[:, sl]
            bd = jnp.concatenate([jnp.where(lo, xp, 0.0), jnp.where(lo, 0.0, xp)], axis=0).astype(BF16)
            y_diag = jnp.dot(m, bd, preferred_element_type=F32)
            y_scr[:, sl] = y_diag + y_off[:, pp * LANE:(pp + 1) * LANE] + xs[:, sl] * dx_ref[:, sl]
        s_new = jnp.dot(bg.T.astype(BF16), xdd[:, gsl].astype(BF16), preferred_element_type=F32)
        s_scr[:, gsl] = s_g * e_last[:, gsl] + s_new

    for g in range(N_GROUPS):
        gsl = slice(g * GROUP_W, (g + 1) * GROUP_W)
        gz = y_scr[:, gsl] * _silu(z_ref[:, gsl].astype(F32))
        ms = jnp.mean(gz * gz, axis=-1, keepdims=True)
        g_ref[:, gsl] = (gz * lax.rsqrt(ms + NORM_EPS) * nw_ref[:, gsl]).astype(g_ref.dtype)

    @pl.when(c == pl.num_programs(1) - 1)
    def _():
        so_ref[0] = s_scr[...]


def ssd_mixer(z, xbc, dt, cw, cb, dtb, alog, dx, nw, e_mat, tri, shift, init_conv, init_ssm, nb, nc, valid_rows):
    q = SSD_Q
    const = lambda a: pl.BlockSpec(a.shape, lambda b, c: (0,) * a.ndim)
    return pl.pallas_call(
        functools.partial(_ssd_kernel, valid_rows=valid_rows),
        grid=(nb, nc),
        in_specs=[pl.BlockSpec((q, D_INNER), lambda b, c: (b * nc + c, 0)),
                  pl.BlockSpec((q, CONV_DIM), lambda b, c: (b * nc + c, 0)),
                  pl.BlockSpec((q, LANE), lambda b, c: (b * nc + c, 0)),
                  const(cw), const(cb), const(dtb), const(alog), const(dx), const(nw), const(e_mat), const(tri),
                  const(shift),
                  pl.BlockSpec((1, 8, CONV_DIM), lambda b, c: (b, 0, 0)),
                  pl.BlockSpec((1, D_STATE, D_INNER), lambda b, c: (b, 0, 0))],
        out_specs=[pl.BlockSpec((q, D_INNER), lambda b, c: (b * nc + c, 0)),
                   pl.BlockSpec((1, D_STATE, D_INNER), lambda b, c: (b, 0, 0))],
        out_shape=[jax.ShapeDtypeStruct((nb * nc * q, D_INNER), BF16),
                   jax.ShapeDtypeStruct((nb, D_STATE, D_INNER), F32)],
        scratch_shapes=[pltpu.VMEM((8, CONV_DIM), F32),
                        pltpu.VMEM((D_STATE, D_INNER), F32),
                        pltpu.VMEM((q, D_INNER), F32)],
        compiler_params=_cparams(("parallel", "arbitrary")),
        name="ssd_mixer",
    )(z, xbc, dt, cw, cb, dtb, alog, dx, nw, e_mat, tri, shift, init_conv, init_ssm)


def _attn_kernel(q_ref, k_ref, v_ref, u_ref, o_ref, qs_scr, acc_scr, c_scr, y_scr, t_scr, tot_scr,
                 *, tq, tk, pos0, n_kblocks):
    qi = pl.program_id(2)
    qv = q_ref[0]
    lane_q = lax.broadcasted_iota(jnp.int32, qv.shape, 1)
    zero_q = jnp.zeros_like(qv)
    qs_scr[0:tq, :] = jnp.where(lane_q < HEAD_DIM_B, qv, zero_q)
    qs_scr[tq:2 * tq, :] = jnp.where(lane_q < HEAD_DIM_B, zero_q, qv)
    y_scale = LOG2_E / math.sqrt(HEAD_DIM_B)
    q_idx = lax.broadcasted_iota(jnp.int32, (2 * tq, tk), 0) & (tq - 1)
    pos_gap = pos0 + qi * tq + q_idx - lax.broadcasted_iota(jnp.int32, (2 * tq, tk), 1)
    n_vis = jnp.minimum((pos0 + (qi + 1) * tq - 2) // tk + 1, n_kblocks)
    n_full = jnp.minimum((pos0 + qi * tq) // tk, n_kblocks)
    n_masked = n_vis - n_full

    def kblock(i):
        return jnp.maximum(n_vis - 1 - i, 0)

    def rows_of(ref, i):
        return ref[0, pl.ds(pl.multiple_of(kblock(i) * tk, tk), tk), :]

    def score_block(i):
        y = lax.dot_general(qs_scr[...], rows_of(k_ref, i), (((1,), (1,)), ((), ())), preferred_element_type=F32)
        return y * y_scale

    def suffix_block(y, i, masked):
        sp = jnp.maximum(y, jnp.log(1.0 + jnp.exp2(jnp.minimum(y, 126.0))) * LOG2_E)
        if masked:
            vis = pos_gap > kblock(i) * tk
            sp = jnp.where(vis, sp, 0.0)
        cs = jnp.dot(sp.astype(BF16), u_ref[...], preferred_element_type=F32)
        t = jnp.minimum(y - cs, 0.0)
        if masked:
            t = jnp.where(vis, t, HIDDEN)
        return t, cs[:, 0:1]

    def weigh_block(t, c, i):
        return jnp.dot(jnp.exp2(t - c).astype(BF16), rows_of(v_ref, i), preferred_element_type=F32)

    @pl.when(n_vis >= 2)
    def _():
        t0, tot0 = suffix_block(score_block(0), 0, True)
        t1, tot1 = suffix_block(score_block(1), 1, True)
        acc_scr[...] = weigh_block(t0, 0.0, 0) + weigh_block(t1, tot0, 1)
        c_scr[...] = tot0 + tot1

    @pl.when(n_vis < 2)
    def _():
        t0, tot0 = suffix_block(score_block(0), 0, True)
        acc_scr[...] = weigh_block(t0, 0.0, 0)
        c_scr[...] = tot0

    def scores(i, dst):
        y_scr[dst] = score_block(i)

    def suffix(i, masked, src, dst):
        t_scr[dst], tot_scr[dst] = suffix_block(y_scr[src], i, masked)

    def weigh(i, src):
        c = c_scr[...]
        acc_scr[...] += weigh_block(t_scr[src], c, i)
        c_scr[...] = c + tot_scr[src]

    def trip(i, masked):
        weigh(i, 0)
        suffix(i + 1, masked, 0, 0)
        scores(i + 2, 0)

    def group(size):
        def body(i):
            for j in range(size):
                cur, nxt = j & 1, (j + 1) & 1
                scores(i + j + 2, nxt)
                suffix(i + j + 1, False, cur, nxt)
                weigh(i + j, cur)
        return body

    def run(start, stop, step, body):
        def cond(i):
            return (i < stop) & (jnp.min(c_scr[...]) < SKIP_BITS)

        def step_body(i):
            body(i)
            return i + step

        lax.while_loop(cond, step_body, start)

    @pl.when((n_vis > 2) & (jnp.min(c_scr[...]) < SKIP_BITS))
    def _():
        scores(2, 0)
        suffix(2, True, 0, 0)
        scores(3, 0)
        n_mt = jnp.maximum(n_masked - 1, 2)
        first_pair = n_mt + ((n_vis - n_mt) & 1)
        first_group = first_pair + ((n_vis - first_pair) % ATTN_UNROLL)
        run(2, n_mt, 1, lambda i: trip(i, True))
        run(n_mt, first_pair, 1, lambda i: trip(i, False))
        run(first_pair, first_group, 2, group(2))
        run(first_group, n_vis, ATTN_UNROLL, group(ATTN_UNROLL))

    lane_o = lax.broadcasted_iota(jnp.int32, (tq, LANE), 1)
    o_ref[0] = jnp.where(lane_o < HEAD_DIM_B, acc_scr[0:tq, :], acc_scr[tq:2 * tq, :]).astype(o_ref.dtype)


def stick_breaking(q, k, v, u, pos0, tq):
    b, l, hd = q.shape
    kp = k.shape[1]
    tk = ATTN_TK
    assert tq & (tq - 1) == 0 and l % tq == 0 and kp % tk == 0
    return pl.pallas_call(
        functools.partial(_attn_kernel, tq=tq, tk=tk, pos0=pos0, n_kblocks=kp // tk),
        grid=(b, hd // LANE, l // tq),
        in_specs=[pl.BlockSpec((1, tq, LANE), lambda bi, p, i: (bi, i, p)),
                  pl.BlockSpec((1, kp, LANE), lambda bi, p, i: (bi, 0, p)),
                  pl.BlockSpec((1, kp, LANE), lambda bi, p, i: (bi, 0, p)),
                  pl.BlockSpec(u.shape, lambda bi, p, i: (0, 0))],
        out_specs=pl.BlockSpec((1, tq, LANE), lambda bi, p, i: (bi, i, p)),
        out_shape=jax.ShapeDtypeStruct((b, l, hd), BF16),
        scratch_shapes=[pltpu.VMEM((2 * tq, LANE), BF16), pltpu.VMEM((2 * tq, LANE), F32),
                        pltpu.VMEM((2 * tq, 1), F32), pltpu.VMEM((2, 2 * tq, tk), F32),
                        pltpu.VMEM((2, 2 * tq, tk), F32), pltpu.VMEM((2, 2 * tq, 1), F32)],
        compiler_params=_cparams(("parallel", "parallel", "arbitrary")),
        name="stick_breaking",
    )(q, k, v, u)


class _Path:
    def __init__(self, x3, mods, kv_mod, conv_state, ssm_state, past_k, past_v, p):
        self.b, self.l, self.d = x3.shape
        self.t = self.b * self.l
        self.x = x3.reshape(self.t, self.d)
        self.mods, self.kv_mod, self.p = mods, kv_mod, p
        self.conv_state, self.ssm_state, self.past_k, self.past_v = conv_state, ssm_state, past_k, past_v
        self.per_token = self.l < TOKEN_TILE
        self.rows_per_group = min(TOKEN_TILE, self.t) if self.per_token else self.l
        self.pos0 = past_k.shape[1]
        self.conv_out, self.ssm_out = [], []

    def mod3(self, v):
        if self.per_token:
            tile = min(TOKEN_TILE, self.t)
            return jnp.repeat(v, self.l, axis=0).reshape(self.t // tile, tile, self.d)
        return v[:, None, :]

    def mix_and_route(self, layer):
        p, b, l, d, t, rpg, mod3 = self.p, self.b, self.l, self.d, self.t, self.rows_per_group, self.mod3
        sh1, sc1, g1, sh2, sc2, self.g2 = jnp.split(self.mods[layer], 6, axis=-1)
        nw1 = p['mix_norm_w'][layer][None, :]
        x = self.x
        if layer < N_A_LAYERS:
            i = layer
            z, xbc, dt = norm_mod_matmul(x, nw1, mod3(sh1), mod3(sc1), [p['w_z'][i], p['w_xbc'][i], p['w_dt'][i]],
                                         [BF16, BF16, F32], rpg)
            xbc3 = xbc.reshape(b, l, CONV_DIM)
            self.conv_out.append(xbc3[:, l - (CONV_W - 1):].astype(F32))
            nc = max(l // SSD_Q, 1)
            if l < SSD_Q:
                pad = lambda a: jnp.pad(a.reshape(b, l, -1), ((0, 0), (0, SSD_Q - l), (0, 0))).reshape(b * SSD_Q, -1)
                z, xbc, dt = pad(z), pad(xbc), pad(dt)
            ic = jnp.pad(self.conv_state[i].astype(F32), ((0, 0), (8 - (CONV_W - 1), 0), (0, 0)))
            iss = self.ssm_state[i].astype(F32).transpose(0, 3, 1, 2).reshape(b, D_STATE, D_INNER)
            gm, s_fin = ssd_mixer(z, xbc, dt, p['a_conv_w'][i], p['a_conv_b'][i][None, :], p['dt_bias'][i],
                                  p['a_log'][i], p['d_x'][i], p['a_norm_w'][i][None, :], p['e_mat'], p['tri'], p['shift'],
                                  ic, iss, b, nc, min(l, SSD_Q))
            if l < SSD_Q:
                gm = gm.reshape(b, SSD_Q, D_INNER)[:, :l].reshape(t, D_INNER)
            self.ssm_out.append(s_fin.reshape(b, D_STATE, N_HEADS_A, HEAD_DIM_A).transpose(0, 2, 3, 1))
            x = matmul_residual(gm, p['a_out_proj'][i], x, mod3(g1), rpg)
        else:
            j = layer - N_A_LAYERS
            (qp,) = norm_mod_matmul(x, nw1, mod3(sh1), mod3(sc1), [p['b_w_q'][j]], [BF16], rpg)
            o = stick_breaking(qp.reshape(b, l, d), self.kb, self.vb, p['u_tri'], self.pos0, min(l, 256))
            x = matmul_residual(o.reshape(t, d), p['b_w_o'][j], x, mod3(g1), rpg)
        self.x = x
        self.moe = moe_route(x, p['ffn_norm_w'][layer][None, :], mod3(sh2), mod3(sc2), p['router_wt'][layer],
                             p['router_b'][layer], p['su_tri'], p['sl_tri'], rpg)

    def experts(self, layer):
        p = self.p
        self.ysg = moe_experts(self.moe, p['moe_w1'], p['moe_b1'], p['moe_w2'], p['moe_b2'], layer)

    def combine(self, layer):
        p, b, l, d, rpg, mod3 = self.p, self.b, self.l, self.d, self.rows_per_group, self.mod3
        self.x = moe_combine(self.ysg, self.moe['gates'], self.moe['pos'], self.x, mod3(self.g2), rpg)
        if layer == N_A_LAYERS - 1:
            sh_kv, sc_kv = jnp.split(self.kv_mod, 2, axis=-1)
            self.k_new, self.v_new = norm_mod_matmul(self.x, p['kv_norm_w'][None, :], mod3(sh_kv), mod3(sc_kv),
                                                     [p['w_k'], p['w_v']], [F32, F32], rpg)
            k_all = jnp.concatenate([self.past_k.reshape(b, self.pos0, d), self.k_new.reshape(b, l, d)], axis=1)
            v_all = jnp.concatenate([self.past_v.reshape(b, self.pos0, d), self.v_new.reshape(b, l, d)], axis=1)
            kpad = (-k_all.shape[1]) % ATTN_TK
            self.kb = jnp.pad(k_all, ((0, 0), (0, kpad), (0, 0))).astype(BF16)
            self.vb = jnp.pad(v_all, ((0, 0), (0, kpad), (0, 0))).astype(BF16)

    def finish(self):
        b, l, d = self.b, self.l, self.d
        y = final_rmsnorm(self.x, self.p['final_norm_w'][None, :])
        return (y.reshape(b, l, d), jnp.stack(self.conv_out), jnp.stack(self.ssm_out),
                self.k_new.reshape(b, l, N_HEADS_B, HEAD_DIM_B), self.v_new.reshape(b, l, N_HEADS_B, HEAD_DIM_B))


def _run_paths(paths):
    for layer in range(DEPTH):
        for step in (_Path.mix_and_route, _Path.experts, _Path.combine):
            for path in paths:
                step(path, layer)
    return [path.finish() for path in paths]


def kernel(x_prompt, x_sample, state_conv, state_ssm, cache_k, cache_v, c_prompt, c_sample, mod_w, mod_b,
           mix_norm_w, ffn_norm_w, a_in_proj, a_conv_w, a_conv_b, a_dt_bias, a_A_log, a_D, a_norm_w, a_out_proj,
           kv_mod_w, kv_mod_b, kv_norm_w, w_kv, b_w_q, b_w_o, router_w, router_b, moe_w1, moe_b1, moe_w2, moe_b2,
           final_norm_w):
    bp = x_prompt.shape[0]
    hb = N_HEADS_B * HEAD_DIM_B
    pad_heads = lambda a: jnp.pad(a, ((0, 0), (0, LANE - N_HEADS_A)))[:, None, :]
    head_of_lane = jnp.arange(D_INNER) // HEAD_DIM_A
    p = dict(
        mix_norm_w=mix_norm_w, ffn_norm_w=ffn_norm_w, kv_norm_w=kv_norm_w, final_norm_w=final_norm_w,
        w_z=a_in_proj[:, :, :D_INNER].astype(BF16),
        w_xbc=a_in_proj[:, :, D_INNER:D_INNER + CONV_DIM].astype(BF16),
        w_dt=jnp.pad(a_in_proj[:, :, D_INNER + CONV_DIM:], ((0, 0), (0, 0), (0, LANE - N_HEADS_A))).astype(BF16),
        a_conv_w=a_conv_w, a_conv_b=a_conv_b, dt_bias=pad_heads(a_dt_bias), a_log=pad_heads(a_A_log),
        d_x=a_D[:, head_of_lane][:, None, :], a_norm_w=a_norm_w, a_out_proj=a_out_proj.astype(BF16),
        e_mat=jnp.tile((jnp.arange(LANE)[:, None] == head_of_lane[None, :]).astype(BF16), (3, 1)),
        tri=jnp.tile((jnp.arange(SSD_Q)[:, None] >= jnp.arange(SSD_Q)[None, :]).astype(BF16), (1, 3)),
        shift=jnp.stack([(jnp.arange(SSD_Q)[:, None] + tap - (CONV_W - 1) == jnp.arange(SSD_Q)[None, :])
                         for tap in range(CONV_W - 1)]).astype(BF16),
        u_tri=(jnp.arange(ATTN_TK)[:, None] >= jnp.arange(ATTN_TK)[None, :]).astype(BF16),
        w_k=w_kv[:, :hb].astype(BF16), w_v=w_kv[:, hb:].astype(BF16),
        b_w_q=b_w_q.astype(BF16), b_w_o=b_w_o.astype(BF16),
        router_wt=router_w.transpose(0, 2, 1), router_b=router_b[:, :, None],
        su_tri=(jnp.arange(TOKEN_TILE)[:, None] < jnp.arange(TOKEN_TILE)[None, :]).astype(BF16),
        sl_tri=(jnp.arange(N_EXPERTS)[:, None] > jnp.arange(N_EXPERTS)[None, :]).astype(BF16),
        moe_w1=moe_w1, moe_b1=moe_b1[:, :, None, :], moe_w2=moe_w2, moe_b2=moe_b2[:, :, None, :],
    )
    c_all = jnp.concatenate([c_prompt, c_sample], axis=0)
    mods = mod_vectors(c_all, mod_w, mod_b[:, None, :])
    kv_mod = mod_vectors(c_all, kv_mod_w[None], kv_mod_b[None, None, :])[0]

    bg = bp // PROMPT_GROUPS
    conv0 = jnp.zeros((N_A_LAYERS, bg, CONV_W - 1, CONV_DIM), F32)
    ssm0 = jnp.zeros((N_A_LAYERS, bg, N_HEADS_A, HEAD_DIM_A, D_STATE), F32)
    kv0 = jnp.zeros((bg, 0, N_HEADS_B, HEAD_DIM_B), F32)
    paths = [_Path(x_prompt[g * bg:(g + 1) * bg], mods[:, g * bg:(g + 1) * bg], kv_mod[g * bg:(g + 1) * bg],
                   conv0, ssm0, kv0, kv0, p) for g in range(PROMPT_GROUPS)]
    paths.append(_Path(x_sample, mods[:, bp:], kv_mod[bp:], state_conv, state_ssm, cache_k, cache_v, p))
    *groups, out_s = _run_paths(paths)
    out_p = tuple(jnp.concatenate([grp[i] for grp in groups], axis=ax) for i, ax in enumerate((0, 1, 1, 0, 0)))
    return (out_p[0], out_s[0]) + out_p[1:] + out_s[1:]
```

```python
import functools
import math

import jax
import jax.numpy as jnp
from jax import lax
from jax.experimental import pallas as pl
from jax.experimental.pallas import tpu as pltpu

F32 = jnp.float32
BF16 = jnp.bfloat16
HIGHEST = lax.Precision.HIGHEST

D_MODEL = 1024
DEPTH = 4
N_A_LAYERS = 2
D_INNER = 2048
HEAD_DIM_A = 64
N_HEADS_A = 32
D_STATE = 128
N_GROUPS = 4
GROUP_W = D_INNER // N_GROUPS
CONV_W = 4
CONV_DIM = D_INNER + 2 * N_GROUPS * D_STATE
HEAD_DIM_B = 64
N_HEADS_B = 16
N_EXPERTS = 32
TOP_K = 4
D_FF = 1024
SWIGLU_ALPHA = 1.702
SWIGLU_LIMIT = 7.0
NORM_EPS = 1e-6
LOG2_E = 1.4426950408889634
HIDDEN = -1e30
SKIP_BITS = 160.0

LANE = 128
SSD_Q = 64
TOKEN_TILE = 512
MOE_TILE = 512
ATTN_TK = 256
ATTN_UNROLL = 4
PROMPT_GROUPS = 1
VMEM_LIMIT = 56 * 1024 * 1024


def _cparams(sem):
    return pltpu.CompilerParams(dimension_semantics=sem, vmem_limit_bytes=VMEM_LIMIT)


def _softplus(x):
    return jnp.maximum(x, 0.0) + jnp.log1p(jnp.exp(-jnp.abs(x)))


def _silu(x):
    return x * jax.nn.sigmoid(x)


def _norm_mod(x, nw, sh, sc):
    ms = jnp.mean(x * x, axis=-1, keepdims=True)
    return (x * lax.rsqrt(ms + NORM_EPS) * nw) * (1.0 + sc) + sh


def _mod_spec(mod3, tile, rows_per_group):
    steps = max(rows_per_group // tile, 1)
    g = mod3.shape[1]
    return pl.BlockSpec((1, g, mod3.shape[2]), lambda i: (i // steps, 0, 0))


def _mod_kernel(c_ref, w_ref, b_ref, o_ref):
    cs = _silu(c_ref[...])
    o_ref[0] = jnp.dot(cs, w_ref[0], precision=HIGHEST, preferred_element_type=F32) + b_ref[0]


def mod_vectors(c, w, b):
    nl, d, n = w.shape
    tn = 1024
    return pl.pallas_call(
        _mod_kernel,
        grid=(nl, n // tn),
        in_specs=[pl.BlockSpec(c.shape, lambda l, j: (0, 0)),
                  pl.BlockSpec((1, d, tn), lambda l, j: (l, 0, j)),
                  pl.BlockSpec((1, 1, tn), lambda l, j: (l, 0, j))],
        out_specs=pl.BlockSpec((1, c.shape[0], tn), lambda l, j: (l, 0, j)),
        out_shape=jax.ShapeDtypeStruct((nl, c.shape[0], n), F32),
        compiler_params=_cparams(("arbitrary", "arbitrary")),
        name="mod_vectors",
    )(c, w, b)


def _nmm_kernel(x_ref, nw_ref, sh_ref, sc_ref, *refs, n_w):
    h = _norm_mod(x_ref[...], nw_ref[...], sh_ref[0], sc_ref[0]).astype(BF16)
    for w_ref, o_ref in zip(refs[:n_w], refs[n_w:]):
        o_ref[...] = jnp.dot(h, w_ref[...], preferred_element_type=F32).astype(o_ref.dtype)


def norm_mod_matmul(x, nw, sh3, sc3, ws, out_dtypes, rows_per_group):
    t, d = x.shape
    tm = min(TOKEN_TILE, t)
    in_specs = [pl.BlockSpec((tm, d), lambda i: (i, 0)),
                pl.BlockSpec((1, d), lambda i: (0, 0)),
                _mod_spec(sh3, tm, rows_per_group), _mod_spec(sc3, tm, rows_per_group)]
    in_specs += [pl.BlockSpec(w.shape, lambda i: (0, 0), pipeline_mode=pl.Buffered(1)) for w in ws]
    return pl.pallas_call(
        functools.partial(_nmm_kernel, n_w=len(ws)),
        grid=(t // tm,),
        in_specs=in_specs,
        out_specs=[pl.BlockSpec((tm, w.shape[1]), lambda i: (i, 0)) for w in ws],
        out_shape=[jax.ShapeDtypeStruct((t, w.shape[1]), dt) for w, dt in zip(ws, out_dtypes)],
        compiler_params=_cparams(("parallel",)),
        name="norm_mod_matmul",
    )(x, nw, sh3, sc3, *ws)


def _mmres_kernel(a_ref, w_ref, x_ref, g_ref, o_ref):
    acc = jnp.dot(a_ref[...], w_ref[...], preferred_element_type=F32)
    o_ref[...] = x_ref[...] + g_ref[0] * acc


def matmul_residual(a, w, x, g3, rows_per_group):
    t, d = x.shape
    tm = min(TOKEN_TILE, t)
    return pl.pallas_call(
        _mmres_kernel,
        grid=(t // tm,),
        in_specs=[pl.BlockSpec((tm, a.shape[1]), lambda i: (i, 0)),
                  pl.BlockSpec(w.shape, lambda i: (0, 0), pipeline_mode=pl.Buffered(1)),
                  pl.BlockSpec((tm, d), lambda i: (i, 0)),
                  _mod_spec(g3, tm, rows_per_group)],
        out_specs=pl.BlockSpec((tm, d), lambda i: (i, 0)),
        out_shape=jax.ShapeDtypeStruct((t, d), F32),
        compiler_params=_cparams(("parallel",)),
        name="matmul_residual",
    )(a, w, x, g3)


def _rms_kernel(x_ref, nw_ref, o_ref):
    x = x_ref[...]
    ms = jnp.mean(x * x, axis=-1, keepdims=True)
    o_ref[...] = x * lax.rsqrt(ms + NORM_EPS) * nw_ref[...]


def final_rmsnorm(x, nw):
    t, d = x.shape
    tm = min(TOKEN_TILE, t)
    return pl.pallas_call(
        _rms_kernel,
        grid=(t // tm,),
        in_specs=[pl.BlockSpec((tm, d), lambda i: (i, 0)), pl.BlockSpec((1, d), lambda i: (0, 0))],
        out_specs=pl.BlockSpec((tm, d), lambda i: (i, 0)),
        out_shape=jax.ShapeDtypeStruct((t, d), F32),
        compiler_params=_cparams(("parallel",)),
        name="final_rmsnorm",
    )(x, nw)


def _router_kernel(x_ref, nw_ref, sh_ref, sc_ref, rwt_ref, rb_ref, su_ref, sl_ref, hs_ref, g_ref, p_ref, cnt_ref,
                   run_ref, tot_ref, run_scr):
    @pl.when(pl.program_id(0) == 0)
    def _():
        run_scr[...] = jnp.zeros_like(run_scr)

    h = _norm_mod(x_ref[...], nw_ref[...], sh_ref[0], sc_ref[0])
    logits = lax.dot_general(rwt_ref[...], h, (((1,), (1,)), ((), ())),
                             precision=HIGHEST, preferred_element_type=F32) + rb_ref[...]
    eid = lax.broadcasted_iota(jnp.int32, logits.shape, 0)
    vals, idxs, hits = [], [], []
    for _ in range(TOP_K):
        m = jnp.max(logits, axis=0, keepdims=True)
        idx = jnp.min(jnp.where(logits == m, eid, N_EXPERTS), axis=0, keepdims=True)
        hit = eid == idx
        vals.append(m)
        idxs.append(idx)
        hits.append(hit)
        logits = jnp.where(hit, -jnp.inf, logits)
    ex = [jnp.exp(v - vals[0]) for v in vals]
    den = ex[0] + ex[1] + ex[2] + ex[3]
    cnt = (hits[0] | hits[1] | hits[2] | hits[3]).astype(F32)
    cnt_b = cnt.astype(BF16)
    earlier = jnp.dot(cnt_b, su_ref[...], preferred_element_type=F32)
    lower = jnp.dot(sl_ref[...], cnt_b, preferred_element_type=F32)
    cnt_tile = jnp.sum(cnt, axis=1, keepdims=True)
    where_to = earlier + jnp.sum(lower, axis=1, keepdims=True)
    tm = x_ref.shape[0]
    slot = lax.broadcasted_iota(jnp.int32, (TOP_K * tm, tm), 0)
    place = None
    for k in range(TOP_K):
        pos = jnp.sum(jnp.where(hits[k], where_to, 0.0), axis=0, keepdims=True).astype(jnp.int32)
        g_ref[k:k + 1, :] = ex[k] / den
        p_ref[k:k + 1, :] = pos
        place = (slot == pos) if place is None else place | (slot == pos)
    hs_ref[...] = jnp.dot(place.astype(BF16), h.astype(BF16), preferred_element_type=F32).astype(BF16)
    cnt_ref[0] = cnt_tile
    run_ref[0] = run_scr[...]
    run_scr[...] += cnt_tile
    tot_ref[...] = run_scr[...]


def moe_router(x, nw, sh3, sc3, rwt, rb, su, sl, rows_per_group):
    t, d = x.shape
    tm = min(TOKEN_TILE, t)
    n_tiles = t // tm
    return pl.pallas_call(
        _router_kernel,
        grid=(n_tiles,),
        in_specs=[pl.BlockSpec((tm, d), lambda i: (i, 0)),
                  pl.BlockSpec((1, d), lambda i: (0, 0)),
                  _mod_spec(sh3, tm, rows_per_group), _mod_spec(sc3, tm, rows_per_group),
                  pl.BlockSpec(rwt.shape, lambda i: (0, 0)),
                  pl.BlockSpec(rb.shape, lambda i: (0, 0)),
                  pl.BlockSpec(su.shape, lambda i: (0, 0)),
                  pl.BlockSpec(sl.shape, lambda i: (0, 0))],
        out_specs=[pl.BlockSpec((TOP_K * tm, d), lambda i: (i, 0)),
                   pl.BlockSpec((TOP_K, tm), lambda i: (0, i)),
                   pl.BlockSpec((TOP_K, tm), lambda i: (0, i)),
                   pl.BlockSpec((1, N_EXPERTS, 1), lambda i: (i, 0, 0)),
                   pl.BlockSpec((1, N_EXPERTS, 1), lambda i: (i, 0, 0)),
                   pl.BlockSpec((N_EXPERTS, 1), lambda i: (0, 0))],
        out_shape=[jax.ShapeDtypeStruct((TOP_K * t, d), BF16),
                   jax.ShapeDtypeStruct((TOP_K, t), F32),
                   jax.ShapeDtypeStruct((TOP_K, t), jnp.int32),
                   jax.ShapeDtypeStruct((n_tiles, N_EXPERTS, 1), F32),
                   jax.ShapeDtypeStruct((n_tiles, N_EXPERTS, 1), F32),
                   jax.ShapeDtypeStruct((N_EXPERTS, 1), F32)],
        scratch_shapes=[pltpu.VMEM((N_EXPERTS, 1), F32)],
        compiler_params=_cparams(("arbitrary",)),
        name="moe_router",
    )(x, nw, sh3, sc3, rwt, rb, su, sl)


def _ffn_kernel(be_ref, nu_ref, xs_ref, w1_ref, b1_ref, w2_ref, b2_ref, o_ref, w1_scr, w2_scr):
    i = pl.program_id(0)

    @pl.when(i < nu_ref[0])
    def _():
        @pl.when((i == 0) | (be_ref[i] != be_ref[jnp.maximum(i - 1, 0)]))
        def _():
            w1_scr[...] = w1_ref[0, 0].astype(BF16)
            w2_scr[...] = w2_ref[0, 0].astype(BF16)

        hid = jnp.dot(xs_ref[...], w1_scr[...], preferred_element_type=F32) + b1_ref[0, 0]
        glu = jnp.minimum(hid[:, :D_FF], SWIGLU_LIMIT)
        lin = jnp.clip(hid[:, D_FF:], -SWIGLU_LIMIT, SWIGLU_LIMIT)
        act = glu * jax.nn.sigmoid(SWIGLU_ALPHA * glu) * (lin + 1.0)
        out = jnp.dot(act.astype(BF16), w2_scr[...], preferred_element_type=F32) + b2_ref[0, 0]
        o_ref[...] = out.astype(o_ref.dtype)


def moe_ffn_blocks(xs, block_e, n_used, w1, b1, w2, b2, layer):
    n_rows, d = xs.shape
    tm = MOE_TILE
    grid_spec = pltpu.PrefetchScalarGridSpec(
        num_scalar_prefetch=2,
        grid=(n_rows // tm,),
        in_specs=[pl.BlockSpec((tm, d), lambda i, be, nu: (i, 0)),
                  pl.BlockSpec((1, 1, d, 2 * D_FF), lambda i, be, nu: (layer, be[i], 0, 0)),
                  pl.BlockSpec((1, 1, 1, 2 * D_FF), lambda i, be, nu: (layer, be[i], 0, 0)),
                  pl.BlockSpec((1, 1, D_FF, d), lambda i, be, nu: (layer, be[i], 0, 0)),
                  pl.BlockSpec((1, 1, 1, d), lambda i, be, nu: (layer, be[i], 0, 0))],
        out_specs=pl.BlockSpec((tm, d), lambda i, be, nu: (i, 0)),
        scratch_shapes=[pltpu.VMEM((d, 2 * D_FF), BF16), pltpu.VMEM((D_FF, d), BF16)],
    )
    return pl.pallas_call(
        _ffn_kernel,
        grid_spec=grid_spec,
        out_shape=jax.ShapeDtypeStruct((n_rows, d), BF16),
        compiler_params=_cparams(("arbitrary",)),
        name="moe_ffn",
    )(block_e, n_used, xs, w1, b1, w2, b2)


def _combine_kernel(y_ref, gt_ref, p_ref, x_ref, g_ref, o_ref):
    tm = x_ref.shape[0]
    eye = lax.broadcasted_iota(jnp.int32, (tm, tm), 0) == lax.broadcasted_iota(jnp.int32, (tm, tm), 1)
    col = lax.broadcasted_iota(jnp.int32, (tm, TOP_K * tm), 1)
    pick = jnp.zeros((tm, TOP_K * tm), F32)
    for k in range(TOP_K):
        g_col = jnp.sum(jnp.where(eye, gt_ref[k:k + 1, :], 0.0), axis=1, keepdims=True)
        p_col = jnp.sum(jnp.where(eye, p_ref[k:k + 1, :], 0), axis=1, keepdims=True)
        pick = jnp.where(col == p_col, g_col, pick)
    ff = jnp.dot(pick.astype(BF16), y_ref[...], preferred_element_type=F32)
    o_ref[...] = x_ref[...] + g_ref[0] * ff


def moe_combine(ysg, gates, pos, x, g3, rows_per_group):
    t, d = x.shape
    tm = min(TOKEN_TILE, t)
    return pl.pallas_call(
        _combine_kernel,
        grid=(t // tm,),
        in_specs=[pl.BlockSpec((TOP_K * tm, d), lambda i: (i, 0)),
                  pl.BlockSpec((TOP_K, tm), lambda i: (0, i)),
                  pl.BlockSpec((TOP_K, tm), lambda i: (0, i)),
                  pl.BlockSpec((tm, d), lambda i: (i, 0)),
                  _mod_spec(g3, tm, rows_per_group)],
        out_specs=pl.BlockSpec((tm, d), lambda i: (i, 0)),
        out_shape=jax.ShapeDtypeStruct((t, d), F32),
        compiler_params=_cparams(("parallel",)),
        name="moe_combine",
    )(ysg, gates, pos, x, g3)


def _take_rows(a, idx):
    return a.at[idx].get(mode="promise_in_bounds")


def moe_route(x, nw, sh3, sc3, rwt, rb, su, sl, rows_per_group):
    t, d = x.shape
    tt = min(TOKEN_TILE, t)
    n_tiles, tl = t // tt, TOP_K * tt
    hs, gates, pos, cnt, run, tot = moe_router(x, nw, sh3, sc3, rwt, rb, su, sl, rows_per_group)
    tm = MOE_TILE
    n_blocks = (t * TOP_K + N_EXPERTS * (tm - 1) + tm - 1) // tm
    i32 = jnp.int32
    counts = tot[:, 0].astype(i32)
    padded = (counts + tm - 1) // tm * tm
    pend = jnp.cumsum(padded)
    pstart = (pend - padded).astype(i32)
    block_e = jnp.minimum(jnp.sum((pend[None, :] <= (jnp.arange(n_blocks) * tm)[:, None]).astype(i32), axis=1),
                          N_EXPERTS - 1)
    n_used = (pend[-1] // tm).astype(i32).reshape(1)
    cnt = cnt[:, :, 0].astype(i32)
    run = run[:, :, 0].astype(i32)
    eoff = jnp.cumsum(cnt, axis=1) - cnt
    tiles = jnp.arange(n_tiles, dtype=i32)
    in_e = (jnp.arange(n_blocks, dtype=i32) * tm - pstart[block_e])[:, None] + jnp.arange(tm, dtype=i32)[None, :]
    run_e = run.T[block_e]
    tile = jnp.sum((run_e[:, None, :] <= in_e[:, :, None]).astype(i32), axis=-1) - 1
    shift = jnp.sum(jnp.where(tiles[None, None, :] == tile[:, :, None], (eoff.T[block_e] - run_e)[:, None, :], 0),
                    axis=-1)
    spread = (jnp.arange(n_blocks * tm, dtype=i32) % (TOP_K * t)).reshape(n_blocks, tm)
    src = jnp.where(in_e < counts[block_e][:, None], tile * tl + shift + in_e, spread)
    j = jnp.arange(tl, dtype=i32)
    e_j = jnp.sum(((eoff + cnt)[:, None, :] <= j[None, :, None]).astype(i32), axis=-1)
    base = pstart[None, :] + run - eoff
    back = jnp.sum(jnp.where(jnp.arange(N_EXPERTS, dtype=i32)[None, None, :] == e_j[:, :, None], base[:, None, :], 0),
                   axis=-1) + j[None, :]
    return dict(xs=_take_rows(hs, src.reshape(-1)), block_e=block_e, n_used=n_used, back=back.reshape(-1),
                gates=gates, pos=pos)


def moe_experts(m, w1, b1, w2, b2, layer):
    ys = moe_ffn_blocks(m['xs'], m['block_e'], m['n_used'], w1, b1, w2, b2, layer)
    return _take_rows(ys, m['back'])


def _split3(x):
    hi = x.astype(BF16)
    r1 = x - hi.astype(F32)
    mid = r1.astype(BF16)
    lo = (r1 - mid.astype(F32)).astype(BF16)
    return hi, mid, lo


def _ssd_kernel(z_ref, xbc_ref, dt_ref, cw_ref, cb_ref, dtb_ref, alog_ref, dx_ref, nw_ref, e_ref, tri_ref, shift_ref,
                ic_ref, is_ref, g_ref, so_ref, tail_scr, s_scr, y_scr, *, valid_rows):
    q = SSD_Q
    c = pl.program_id(1)

    @pl.when(c == 0)
    def _():
        tail_scr[...] = ic_ref[0]
        s_scr[...] = is_ref[0]

    xb = xbc_ref[...]
    acc = cb_ref[...] + xb.astype(F32) * cw_ref[CONV_W - 1:CONV_W, :]
    for tap in range(CONV_W - 1):
        shifted = jnp.dot(shift_ref[tap], xb, preferred_element_type=F32)
        acc = acc + shifted * cw_ref[tap:tap + 1, :]
    t0, t1, t2 = (tail_scr[8 - (CONV_W - 1) + j:8 - (CONV_W - 1) + j + 1, :] for j in range(CONV_W - 1))
    w0, w1, w2 = (cw_ref[j:j + 1, :] for j in range(CONV_W - 1))
    row8 = lax.broadcasted_iota(jnp.int32, (8, CONV_DIM), 0)
    head = jnp.where(row8 == 0, t0 * w0 + t1 * w1 + t2 * w2,
                     jnp.where(row8 == 1, t1 * w0 + t2 * w1, jnp.where(row8 == 2, t2 * w0, 0.0)))
    tail_scr[...] = xbc_ref[q - 16:q, :].astype(F32)[8:16, :]
    xc = _silu(jnp.concatenate([acc[0:8, :] + head, acc[8:q, :]], axis=0))
    xs = xc[:, :D_INNER]

    dt = _softplus(dt_ref[...] + dtb_ref[...])
    if valid_rows < q:
        rows = lax.broadcasted_iota(jnp.int32, dt.shape, 0)
        dt = jnp.where(rows < valid_rows, dt, 0.0)
    a = dt * (-jnp.exp(alog_ref[...]))
    a_cs = jnp.dot(tri_ref[...], jnp.concatenate(_split3(a), axis=0), preferred_element_type=F32)
    dtx = jnp.dot(jnp.concatenate(_split3(dt), axis=1), e_ref[...], preferred_element_type=F32)
    acsx = jnp.dot(jnp.concatenate(_split3(a_cs), axis=1), e_ref[...], preferred_element_type=F32)

    row = lax.broadcasted_iota(jnp.int32, (q, D_INNER), 0)
    col = lax.broadcasted_iota(jnp.int32, (q, D_INNER), 1) & (HEAD_DIM_A - 1)
    acs_row = jnp.sum(jnp.where(row == col, acsx, 0.0), axis=0, keepdims=True)
    decay_in = jnp.where(row >= col, jnp.exp(jnp.minimum(acsx - acs_row, 0.0)), 0.0)
    last = acsx[q - 1:q, :]
    e_in = jnp.exp(acsx)
    e_last = jnp.exp(last)
    xd = xs * dtx
    xdd = xd * jnp.exp(last - acsx)
    lane = lax.broadcasted_iota(jnp.int32, (q, LANE), 1)
    lo = lane < HEAD_DIM_A

    for g in range(N_GROUPS):
        gsl = slice(g * GROUP_W, (g + 1) * GROUP_W)
        b_off = D_INNER + g * D_STATE
        c_off = D_INNER + N_GROUPS * D_STATE + g * D_STATE
        bg = xc[:, b_off:b_off + D_STATE]
        cgb = xc[:, c_off:c_off + D_STATE].astype(BF16)
        bgb = bg.astype(BF16)
        b2 = jnp.concatenate([bgb, bgb], axis=0)
        cb2 = lax.dot_general(cgb, b2, (((1,), (1,)), ((), ())), preferred_element_type=F32)
        s_g = s_scr[:, gsl]
        y_off = jnp.dot(cgb, s_g.astype(BF16), preferred_element_type=F32) * e_in[:, gsl]
        for pp in range(GROUP_W // LANE):
            sl = slice(g * GROUP_W + pp * LANE, g * GROUP_W + (pp + 1) * LANE)
            m = (decay_in[:, sl] * cb2).astype(BF16)
            xp = xd[:, sl]
            bd = jnp.concatenate([jnp.where(lo, xp, 0.0), jnp.where(lo, 0.0, xp)], axis=0).astype(BF16)
            y_diag = jnp.dot(m, bd, preferred_element_type=F32)
            y_scr[:, sl] = y_diag + y_off[:, pp * LANE:(pp + 1) * LANE] + xs[:, sl] * dx_ref[:, sl]
        s_new = jnp.dot(bg.T.astype(BF16), xdd[:, gsl].astype(BF16), preferred_element_type=F32)
        s_scr[:, gsl] = s_g * e_last[:, gsl] + s_new

    for g in range(N_GROUPS):
        gsl = slice(g * GROUP_W, (g + 1) * GROUP_W)
        gz = y_scr[:, gsl] * _silu(z_ref[:, gsl].astype(F32))
        ms = jnp.mean(gz * gz, axis=-1, keepdims=True)
        g_ref[:, gsl] = (gz * lax.rsqrt(ms + NORM_EPS) * nw_ref[:, gsl]).astype(g_ref.dtype)

    @pl.when(c == pl.num_programs(1) - 1)
    def _():
        so_ref[0] = s_scr[...]


def ssd_mixer(z, xbc, dt, cw, cb, dtb, alog, dx, nw, e_mat, tri, shift, init_conv, init_ssm, nb, nc, valid_rows):
    q = SSD_Q
    const = lambda a: pl.BlockSpec(a.shape, lambda b, c: (0,) * a.ndim)
    return pl.pallas_call(
        functools.partial(_ssd_kernel, valid_rows=valid_rows),
        grid=(nb, nc),
        in_specs=[pl.BlockSpec((q, D_INNER), lambda b, c: (b * nc + c, 0)),
                  pl.BlockSpec((q, CONV_DIM), lambda b, c: (b * nc + c, 0)),
                  pl.BlockSpec((q, LANE), lambda b, c: (b * nc + c, 0)),
                  const(cw), const(cb), const(dtb), const(alog), const(dx), const(nw), const(e_mat), const(tri),
                  const(shift),
                  pl.BlockSpec((1, 8, CONV_DIM), lambda b, c: (b, 0, 0)),
                  pl.BlockSpec((1, D_STATE, D_INNER), lambda b, c: (b, 0, 0))],
        out_specs=[pl.BlockSpec((q, D_INNER), lambda b, c: (b * nc + c, 0)),
                   pl.BlockSpec((1, D_STATE, D_INNER), lambda b, c: (b, 0, 0))],
        out_shape=[jax.ShapeDtypeStruct((nb * nc * q, D_INNER), BF16),
                   jax.ShapeDtypeStruct((nb, D_STATE, D_INNER), F32)],
        scratch_shapes=[pltpu.VMEM((8, CONV_DIM), F32),
                        pltpu.VMEM((D_STATE, D_INNER), F32),
                        pltpu.VMEM((q, D_INNER), F32)],
        compiler_params=_cparams(("parallel", "arbitrary")),
        name="ssd_mixer",
    )(z, xbc, dt, cw, cb, dtb, alog, dx, nw, e_mat, tri, shift, init_conv, init_ssm)


def _attn_kernel(q_ref, k_ref, v_ref, u_ref, o_ref, qs_scr, acc_scr, c_scr, y_scr, t_scr, tot_scr,
                 *, tq, tk, pos0, n_kblocks):
    qi = pl.program_id(2)
    qv = q_ref[0]
    lane_q = lax.broadcasted_iota(jnp.int32, qv.shape, 1)
    zero_q = jnp.zeros_like(qv)
    qs_scr[0:tq, :] = jnp.where(lane_q < HEAD_DIM_B, qv, zero_q)
    qs_scr[tq:2 * tq, :] = jnp.where(lane_q < HEAD_DIM_B, zero_q, qv)
    y_scale = LOG2_E / math.sqrt(HEAD_DIM_B)
    q_idx = lax.broadcasted_iota(jnp.int32, (2 * tq, tk), 0) & (tq - 1)
    pos_gap = pos0 + qi * tq + q_idx - lax.broadcasted_iota(jnp.int32, (2 * tq, tk), 1)
    n_vis = jnp.minimum((pos0 + (qi + 1) * tq - 2) // tk + 1, n_kblocks)
    n_full = jnp.minimum((pos0 + qi * tq) // tk, n_kblocks)
    n_masked = n_vis - n_full

    def kblock(i):
        return jnp.maximum(n_vis - 1 - i, 0)

    def rows_of(ref, i):
        return ref[0, pl.ds(pl.multiple_of(kblock(i) * tk, tk), tk), :]

    def score_block(i):
        y = lax.dot_general(qs_scr[...], rows_of(k_ref, i), (((1,), (1,)), ((), ())), preferred_element_type=F32)
        return y * y_scale

    def suffix_block(y, i, masked):
        sp = jnp.maximum(y, jnp.log(1.0 + jnp.exp2(jnp.minimum(y, 126.0))) * LOG2_E)
        if masked:
            vis = pos_gap > kblock(i) * tk
            sp = jnp.where(vis, sp, 0.0)
        cs = jnp.dot(sp.astype(BF16), u_ref[...], preferred_element_type=F32)
        t = jnp.minimum(y - cs, 0.0)
        if masked:
            t = jnp.where(vis, t, HIDDEN)
        return t, cs[:, 0:1]

    def weigh_block(t, c, i):
        return jnp.dot(jnp.exp2(t - c).astype(BF16), rows_of(v_ref, i), preferred_element_type=F32)

    def first_two(mask_second):
        t0, tot0 = suffix_block(score_block(0), 0, True)
        t1, tot1 = suffix_block(score_block(1), 1, mask_second)
        acc_scr[...] = weigh_block(t0, 0.0, 0) + weigh_block(t1, tot0, 1)
        c_scr[...] = tot0 + tot1

    @pl.when((n_vis >= 2) & (n_masked >= 2))
    def _():
        first_two(True)

    @pl.when((n_vis >= 2) & (n_masked < 2))
    def _():
        first_two(False)

    @pl.when(n_vis < 2)
    def _():
        t0, tot0 = suffix_block(score_block(0), 0, True)
        acc_scr[...] = weigh_block(t0, 0.0, 0)
        c_scr[...] = tot0

    def scores(i, dst):
        y_scr[dst] = score_block(i)

    def suffix(i, masked, src, dst):
        t_scr[dst], tot_scr[dst] = suffix_block(y_scr[src], i, masked)

    def weigh(i, src):
        c = c_scr[...]
        acc_scr[...] += weigh_block(t_scr[src], c, i)
        c_scr[...] = c + tot_scr[src]

    def trip(i, masked):
        weigh(i, 0)
        suffix(i + 1, masked, 0, 0)
        scores(i + 2, 0)

    def group(size):
        def body(i):
            for j in range(size):
                cur, nxt = j & 1, (j + 1) & 1
                scores(i + j + 2, nxt)
                suffix(i + j + 1, False, cur, nxt)
                weigh(i + j, cur)
        return body

    def run(start, stop, step, body):
        def cond(i):
            return (i < stop) & (jnp.min(c_scr[...]) < SKIP_BITS)

        def step_body(i):
            body(i)
            return i + step

        lax.while_loop(cond, step_body, start)

    @pl.when((n_vis > 2) & (jnp.min(c_scr[...]) < SKIP_BITS))
    def _():
        scores(2, 0)
        suffix(2, True, 0, 0)
        scores(3, 0)
        n_mt = jnp.maximum(n_masked - 1, 2)
        first_pair = n_mt + ((n_vis - n_mt) & 1)
        first_group = first_pair + ((n_vis - first_pair) % ATTN_UNROLL)
        run(2, n_mt, 1, lambda i: trip(i, True))
        run(n_mt, first_pair, 1, lambda i: trip(i, False))
        run(first_pair, first_group, 2, group(2))
        run(first_group, n_vis, ATTN_UNROLL, group(ATTN_UNROLL))

    lane_o = lax.broadcasted_iota(jnp.int32, (tq, LANE), 1)
    o_ref[0] = jnp.where(lane_o < HEAD_DIM_B, acc_scr[0:tq, :], acc_scr[tq:2 * tq, :]).astype(o_ref.dtype)


def stick_breaking(q, k, v, u, pos0, tq):
    b, l, hd = q.shape
    kp = k.shape[1]
    tk = ATTN_TK
    assert tq & (tq - 1) == 0 and l % tq == 0 and kp % tk == 0
    return pl.pallas_call(
        functools.partial(_attn_kernel, tq=tq, tk=tk, pos0=pos0, n_kblocks=kp // tk),
        grid=(b, hd // LANE, l // tq),
        in_specs=[pl.BlockSpec((1, tq, LANE), lambda bi, p, i: (bi, i, p)),
                  pl.BlockSpec((1, kp, LANE), lambda bi, p, i: (bi, 0, p)),
                  pl.BlockSpec((1, kp, LANE), lambda bi, p, i: (bi, 0, p)),
                  pl.BlockSpec(u.shape, lambda bi, p, i: (0, 0))],
        out_specs=pl.BlockSpec((1, tq, LANE), lambda bi, p, i: (bi, i, p)),
        out_shape=jax.ShapeDtypeStruct((b, l, hd), BF16),
        scratch_shapes=[pltpu.VMEM((2 * tq, LANE), BF16), pltpu.VMEM((2 * tq, LANE), F32),
                        pltpu.VMEM((2 * tq, 1), F32), pltpu.VMEM((2, 2 * tq, tk), F32),
                        pltpu.VMEM((2, 2 * tq, tk), F32), pltpu.VMEM((2, 2 * tq, 1), F32)],
        compiler_params=_cparams(("parallel", "parallel", "arbitrary")),
        name="stick_breaking",
    )(q, k, v, u)


class _Path:
    def __init__(self, x3, mods, kv_mod, conv_state, ssm_state, past_k, past_v, p):
        self.b, self.l, self.d = x3.shape
        self.t = self.b * self.l
        self.x = x3.reshape(self.t, self.d)
        self.mods, self.kv_mod, self.p = mods, kv_mod, p
        self.conv_state, self.ssm_state, self.past_k, self.past_v = conv_state, ssm_state, past_k, past_v
        self.per_token = self.l < TOKEN_TILE
        self.rows_per_group = min(TOKEN_TILE, self.t) if self.per_token else self.l
        self.pos0 = past_k.shape[1]
        self.conv_out, self.ssm_out = [], []

    def mod3(self, v):
        if self.per_token:
            tile = min(TOKEN_TILE, self.t)
            return jnp.repeat(v, self.l, axis=0).reshape(self.t // tile, tile, self.d)
        return v[:, None, :]

    def mix_and_route(self, layer):
        p, b, l, d, t, rpg, mod3 = self.p, self.b, self.l, self.d, self.t, self.rows_per_group, self.mod3
        sh1, sc1, g1, sh2, sc2, self.g2 = jnp.split(self.mods[layer], 6, axis=-1)
        nw1 = p['mix_norm_w'][layer][None, :]
        x = self.x
        if layer < N_A_LAYERS:
            i = layer
            z, xbc, dt = norm_mod_matmul(x, nw1, mod3(sh1), mod3(sc1), [p['w_z'][i], p['w_xbc'][i], p['w_dt'][i]],
                                         [BF16, BF16, F32], rpg)
            xbc3 = xbc.reshape(b, l, CONV_DIM)
            self.conv_out.append(xbc3[:, l - (CONV_W - 1):].astype(F32))
            nc = max(l // SSD_Q, 1)
            if l < SSD_Q:
                pad = lambda a: jnp.pad(a.reshape(b, l, -1), ((0, 0), (0, SSD_Q - l), (0, 0))).reshape(b * SSD_Q, -1)
                z, xbc, dt = pad(z), pad(xbc), pad(dt)
            ic = jnp.pad(self.conv_state[i].astype(F32), ((0, 0), (8 - (CONV_W - 1), 0), (0, 0)))
            iss = self.ssm_state[i].astype(F32).transpose(0, 3, 1, 2).reshape(b, D_STATE, D_INNER)
            gm, s_fin = ssd_mixer(z, xbc, dt, p['a_conv_w'][i], p['a_conv_b'][i][None, :], p['dt_bias'][i],
                                  p['a_log'][i], p['d_x'][i], p['a_norm_w'][i][None, :], p['e_mat'], p['tri'], p['shift'],
                                  ic, iss, b, nc, min(l, SSD_Q))
            if l < SSD_Q:
                gm = gm.reshape(b, SSD_Q, D_INNER)[:, :l].reshape(t, D_INNER)
            self.ssm_out.append(s_fin.reshape(b, D_STATE, N_HEADS_A, HEAD_DIM_A).transpose(0, 2, 3, 1))
            x = matmul_residual(gm, p['a_out_proj'][i], x, mod3(g1), rpg)
        else:
            j = layer - N_A_LAYERS
            (qp,) = norm_mod_matmul(x, nw1, mod3(sh1), mod3(sc1), [p['b_w_q'][j]], [BF16], rpg)
            o = stick_breaking(qp.reshape(b, l, d), self.kb, self.vb, p['u_tri'], self.pos0, min(l, 256))
            x = matmul_residual(o.reshape(t, d), p['b_w_o'][j], x, mod3(g1), rpg)
        self.x = x
        self.moe = moe_route(x, p['ffn_norm_w'][layer][None, :], mod3(sh2), mod3(sc2), p['router_wt'][layer],
                             p['router_b'][layer], p['su_tri'], p['sl_tri'], rpg)

    def experts(self, layer):
        p = self.p
        self.ysg = moe_experts(self.moe, p['moe_w1'], p['moe_b1'], p['moe_w2'], p['moe_b2'], layer)

    def combine(self, layer):
        p, b, l, d, rpg, mod3 = self.p, self.b, self.l, self.d, self.rows_per_group, self.mod3
        self.x = moe_combine(self.ysg, self.moe['gates'], self.moe['pos'], self.x, mod3(self.g2), rpg)
        if layer == N_A_LAYERS - 1:
            sh_kv, sc_kv = jnp.split(self.kv_mod, 2, axis=-1)
            self.k_new, self.v_new = norm_mod_matmul(self.x, p['kv_norm_w'][None, :], mod3(sh_kv), mod3(sc_kv),
                                                     [p['w_k'], p['w_v']], [F32, F32], rpg)
            k_all = jnp.concatenate([self.past_k.reshape(b, self.pos0, d), self.k_new.reshape(b, l, d)], axis=1)
            v_all = jnp.concatenate([self.past_v.reshape(b, self.pos0, d), self.v_new.reshape(b, l, d)], axis=1)
            kpad = (-k_all.shape[1]) % ATTN_TK
            self.kb = jnp.pad(k_all, ((0, 0), (0, kpad), (0, 0))).astype(BF16)
            self.vb = jnp.pad(v_all, ((0, 0), (0, kpad), (0, 0))).astype(BF16)

    def finish(self):
        b, l, d = self.b, self.l, self.d
        y = final_rmsnorm(self.x, self.p['final_norm_w'][None, :])
        return (y.reshape(b, l, d), jnp.stack(self.conv_out), jnp.stack(self.ssm_out),
                self.k_new.reshape(b, l, N_HEADS_B, HEAD_DIM_B), self.v_new.reshape(b, l, N_HEADS_B, HEAD_DIM_B))


def _run_paths(paths):
    for layer in range(DEPTH):
        for step in (_Path.mix_and_route, _Path.experts, _Path.combine):
            for path in paths:
                step(path, layer)
    return [path.finish() for path in paths]


def kernel(x_prompt, x_sample, state_conv, state_ssm, cache_k, cache_v, c_prompt, c_sample, mod_w, mod_b,
           mix_norm_w, ffn_norm_w, a_in_proj, a_conv_w, a_conv_b, a_dt_bias, a_A_log, a_D, a_norm_w, a_out_proj,
           kv_mod_w, kv_mod_b, kv_norm_w, w_kv, b_w_q, b_w_o, router_w, router_b, moe_w1, moe_b1, moe_w2, moe_b2,
           final_norm_w):
    bp = x_prompt.shape[0]
    hb = N_HEADS_B * HEAD_DIM_B
    pad_heads = lambda a: jnp.pad(a, ((0, 0), (0, LANE - N_HEADS_A)))[:, None, :]
    head_of_lane = jnp.arange(D_INNER) // HEAD_DIM_A
    p = dict(
        mix_norm_w=mix_norm_w, ffn_norm_w=ffn_norm_w, kv_norm_w=kv_norm_w, final_norm_w=final_norm_w,
        w_z=a_in_proj[:, :, :D_INNER].astype(BF16),
        w_xbc=a_in_proj[:, :, D_INNER:D_INNER + CONV_DIM].astype(BF16),
        w_dt=jnp.pad(a_in_proj[:, :, D_INNER + CONV_DIM:], ((0, 0), (0, 0), (0, LANE - N_HEADS_A))).astype(BF16),
        a_conv_w=a_conv_w, a_conv_b=a_conv_b, dt_bias=pad_heads(a_dt_bias), a_log=pad_heads(a_A_log),
        d_x=a_D[:, head_of_lane][:, None, :], a_norm_w=a_norm_w, a_out_proj=a_out_proj.astype(BF16),
        e_mat=jnp.tile((jnp.arange(LANE)[:, None] == head_of_lane[None, :]).astype(BF16), (3, 1)),
        tri=jnp.tile((jnp.arange(SSD_Q)[:, None] >= jnp.arange(SSD_Q)[None, :]).astype(BF16), (1, 3)),
        shift=jnp.stack([(jnp.arange(SSD_Q)[:, None] + tap - (CONV_W - 1) == jnp.arange(SSD_Q)[None, :])
                         for tap in range(CONV_W - 1)]).astype(BF16),
        u_tri=(jnp.arange(ATTN_TK)[:, None] >= jnp.arange(ATTN_TK)[None, :]).astype(BF16),
        w_k=w_kv[:, :hb].astype(BF16), w_v=w_kv[:, hb:].astype(BF16),
        b_w_q=b_w_q.astype(BF16), b_w_o=b_w_o.astype(BF16),
        router_wt=router_w.transpose(0, 2, 1), router_b=router_b[:, :, None],
        su_tri=(jnp.arange(TOKEN_TILE)[:, None] < jnp.arange(TOKEN_TILE)[None, :]).astype(BF16),
        sl_tri=(jnp.arange(N_EXPERTS)[:, None] > jnp.arange(N_EXPERTS)[None, :]).astype(BF16),
        moe_w1=moe_w1, moe_b1=moe_b1[:, :, None, :], moe_w2=moe_w2, moe_b2=moe_b2[:, :, None, :],
    )
    c_all = jnp.concatenate([c_prompt, c_sample], axis=0)
    mods = mod_vectors(c_all, mod_w, mod_b[:, None, :])
    kv_mod = mod_vectors(c_all, kv_mod_w[None], kv_mod_b[None, None, :])[0]

    bg = bp // PROMPT_GROUPS
    conv0 = jnp.zeros((N_A_LAYERS, bg, CONV_W - 1, CONV_DIM), F32)
    ssm0 = jnp.zeros((N_A_LAYERS, bg, N_HEADS_A, HEAD_DIM_A, D_STATE), F32)
    kv0 = jnp.zeros((bg, 0, N_HEADS_B, HEAD_DIM_B), F32)
    paths = [_Path(x_prompt[g * bg:(g + 1) * bg], mods[:, g * bg:(g + 1) * bg], kv_mod[g * bg:(g + 1) * bg],
                   conv0, ssm0, kv0, kv0, p) for g in range(PROMPT_GROUPS)]
    paths.append(_Path(x_sample, mods[:, bp:], kv_mod[bp:], state_conv, state_ssm, cache_k, cache_v, p))
    *groups, out_s = _run_paths(paths)
    out_p = tuple(jnp.concatenate([grp[i] for grp in groups], axis=ax) for i, ax in enumerate((0, 1, 1, 0, 0)))
    return (out_p[0], out_s[0]) + out_p[1:] + out_s[1:]
```

```python
import functools
import math

import jax
import jax.numpy as jnp
from jax import lax
from jax.experimental import pallas as pl
from jax.experimental.pallas import tpu as pltpu

F32 = jnp.float32
BF16 = jnp.bfloat16
HIGHEST = lax.Precision.HIGHEST

D_MODEL = 1024
DEPTH = 4
N_A_LAYERS = 2
D_INNER = 2048
HEAD_DIM_A = 64
N_HEADS_A = 32
D_STATE = 128
N_GROUPS = 4
GROUP_W = D_INNER // N_GROUPS
CONV_W = 4
CONV_DIM = D_INNER + 2 * N_GROUPS * D_STATE
HEAD_DIM_B = 64
N_HEADS_B = 16
N_EXPERTS = 32
TOP_K = 4
D_FF = 1024
SWIGLU_ALPHA = 1.702
SWIGLU_LIMIT = 7.0
NORM_EPS = 1e-6
LOG2_E = 1.4426950408889634
HIDDEN = -1e30
SKIP_BITS = 160.0

LANE = 128
SSD_Q = 64
TOKEN_TILE = 512
MOE_TILE = 512
ATTN_TK = 256
ATTN_UNROLL = 4
ATTN_PAIRS = 2
PROMPT_GROUPS = 1
VMEM_LIMIT = 56 * 1024 * 1024


def _cparams(sem):
    return pltpu.CompilerParams(dimension_semantics=sem, vmem_limit_bytes=VMEM_LIMIT)


def _softplus(x):
    return jnp.maximum(x, 0.0) + jnp.log1p(jnp.exp(-jnp.abs(x)))


def _silu(x):
    return x * jax.nn.sigmoid(x)


def _norm_mod(x, nw, sh, sc):
    ms = jnp.mean(x * x, axis=-1, keepdims=True)
    return (x * lax.rsqrt(ms + NORM_EPS) * nw) * (1.0 + sc) + sh


def _mod_spec(mod3, tile, rows_per_group):
    steps = max(rows_per_group // tile, 1)
    g = mod3.shape[1]
    return pl.BlockSpec((1, g, mod3.shape[2]), lambda i: (i // steps, 0, 0))


def _mod_kernel(c_ref, w_ref, b_ref, o_ref):
    cs = _silu(c_ref[...])
    o_ref[0] = jnp.dot(cs, w_ref[0], precision=HIGHEST, preferred_element_type=F32) + b_ref[0]


def mod_vectors(c, w, b):
    nl, d, n = w.shape
    tn = 1024
    return pl.pallas_call(
        _mod_kernel,
        grid=(nl, n // tn),
        in_specs=[pl.BlockSpec(c.shape, lambda l, j: (0, 0)),
                  pl.BlockSpec((1, d, tn), lambda l, j: (l, 0, j)),
                  pl.BlockSpec((1, 1, tn), lambda l, j: (l, 0, j))],
        out_specs=pl.BlockSpec((1, c.shape[0], tn), lambda l, j: (l, 0, j)),
        out_shape=jax.ShapeDtypeStruct((nl, c.shape[0], n), F32),
        compiler_params=_cparams(("arbitrary", "arbitrary")),
        name="mod_vectors",
    )(c, w, b)


def _nmm_kernel(x_ref, nw_ref, sh_ref, sc_ref, *refs, n_w):
    h = _norm_mod(x_ref[...], nw_ref[...], sh_ref[0], sc_ref[0]).astype(BF16)
    for w_ref, o_ref in zip(refs[:n_w], refs[n_w:]):
        o_ref[...] = jnp.dot(h, w_ref[...], preferred_element_type=F32).astype(o_ref.dtype)


def norm_mod_matmul(x, nw, sh3, sc3, ws, out_dtypes, rows_per_group):
    t, d = x.shape
    tm = min(TOKEN_TILE, t)
    in_specs = [pl.BlockSpec((tm, d), lambda i: (i, 0)),
                pl.BlockSpec((1, d), lambda i: (0, 0)),
                _mod_spec(sh3, tm, rows_per_group), _mod_spec(sc3, tm, rows_per_group)]
    in_specs += [pl.BlockSpec(w.shape, lambda i: (0, 0), pipeline_mode=pl.Buffered(1)) for w in ws]
    return pl.pallas_call(
        functools.partial(_nmm_kernel, n_w=len(ws)),
        grid=(t // tm,),
        in_specs=in_specs,
        out_specs=[pl.BlockSpec((tm, w.shape[1]), lambda i: (i, 0)) for w in ws],
        out_shape=[jax.ShapeDtypeStruct((t, w.shape[1]), dt) for w, dt in zip(ws, out_dtypes)],
        compiler_params=_cparams(("parallel",)),
        name="norm_mod_matmul",
    )(x, nw, sh3, sc3, *ws)


def _mmres_kernel(a_ref, w_ref, x_ref, g_ref, o_ref):
    acc = jnp.dot(a_ref[...], w_ref[...], preferred_element_type=F32)
    o_ref[...] = x_ref[...] + g_ref[0] * acc


def matmul_residual(a, w, x, g3, rows_per_group):
    t, d = x.shape
    tm = min(TOKEN_TILE, t)
    return pl.pallas_call(
        _mmres_kernel,
        grid=(t // tm,),
        in_specs=[pl.BlockSpec((tm, a.shape[1]), lambda i: (i, 0)),
                  pl.BlockSpec(w.shape, lambda i: (0, 0), pipeline_mode=pl.Buffered(1)),
                  pl.BlockSpec((tm, d), lambda i: (i, 0)),
                  _mod_spec(g3, tm, rows_per_group)],
        out_specs=pl.BlockSpec((tm, d), lambda i: (i, 0)),
        out_shape=jax.ShapeDtypeStruct((t, d), F32),
        compiler_params=_cparams(("parallel",)),
        name="matmul_residual",
    )(a, w, x, g3)


def _rms_kernel(x_ref, nw_ref, o_ref):
    x = x_ref[...]
    ms = jnp.mean(x * x, axis=-1, keepdims=True)
    o_ref[...] = x * lax.rsqrt(ms + NORM_EPS) * nw_ref[...]


def final_rmsnorm(x, nw):
    t, d = x.shape
    tm = min(TOKEN_TILE, t)
    return pl.pallas_call(
        _rms_kernel,
        grid=(t // tm,),
        in_specs=[pl.BlockSpec((tm, d), lambda i: (i, 0)), pl.BlockSpec((1, d), lambda i: (0, 0))],
        out_specs=pl.BlockSpec((tm, d), lambda i: (i, 0)),
        out_shape=jax.ShapeDtypeStruct((t, d), F32),
        compiler_params=_cparams(("parallel",)),
        name="final_rmsnorm",
    )(x, nw)


def _router_kernel(x_ref, nw_ref, sh_ref, sc_ref, rwt_ref, rb_ref, su_ref, sl_ref, hs_ref, g_ref, p_ref, cnt_ref,
                   run_ref, tot_ref, run_scr):
    @pl.when(pl.program_id(0) == 0)
    def _():
        run_scr[...] = jnp.zeros_like(run_scr)

    h = _norm_mod(x_ref[...], nw_ref[...], sh_ref[0], sc_ref[0])
    logits = lax.dot_general(rwt_ref[...], h, (((1,), (1,)), ((), ())),
                             precision=HIGHEST, preferred_element_type=F32) + rb_ref[...]
    eid = lax.broadcasted_iota(jnp.int32, logits.shape, 0)
    vals, idxs, hits = [], [], []
    for _ in range(TOP_K):
        m = jnp.max(logits, axis=0, keepdims=True)
        idx = jnp.min(jnp.where(logits == m, eid, N_EXPERTS), axis=0, keepdims=True)
        hit = eid == idx
        vals.append(m)
        idxs.append(idx)
        hits.append(hit)
        logits = jnp.where(hit, -jnp.inf, logits)
    ex = [jnp.exp(v - vals[0]) for v in vals]
    den = ex[0] + ex[1] + ex[2] + ex[3]
    cnt = (hits[0] | hits[1] | hits[2] | hits[3]).astype(F32)
    cnt_b = cnt.astype(BF16)
    earlier = jnp.dot(cnt_b, su_ref[...], preferred_element_type=F32)
    lower = jnp.dot(sl_ref[...], cnt_b, preferred_element_type=F32)
    cnt_tile = jnp.sum(cnt, axis=1, keepdims=True)
    where_to = earlier + jnp.sum(lower, axis=1, keepdims=True)
    tm = x_ref.shape[0]
    slot = lax.broadcasted_iota(jnp.int32, (TOP_K * tm, tm), 0)
    place = None
    for k in range(TOP_K):
        pos = jnp.sum(jnp.where(hits[k], where_to, 0.0), axis=0, keepdims=True).astype(jnp.int32)
        g_ref[k:k + 1, :] = ex[k] / den
        p_ref[k:k + 1, :] = pos
        place = (slot == pos) if place is None else place | (slot == pos)
    hs_ref[...] = jnp.dot(place.astype(BF16), h.astype(BF16), preferred_element_type=F32).astype(BF16)
    cnt_ref[0] = cnt_tile
    run_ref[0] = run_scr[...]
    run_scr[...] += cnt_tile
    tot_ref[...] = run_scr[...]


def moe_router(x, nw, sh3, sc3, rwt, rb, su, sl, rows_per_group):
    t, d = x.shape
    tm = min(TOKEN_TILE, t)
    n_tiles = t // tm
    return pl.pallas_call(
        _router_kernel,
        grid=(n_tiles,),
        in_specs=[pl.BlockSpec((tm, d), lambda i: (i, 0)),
                  pl.BlockSpec((1, d), lambda i: (0, 0)),
                  _mod_spec(sh3, tm, rows_per_group), _mod_spec(sc3, tm, rows_per_group),
                  pl.BlockSpec(rwt.shape, lambda i: (0, 0)),
                  pl.BlockSpec(rb.shape, lambda i: (0, 0)),
                  pl.BlockSpec(su.shape, lambda i: (0, 0)),
                  pl.BlockSpec(sl.shape, lambda i: (0, 0))],
        out_specs=[pl.BlockSpec((TOP_K * tm, d), lambda i: (i, 0)),
                   pl.BlockSpec((TOP_K, tm), lambda i: (0, i)),
                   pl.BlockSpec((TOP_K, tm), lambda i: (0, i)),
                   pl.BlockSpec((1, N_EXPERTS, 1), lambda i: (i, 0, 0)),
                   pl.BlockSpec((1, N_EXPERTS, 1), lambda i: (i, 0, 0)),
                   pl.BlockSpec((N_EXPERTS, 1), lambda i: (0, 0))],
        out_shape=[jax.ShapeDtypeStruct((TOP_K * t, d), BF16),
                   jax.ShapeDtypeStruct((TOP_K, t), F32),
                   jax.ShapeDtypeStruct((TOP_K, t), jnp.int32),
                   jax.ShapeDtypeStruct((n_tiles, N_EXPERTS, 1), F32),
                   jax.ShapeDtypeStruct((n_tiles, N_EXPERTS, 1), F32),
                   jax.ShapeDtypeStruct((N_EXPERTS, 1), F32)],
        scratch_shapes=[pltpu.VMEM((N_EXPERTS, 1), F32)],
        compiler_params=_cparams(("arbitrary",)),
        name="moe_router",
    )(x, nw, sh3, sc3, rwt, rb, su, sl)


def _ffn_kernel(be_ref, nu_ref, xs_ref, w1_ref, b1_ref, w2_ref, b2_ref, o_ref, w1_scr, w2_scr):
    i = pl.program_id(0)

    @pl.when(i < nu_ref[0])
    def _():
        @pl.when((i == 0) | (be_ref[i] != be_ref[jnp.maximum(i - 1, 0)]))
        def _():
            w1_scr[...] = w1_ref[0, 0].astype(BF16)
            w2_scr[...] = w2_ref[0, 0].astype(BF16)

        hid = jnp.dot(xs_ref[...], w1_scr[...], preferred_element_type=F32) + b1_ref[0, 0]
        glu = jnp.minimum(hid[:, :D_FF], SWIGLU_LIMIT)
        lin = jnp.clip(hid[:, D_FF:], -SWIGLU_LIMIT, SWIGLU_LIMIT)
        act = glu * jax.nn.sigmoid(SWIGLU_ALPHA * glu) * (lin + 1.0)
        out = jnp.dot(act.astype(BF16), w2_scr[...], preferred_element_type=F32) + b2_ref[0, 0]
        o_ref[...] = out.astype(o_ref.dtype)


def moe_ffn_blocks(xs, block_e, n_used, w1, b1, w2, b2, layer):
    n_rows, d = xs.shape
    tm = MOE_TILE
    grid_spec = pltpu.PrefetchScalarGridSpec(
        num_scalar_prefetch=2,
        grid=(n_rows // tm,),
        in_specs=[pl.BlockSpec((tm, d), lambda i, be, nu: (i, 0)),
                  pl.BlockSpec((1, 1, d, 2 * D_FF), lambda i, be, nu: (layer, be[i], 0, 0)),
                  pl.BlockSpec((1, 1, 1, 2 * D_FF), lambda i, be, nu: (layer, be[i], 0, 0)),
                  pl.BlockSpec((1, 1, D_FF, d), lambda i, be, nu: (layer, be[i], 0, 0)),
                  pl.BlockSpec((1, 1, 1, d), lambda i, be, nu: (layer, be[i], 0, 0))],
        out_specs=pl.BlockSpec((tm, d), lambda i, be, nu: (i, 0)),
        scratch_shapes=[pltpu.VMEM((d, 2 * D_FF), BF16), pltpu.VMEM((D_FF, d), BF16)],
    )
    return pl.pallas_call(
        _ffn_kernel,
        grid_spec=grid_spec,
        out_shape=jax.ShapeDtypeStruct((n_rows, d), BF16),
        compiler_params=_cparams(("arbitrary",)),
        name="moe_ffn",
    )(block_e, n_used, xs, w1, b1, w2, b2)


def _combine_kernel(y_ref, gt_ref, p_ref, x_ref, g_ref, o_ref):
    tm = x_ref.shape[0]
    eye = lax.broadcasted_iota(jnp.int32, (tm, tm), 0) == lax.broadcasted_iota(jnp.int32, (tm, tm), 1)
    col = lax.broadcasted_iota(jnp.int32, (tm, TOP_K * tm), 1)
    pick = jnp.zeros((tm, TOP_K * tm), F32)
    for k in range(TOP_K):
        g_col = jnp.sum(jnp.where(eye, gt_ref[k:k + 1, :], 0.0), axis=1, keepdims=True)
        p_col = jnp.sum(jnp.where(eye, p_ref[k:k + 1, :], 0), axis=1, keepdims=True)
        pick = jnp.where(col == p_col, g_col, pick)
    ff = jnp.dot(pick.astype(BF16), y_ref[...], preferred_element_type=F32)
    o_ref[...] = x_ref[...] + g_ref[0] * ff


def moe_combine(ysg, gates, pos, x, g3, rows_per_group):
    t, d = x.shape
    tm = min(TOKEN_TILE, t)
    return pl.pallas_call(
        _combine_kernel,
        grid=(t // tm,),
        in_specs=[pl.BlockSpec((TOP_K * tm, d), lambda i: (i, 0)),
                  pl.BlockSpec((TOP_K, tm), lambda i: (0, i)),
                  pl.BlockSpec((TOP_K, tm), lambda i: (0, i)),
                  pl.BlockSpec((tm, d), lambda i: (i, 0)),
                  _mod_spec(g3, tm, rows_per_group)],
        out_specs=pl.BlockSpec((tm, d), lambda i: (i, 0)),
        out_shape=jax.ShapeDtypeStruct((t, d), F32),
        compiler_params=_cparams(("parallel",)),
        name="moe_combine",
    )(ysg, gates, pos, x, g3)


def _take_rows(a, idx):
    return a.at[idx].get(mode="promise_in_bounds")


def moe_route(x, nw, sh3, sc3, rwt, rb, su, sl, rows_per_group):
    t, d = x.shape
    tt = min(TOKEN_TILE, t)
    n_tiles, tl = t // tt, TOP_K * tt
    hs, gates, pos, cnt, run, tot = moe_router(x, nw, sh3, sc3, rwt, rb, su, sl, rows_per_group)
    tm = MOE_TILE
    n_blocks = (t * TOP_K + N_EXPERTS * (tm - 1) + tm - 1) // tm
    i32 = jnp.int32
    counts = tot[:, 0].astype(i32)
    padded = (counts + tm - 1) // tm * tm
    pend = jnp.cumsum(padded)
    pstart = (pend - padded).astype(i32)
    block_e = jnp.minimum(jnp.sum((pend[None, :] <= (jnp.arange(n_blocks) * tm)[:, None]).astype(i32), axis=1),
                          N_EXPERTS - 1)
    n_used = (pend[-1] // tm).astype(i32).reshape(1)
    cnt = cnt[:, :, 0].astype(i32)
    run = run[:, :, 0].astype(i32)
    eoff = jnp.cumsum(cnt, axis=1) - cnt
    tiles = jnp.arange(n_tiles, dtype=i32)
    in_e = (jnp.arange(n_blocks, dtype=i32) * tm - pstart[block_e])[:, None] + jnp.arange(tm, dtype=i32)[None, :]
    run_e = run.T[block_e]
    tile = jnp.sum((run_e[:, None, :] <= in_e[:, :, None]).astype(i32), axis=-1) - 1
    shift = jnp.sum(jnp.where(tiles[None, None, :] == tile[:, :, None], (eoff.T[block_e] - run_e)[:, None, :], 0),
                    axis=-1)
    spread = (jnp.arange(n_blocks * tm, dtype=i32) % (TOP_K * t)).reshape(n_blocks, tm)
    src = jnp.where(in_e < counts[block_e][:, None], tile * tl + shift + in_e, spread)
    j = jnp.arange(tl, dtype=i32)
    e_j = jnp.sum(((eoff + cnt)[:, None, :] <= j[None, :, None]).astype(i32), axis=-1)
    base = pstart[None, :] + run - eoff
    back = jnp.sum(jnp.where(jnp.arange(N_EXPERTS, dtype=i32)[None, None, :] == e_j[:, :, None], base[:, None, :], 0),
                   axis=-1) + j[None, :]
    return dict(xs=_take_rows(hs, src.reshape(-1)), block_e=block_e, n_used=n_used, back=back.reshape(-1),
                gates=gates, pos=pos)


def moe_experts(m, w1, b1, w2, b2, layer):
    ys = moe_ffn_blocks(m['xs'], m['block_e'], m['n_used'], w1, b1, w2, b2, layer)
    return _take_rows(ys, m['back'])


def _split3(x):
    hi = x.astype(BF16)
    r1 = x - hi.astype(F32)
    mid = r1.astype(BF16)
    lo = (r1 - mid.astype(F32)).astype(BF16)
    return hi, mid, lo


def _ssd_kernel(z_ref, xbc_ref, dt_ref, cw_ref, cb_ref, dtb_ref, alog_ref, dx_ref, nw_ref, e_ref, tri_ref, shift_ref,
                ic_ref, is_ref, g_ref, so_ref, tail_scr, s_scr, y_scr, *, valid_rows):
    q = SSD_Q
    c = pl.program_id(1)

    @pl.when(c == 0)
    def _():
        tail_scr[...] = ic_ref[0]
        s_scr[...] = is_ref[0]

    xb = xbc_ref[...]
    acc = cb_ref[...] + xb.astype(F32) * cw_ref[CONV_W - 1:CONV_W, :]
    for tap in range(CONV_W - 1):
        shifted = jnp.dot(shift_ref[tap], xb, preferred_element_type=F32)
        acc = acc + shifted * cw_ref[tap:tap + 1, :]
    t0, t1, t2 = (tail_scr[8 - (CONV_W - 1) + j:8 - (CONV_W - 1) + j + 1, :] for j in range(CONV_W - 1))
    w0, w1, w2 = (cw_ref[j:j + 1, :] for j in range(CONV_W - 1))
    row8 = lax.broadcasted_iota(jnp.int32, (8, CONV_DIM), 0)
    head = jnp.where(row8 == 0, t0 * w0 + t1 * w1 + t2 * w2,
                     jnp.where(row8 == 1, t1 * w0 + t2 * w1, jnp.where(row8 == 2, t2 * w0, 0.0)))
    tail_scr[...] = xbc_ref[q - 16:q, :].astype(F32)[8:16, :]
    xc = _silu(jnp.concatenate([acc[0:8, :] + head, acc[8:q, :]], axis=0))
    xs = xc[:, :D_INNER]

    dt = _softplus(dt_ref[...] + dtb_ref[...])
    if valid_rows < q:
        rows = lax.broadcasted_iota(jnp.int32, dt.shape, 0)
        dt = jnp.where(rows < valid_rows, dt, 0.0)
    a = dt * (-jnp.exp(alog_ref[...]))
    a_cs = jnp.dot(tri_ref[...], jnp.concatenate(_split3(a), axis=0), preferred_element_type=F32)
    dtx = jnp.dot(jnp.concatenate(_split3(dt), axis=1), e_ref[...], preferred_element_type=F32)
    acsx = jnp.dot(jnp.concatenate(_split3(a_cs), axis=1), e_ref[...], preferred_element_type=F32)

    row = lax.broadcasted_iota(jnp.int32, (q, D_INNER), 0)
    col = lax.broadcasted_iota(jnp.int32, (q, D_INNER), 1) & (HEAD_DIM_A - 1)
    acs_row = jnp.sum(jnp.where(row == col, acsx, 0.0), axis=0, keepdims=True)
    decay_in = jnp.where(row >= col, jnp.exp(jnp.minimum(acsx - acs_row, 0.0)), 0.0)
    last = acsx[q - 1:q, :]
    e_in = jnp.exp(acsx)
    e_last = jnp.exp(last)
    xd = xs * dtx
    xdd = xd * jnp.exp(last - acsx)
    lane = lax.broadcasted_iota(jnp.int32, (q, LANE), 1)
    lo = lane < HEAD_DIM_A

    for g in range(N_GROUPS):
        gsl = slice(g * GROUP_W, (g + 1) * GROUP_W)
        b_off = D_INNER + g * D_STATE
        c_off = D_INNER + N_GROUPS * D_STATE + g * D_STATE
        bg = xc[:, b_off:b_off + D_STATE]
        cgb = xc[:, c_off:c_off + D_STATE].astype(BF16)
        bgb = bg.astype(BF16)
        b2 = jnp.concatenate([bgb, bgb], axis=0)
        cb2 = lax.dot_general(cgb, b2, (((1,), (1,)), ((), ())), preferred_element_type=F32)
        s_g = s_scr[:, gsl]
        y_off = jnp.dot(cgb, s_g.astype(BF16), preferred_element_type=F32) * e_in[:, gsl]
        for pp in range(GROUP_W // LANE):
            sl = slice(g * GROUP_W + pp * LANE, g * GROUP_W + (pp + 1) * LANE)
            m = (decay_in[:, sl] * cb2).astype(BF16)
            xp = xd[:, sl]
            bd = jnp.concatenate([jnp.where(lo, xp, 0.0), jnp.where(lo, 0.0, xp)], axis=0).astype(BF16)
            y_diag = jnp.dot(m, bd, preferred_element_type=F32)
            y_scr[:, sl] = y_diag + y_off[:, pp * LANE:(pp + 1) * LANE] + xs[:, sl] * dx_ref[:, sl]
        s_new = jnp.dot(bg.T.astype(BF16), xdd[:, gsl].astype(BF16), preferred_element_type=F32)
        s_scr[:, gsl] = s_g * e_last[:, gsl] + s_new

    for g in range(N_GROUPS):
        gsl = slice(g * GROUP_W, (g + 1) * GROUP_W)
        gz = y_scr[:, gsl] * _silu(z_ref[:, gsl].astype(F32))
        ms = jnp.mean(gz * gz, axis=-1, keepdims=True)
        g_ref[:, gsl] = (gz * lax.rsqrt(ms + NORM_EPS) * nw_ref[:, gsl]).astype(g_ref.dtype)

    @pl.when(c == pl.num_programs(1) - 1)
    def _():
        so_ref[0] = s_scr[...]


def ssd_mixer(z, xbc, dt, cw, cb, dtb, alog, dx, nw, e_mat, tri, shift, init_conv, init_ssm, nb, nc, valid_rows):
    q = SSD_Q
    const = lambda a: pl.BlockSpec(a.shape, lambda b, c: (0,) * a.ndim)
    return pl.pallas_call(
        functools.partial(_ssd_kernel, valid_rows=valid_rows),
        grid=(nb, nc),
        in_specs=[pl.BlockSpec((q, D_INNER), lambda b, c: (b * nc + c, 0)),
                  pl.BlockSpec((q, CONV_DIM), lambda b, c: (b * nc + c, 0)),
                  pl.BlockSpec((q, LANE), lambda b, c: (b * nc + c, 0)),
                  const(cw), const(cb), const(dtb), const(alog), const(dx), const(nw), const(e_mat), const(tri),
                  const(shift),
                  pl.BlockSpec((1, 8, CONV_DIM), lambda b, c: (b, 0, 0)),
                  pl.BlockSpec((1, D_STATE, D_INNER), lambda b, c: (b, 0, 0))],
        out_specs=[pl.BlockSpec((q, D_INNER), lambda b, c: (b * nc + c, 0)),
                   pl.BlockSpec((1, D_STATE, D_INNER), lambda b, c: (b, 0, 0))],
        out_shape=[jax.ShapeDtypeStruct((nb * nc * q, D_INNER), BF16),
                   jax.ShapeDtypeStruct((nb, D_STATE, D_INNER), F32)],
        scratch_shapes=[pltpu.VMEM((8, CONV_DIM), F32),
                        pltpu.VMEM((D_STATE, D_INNER), F32),
                        pltpu.VMEM((q, D_INNER), F32)],
        compiler_params=_cparams(("parallel", "arbitrary")),
        name="ssd_mixer",
    )(z, xbc, dt, cw, cb, dtb, alog, dx, nw, e_mat, tri, shift, init_conv, init_ssm)


def _attn_kernel(*refs, n_pairs, **static):
    for pair in range(n_pairs):
        _attn_pair(slice(pair * LANE, (pair + 1) * LANE), *refs, **static)


def _attn_pair(lanes, q_ref, k_ref, v_ref, u_ref, o_ref, qs_scr, acc_scr, c_scr, y_scr, t_scr, tot_scr,
               *, tq, tk, pos0, n_kblocks):
    qi = pl.program_id(2)
    qv = q_ref[0, :, lanes]
    lane_q = lax.broadcasted_iota(jnp.int32, qv.shape, 1)
    zero_q = jnp.zeros_like(qv)
    qs_scr[0:tq, :] = jnp.where(lane_q < HEAD_DIM_B, qv, zero_q)
    qs_scr[tq:2 * tq, :] = jnp.where(lane_q < HEAD_DIM_B, zero_q, qv)
    y_scale = LOG2_E / math.sqrt(HEAD_DIM_B)
    q_idx = lax.broadcasted_iota(jnp.int32, (2 * tq, tk), 0) & (tq - 1)
    pos_gap = pos0 + qi * tq + q_idx - lax.broadcasted_iota(jnp.int32, (2 * tq, tk), 1)
    n_vis = jnp.minimum((pos0 + (qi + 1) * tq - 2) // tk + 1, n_kblocks)
    n_full = jnp.minimum((pos0 + qi * tq) // tk, n_kblocks)
    n_masked = n_vis - n_full

    def kblock(i):
        return jnp.maximum(n_vis - 1 - i, 0)

    def rows_of(ref, i):
        return ref[0, pl.ds(pl.multiple_of(kblock(i) * tk, tk), tk), lanes]

    def score_block(i):
        y = lax.dot_general(qs_scr[...], rows_of(k_ref, i), (((1,), (1,)), ((), ())), preferred_element_type=F32)
        return y * y_scale

    def suffix_block(y, i, masked):
        sp = jnp.maximum(y, jnp.log(1.0 + jnp.exp2(jnp.minimum(y, 126.0))) * LOG2_E)
        if masked:
            vis = pos_gap > kblock(i) * tk
            sp = jnp.where(vis, sp, 0.0)
        cs = jnp.dot(sp.astype(BF16), u_ref[...], preferred_element_type=F32)
        t = jnp.minimum(y - cs, 0.0)
        if masked:
            t = jnp.where(vis, t, HIDDEN)
        return t, cs[:, 0:1]

    def weigh_block(t, c, i):
        return jnp.dot(jnp.exp2(t - c).astype(BF16), rows_of(v_ref, i), preferred_element_type=F32)

    def first_two(mask_second):
        t0, tot0 = suffix_block(score_block(0), 0, True)
        t1, tot1 = suffix_block(score_block(1), 1, mask_second)
        acc_scr[...] = weigh_block(t0, 0.0, 0) + weigh_block(t1, tot0, 1)
        c_scr[...] = tot0 + tot1

    @pl.when((n_vis >= 2) & (n_masked >= 2))
    def _():
        first_two(True)

    @pl.when((n_vis >= 2) & (n_masked < 2))
    def _():
        first_two(False)

    @pl.when(n_vis < 2)
    def _():
        t0, tot0 = suffix_block(score_block(0), 0, True)
        acc_scr[...] = weigh_block(t0, 0.0, 0)
        c_scr[...] = tot0

    def scores(i, dst):
        y_scr[dst] = score_block(i)

    def suffix(i, masked, src, dst):
        t_scr[dst], tot_scr[dst] = suffix_block(y_scr[src], i, masked)

    def weigh(i, src):
        c = c_scr[...]
        acc_scr[...] += weigh_block(t_scr[src], c, i)
        c_scr[...] = c + tot_scr[src]

    def trip(i, masked):
        weigh(i, 0)
        suffix(i + 1, masked, 0, 0)
        scores(i + 2, 0)

    def group(size):
        def body(i):
            for j in range(size):
                cur, nxt = j & 1, (j + 1) & 1
                scores(i + j + 2, nxt)
                suffix(i + j + 1, False, cur, nxt)
                weigh(i + j, cur)
        return body

    def run(start, stop, step, body):
        def cond(i):
            return (i < stop) & (jnp.min(c_scr[...]) < SKIP_BITS)

        def step_body(i):
            body(i)
            return i + step

        lax.while_loop(cond, step_body, start)

    @pl.when((n_vis > 2) & (jnp.min(c_scr[...]) < SKIP_BITS))
    def _():
        scores(2, 0)
        suffix(2, True, 0, 0)
        scores(3, 0)
        n_mt = jnp.maximum(n_masked - 1, 2)
        first_pair = n_mt + ((n_vis - n_mt) & 1)
        first_group = first_pair + ((n_vis - first_pair) % ATTN_UNROLL)
        run(2, n_mt, 1, lambda i: trip(i, True))
        run(n_mt, first_pair, 1, lambda i: trip(i, False))
        run(first_pair, first_group, 2, group(2))
        run(first_group, n_vis, ATTN_UNROLL, group(ATTN_UNROLL))

    lane_o = lax.broadcasted_iota(jnp.int32, (tq, LANE), 1)
    o_ref[0, :, lanes] = jnp.where(lane_o < HEAD_DIM_B, acc_scr[0:tq, :], acc_scr[tq:2 * tq, :]).astype(o_ref.dtype)


def stick_breaking(q, k, v, u, pos0, tq):
    b, l, hd = q.shape
    kp = k.shape[1]
    tk = ATTN_TK
    width = ATTN_PAIRS * LANE
    assert tq & (tq - 1) == 0 and l % tq == 0 and kp % tk == 0 and hd % width == 0
    return pl.pallas_call(
        functools.partial(_attn_kernel, n_pairs=ATTN_PAIRS, tq=tq, tk=tk, pos0=pos0, n_kblocks=kp // tk),
        grid=(b, hd // width, l // tq),
        in_specs=[pl.BlockSpec((1, tq, width), lambda bi, p, i: (bi, i, p)),
                  pl.BlockSpec((1, kp, width), lambda bi, p, i: (bi, 0, p)),
                  pl.BlockSpec((1, kp, width), lambda bi, p, i: (bi, 0, p)),
                  pl.BlockSpec(u.shape, lambda bi, p, i: (0, 0))],
        out_specs=pl.BlockSpec((1, tq, width), lambda bi, p, i: (bi, i, p)),
        out_shape=jax.ShapeDtypeStruct((b, l, hd), BF16),
        scratch_shapes=[pltpu.VMEM((2 * tq, LANE), BF16), pltpu.VMEM((2 * tq, LANE), F32),
                        pltpu.VMEM((2 * tq, 1), F32), pltpu.VMEM((2, 2 * tq, tk), F32),
                        pltpu.VMEM((2, 2 * tq, tk), F32), pltpu.VMEM((2, 2 * tq, 1), F32)],
        compiler_params=_cparams(("parallel", "parallel", "arbitrary")),
        name="stick_breaking",
    )(q, k, v, u)


class _Path:
    def __init__(self, x3, mods, kv_mod, conv_state, ssm_state, past_k, past_v, p):
        self.b, self.l, self.d = x3.shape
        self.t = self.b * self.l
        self.x = x3.reshape(self.t, self.d)
        self.mods, self.kv_mod, self.p = mods, kv_mod, p
        self.conv_state, self.ssm_state, self.past_k, self.past_v = conv_state, ssm_state, past_k, past_v
        self.per_token = self.l < TOKEN_TILE
        self.rows_per_group = min(TOKEN_TILE, self.t) if self.per_token else self.l
        self.pos0 = past_k.shape[1]
        self.conv_out, self.ssm_out = [], []

    def mod3(self, v):
        if self.per_token:
            tile = min(TOKEN_TILE, self.t)
            return jnp.repeat(v, self.l, axis=0).reshape(self.t // tile, tile, self.d)
        return v[:, None, :]

    def mix_and_route(self, layer):
        p, b, l, d, t, rpg, mod3 = self.p, self.b, self.l, self.d, self.t, self.rows_per_group, self.mod3
        sh1, sc1, g1, sh2, sc2, self.g2 = jnp.split(self.mods[layer], 6, axis=-1)
        nw1 = p['mix_norm_w'][layer][None, :]
        x = self.x
        if layer < N_A_LAYERS:
            i = layer
            z, xbc, dt = norm_mod_matmul(x, nw1, mod3(sh1), mod3(sc1), [p['w_z'][i], p['w_xbc'][i], p['w_dt'][i]],
                                         [BF16, BF16, F32], rpg)
            xbc3 = xbc.reshape(b, l, CONV_DIM)
            self.conv_out.append(xbc3[:, l - (CONV_W - 1):].astype(F32))
            nc = max(l // SSD_Q, 1)
            if l < SSD_Q:
                pad = lambda a: jnp.pad(a.reshape(b, l, -1), ((0, 0), (0, SSD_Q - l), (0, 0))).reshape(b * SSD_Q, -1)
                z, xbc, dt = pad(z), pad(xbc), pad(dt)
            ic = jnp.pad(self.conv_state[i].astype(F32), ((0, 0), (8 - (CONV_W - 1), 0), (0, 0)))
            iss = self.ssm_state[i].astype(F32).transpose(0, 3, 1, 2).reshape(b, D_STATE, D_INNER)
            gm, s_fin = ssd_mixer(z, xbc, dt, p['a_conv_w'][i], p['a_conv_b'][i][None, :], p['dt_bias'][i],
                                  p['a_log'][i], p['d_x'][i], p['a_norm_w'][i][None, :], p['e_mat'], p['tri'], p['shift'],
                                  ic, iss, b, nc, min(l, SSD_Q))
            if l < SSD_Q:
                gm = gm.reshape(b, SSD_Q, D_INNER)[:, :l].reshape(t, D_INNER)
            self.ssm_out.append(s_fin.reshape(b, D_STATE, N_HEADS_A, HEAD_DIM_A).transpose(0, 2, 3, 1))
            x = matmul_residual(gm, p['a_out_proj'][i], x, mod3(g1), rpg)
        else:
            j = layer - N_A_LAYERS
            (qp,) = norm_mod_matmul(x, nw1, mod3(sh1), mod3(sc1), [p['b_w_q'][j]], [BF16], rpg)
            o = stick_breaking(qp.reshape(b, l, d), self.kb, self.vb, p['u_tri'], self.pos0, min(l, 256))
            x = matmul_residual(o.reshape(t, d), p['b_w_o'][j], x, mod3(g1), rpg)
        self.x = x
        self.moe = moe_route(x, p['ffn_norm_w'][layer][None, :], mod3(sh2), mod3(sc2), p['router_wt'][layer],
                             p['router_b'][layer], p['su_tri'], p['sl_tri'], rpg)

    def experts(self, layer):
        p = self.p
        self.ysg = moe_experts(self.moe, p['moe_w1'], p['moe_b1'], p['moe_w2'], p['moe_b2'], layer)

    def combine(self, layer):
        p, b, l, d, rpg, mod3 = self.p, self.b, self.l, self.d, self.rows_per_group, self.mod3
        self.x = moe_combine(self.ysg, self.moe['gates'], self.moe['pos'], self.x, mod3(self.g2), rpg)
        if layer == N_A_LAYERS - 1:
            sh_kv, sc_kv = jnp.split(self.kv_mod, 2, axis=-1)
            self.k_new, self.v_new = norm_mod_matmul(self.x, p['kv_norm_w'][None, :], mod3(sh_kv), mod3(sc_kv),
                                                     [p['w_k'], p['w_v']], [F32, F32], rpg)
            k_all = jnp.concatenate([self.past_k.reshape(b, self.pos0, d), self.k_new.reshape(b, l, d)], axis=1)
            v_all = jnp.concatenate([self.past_v.reshape(b, self.pos0, d), self.v_new.reshape(b, l, d)], axis=1)
            kpad = (-k_all.shape[1]) % ATTN_TK
            self.kb = jnp.pad(k_all, ((0, 0), (0, kpad), (0, 0))).astype(BF16)
            self.vb = jnp.pad(v_all, ((0, 0), (0, kpad), (0, 0))).astype(BF16)

    def finish(self):
        b, l, d = self.b, self.l, self.d
        y = final_rmsnorm(self.x, self.p['final_norm_w'][None, :])
        return (y.reshape(b, l, d), jnp.stack(self.conv_out), jnp.stack(self.ssm_out),
                self.k_new.reshape(b, l, N_HEADS_B, HEAD_DIM_B), self.v_new.reshape(b, l, N_HEADS_B, HEAD_DIM_B))


def _run_paths(paths):
    for layer in range(DEPTH):
        for step in (_Path.mix_and_route, _Path.experts, _Path.combine):
            for path in paths:
                step(path, layer)
    return [path.finish() for path in paths]


def kernel(x_prompt, x_sample, state_conv, state_ssm, cache_k, cache_v, c_prompt, c_sample, mod_w, mod_b,
           mix_norm_w, ffn_norm_w, a_in_proj, a_conv_w, a_conv_b, a_dt_bias, a_A_log, a_D, a_norm_w, a_out_proj,
           kv_mod_w, kv_mod_b, kv_norm_w, w_kv, b_w_q, b_w_o, router_w, router_b, moe_w1, moe_b1, moe_w2, moe_b2,
           final_norm_w):
    bp = x_prompt.shape[0]
    hb = N_HEADS_B * HEAD_DIM_B
    pad_heads = lambda a: jnp.pad(a, ((0, 0), (0, LANE - N_HEADS_A)))[:, None, :]
    head_of_lane = jnp.arange(D_INNER) // HEAD_DIM_A
    p = dict(
        mix_norm_w=mix_norm_w, ffn_norm_w=ffn_norm_w, kv_norm_w=kv_norm_w, final_norm_w=final_norm_w,
        w_z=a_in_proj[:, :, :D_INNER].astype(BF16),
        w_xbc=a_in_proj[:, :, D_INNER:D_INNER + CONV_DIM].astype(BF16),
        w_dt=jnp.pad(a_in_proj[:, :, D_INNER + CONV_DIM:], ((0, 0), (0, 0), (0, LANE - N_HEADS_A))).astype(BF16),
        a_conv_w=a_conv_w, a_conv_b=a_conv_b, dt_bias=pad_heads(a_dt_bias), a_log=pad_heads(a_A_log),
        d_x=a_D[:, head_of_lane][:, None, :], a_norm_w=a_norm_w, a_out_proj=a_out_proj.astype(BF16),
        e_mat=jnp.tile((jnp.arange(LANE)[:, None] == head_of_lane[None, :]).astype(BF16), (3, 1)),
        tri=jnp.tile((jnp.arange(SSD_Q)[:, None] >= jnp.arange(SSD_Q)[None, :]).astype(BF16), (1, 3)),
        shift=jnp.stack([(jnp.arange(SSD_Q)[:, None] + tap - (CONV_W - 1) == jnp.arange(SSD_Q)[None, :])
                         for tap in range(CONV_W - 1)]).astype(BF16),
        u_tri=(jnp.arange(ATTN_TK)[:, None] >= jnp.arange(ATTN_TK)[None, :]).astype(BF16),
        w_k=w_kv[:, :hb].astype(BF16), w_v=w_kv[:, hb:].astype(BF16),
        b_w_q=b_w_q.astype(BF16), b_w_o=b_w_o.astype(BF16),
        router_wt=router_w.transpose(0, 2, 1), router_b=router_b[:, :, None],
        su_tri=(jnp.arange(TOKEN_TILE)[:, None] < jnp.arange(TOKEN_TILE)[None, :]).astype(BF16),
        sl_tri=(jnp.arange(N_EXPERTS)[:, None] > jnp.arange(N_EXPERTS)[None, :]).astype(BF16),
        moe_w1=moe_w1, moe_b1=moe_b1[:, :, None, :], moe_w2=moe_w2, moe_b2=moe_b2[:, :, None, :],
    )
    c_all = jnp.concatenate([c_prompt, c_sample], axis=0)
    mods = mod_vectors(c_all, mod_w, mod_b[:, None, :])
    kv_mod = mod_vectors(c_all, kv_mod_w[None], kv_mod_b[None, None, :])[0]

    bg = bp // PROMPT_GROUPS
    conv0 = jnp.zeros((N_A_LAYERS, bg, CONV_W - 1, CONV_DIM), F32)
    ssm0 = jnp.zeros((N_A_LAYERS, bg, N_HEADS_A, HEAD_DIM_A, D_STATE), F32)
    kv0 = jnp.zeros((bg, 0, N_HEADS_B, HEAD_DIM_B), F32)
    paths = [_Path(x_prompt[g * bg:(g + 1) * bg], mods[:, g * bg:(g + 1) * bg], kv_mod[g * bg:(g + 1) * bg],
                   conv0, ssm0, kv0, kv0, p) for g in range(PROMPT_GROUPS)]
    paths.append(_Path(x_sample, mods[:, bp:], kv_mod[bp:], state_conv, state_ssm, cache_k, cache_v, p))
    *groups, out_s = _run_paths(paths)
    out_p = tuple(jnp.concatenate([grp[i] for grp in groups], axis=ax) for i, ax in enumerate((0, 1, 1, 0, 0)))
    return (out_p[0], out_s[0]) + out_p[1:] + out_s[1:]
```

```python
import functools
import math

import jax
import jax.numpy as jnp
from jax import lax
from jax.experimental import pallas as pl
from jax.experimental.pallas import tpu as pltpu

F32 = jnp.float32
BF16 = jnp.bfloat16
HIGHEST = lax.Precision.HIGHEST

D_MODEL = 1024
DEPTH = 4
N_A_LAYERS = 2
D_INNER = 2048
HEAD_DIM_A = 64
N_HEADS_A = 32
D_STATE = 128
N_GROUPS = 4
GROUP_W = D_INNER // N_GROUPS
CONV_W = 4
CONV_DIM = D_INNER + 2 * N_GROUPS * D_STATE
HEAD_DIM_B = 64
N_HEADS_B = 16
N_EXPERTS = 32
TOP_K = 4
D_FF = 1024
SWIGLU_ALPHA = 1.702
SWIGLU_LIMIT = 7.0
NORM_EPS = 1e-6
LOG2_E = 1.4426950408889634
HIDDEN = -1e30
SKIP_BITS = 160.0

LANE = 128
SSD_Q = 64
TOKEN_TILE = 512
MOE_TILE = 512
ATTN_TK = 256
ATTN_UNROLL = 4
ATTN_PAIRS = 4
PROMPT_GROUPS = 1
VMEM_LIMIT = 56 * 1024 * 1024


def _cparams(sem):
    return pltpu.CompilerParams(dimension_semantics=sem, vmem_limit_bytes=VMEM_LIMIT)


def _softplus(x):
    return jnp.maximum(x, 0.0) + jnp.log1p(jnp.exp(-jnp.abs(x)))


def _silu(x):
    return x * jax.nn.sigmoid(x)


def _norm_mod(x, nw, sh, sc):
    ms = jnp.mean(x * x, axis=-1, keepdims=True)
    return (x * lax.rsqrt(ms + NORM_EPS) * nw) * (1.0 + sc) + sh


def _mod_spec(mod3, tile, rows_per_group):
    steps = max(rows_per_group // tile, 1)
    g = mod3.shape[1]
    return pl.BlockSpec((1, g, mod3.shape[2]), lambda i: (i // steps, 0, 0))


def _mod_kernel(c_ref, w_ref, b_ref, o_ref):
    cs = _silu(c_ref[...])
    o_ref[0] = jnp.dot(cs, w_ref[0], precision=HIGHEST, preferred_element_type=F32) + b_ref[0]


def mod_vectors(c, w, b):
    nl, d, n = w.shape
    tn = 1024
    return pl.pallas_call(
        _mod_kernel,
        grid=(nl, n // tn),
        in_specs=[pl.BlockSpec(c.shape, lambda l, j: (0, 0)),
                  pl.BlockSpec((1, d, tn), lambda l, j: (l, 0, j)),
                  pl.BlockSpec((1, 1, tn), lambda l, j: (l, 0, j))],
        out_specs=pl.BlockSpec((1, c.shape[0], tn), lambda l, j: (l, 0, j)),
        out_shape=jax.ShapeDtypeStruct((nl, c.shape[0], n), F32),
        compiler_params=_cparams(("arbitrary", "arbitrary")),
        name="mod_vectors",
    )(c, w, b)


def _nmm_kernel(x_ref, nw_ref, sh_ref, sc_ref, *refs, n_w):
    h = _norm_mod(x_ref[...], nw_ref[...], sh_ref[0], sc_ref[0]).astype(BF16)
    for w_ref, o_ref in zip(refs[:n_w], refs[n_w:]):
        o_ref[...] = jnp.dot(h, w_ref[...], preferred_element_type=F32).astype(o_ref.dtype)


def norm_mod_matmul(x, nw, sh3, sc3, ws, out_dtypes, rows_per_group):
    t, d = x.shape
    tm = min(TOKEN_TILE, t)
    in_specs = [pl.BlockSpec((tm, d), lambda i: (i, 0)),
                pl.BlockSpec((1, d), lambda i: (0, 0)),
                _mod_spec(sh3, tm, rows_per_group), _mod_spec(sc3, tm, rows_per_group)]
    in_specs += [pl.BlockSpec(w.shape, lambda i: (0, 0), pipeline_mode=pl.Buffered(1)) for w in ws]
    return pl.pallas_call(
        functools.partial(_nmm_kernel, n_w=len(ws)),
        grid=(t // tm,),
        in_specs=in_specs,
        out_specs=[pl.BlockSpec((tm, w.shape[1]), lambda i: (i, 0)) for w in ws],
        out_shape=[jax.ShapeDtypeStruct((t, w.shape[1]), dt) for w, dt in zip(ws, out_dtypes)],
        compiler_params=_cparams(("parallel",)),
        name="norm_mod_matmul",
    )(x, nw, sh3, sc3, *ws)


def _mmres_kernel(a_ref, w_ref, x_ref, g_ref, o_ref):
    acc = jnp.dot(a_ref[...], w_ref[...], preferred_element_type=F32)
    o_ref[...] = x_ref[...] + g_ref[0] * acc


def matmul_residual(a, w, x, g3, rows_per_group):
    t, d = x.shape
    tm = min(TOKEN_TILE, t)
    return pl.pallas_call(
        _mmres_kernel,
        grid=(t // tm,),
        in_specs=[pl.BlockSpec((tm, a.shape[1]), lambda i: (i, 0)),
                  pl.BlockSpec(w.shape, lambda i: (0, 0), pipeline_mode=pl.Buffered(1)),
                  pl.BlockSpec((tm, d), lambda i: (i, 0)),
                  _mod_spec(g3, tm, rows_per_group)],
        out_specs=pl.BlockSpec((tm, d), lambda i: (i, 0)),
        out_shape=jax.ShapeDtypeStruct((t, d), F32),
        compiler_params=_cparams(("parallel",)),
        name="matmul_residual",
    )(a, w, x, g3)


def _rms_kernel(x_ref, nw_ref, o_ref):
    x = x_ref[...]
    ms = jnp.mean(x * x, axis=-1, keepdims=True)
    o_ref[...] = x * lax.rsqrt(ms + NORM_EPS) * nw_ref[...]


def final_rmsnorm(x, nw):
    t, d = x.shape
    tm = min(TOKEN_TILE, t)
    return pl.pallas_call(
        _rms_kernel,
        grid=(t // tm,),
        in_specs=[pl.BlockSpec((tm, d), lambda i: (i, 0)), pl.BlockSpec((1, d), lambda i: (0, 0))],
        out_specs=pl.BlockSpec((tm, d), lambda i: (i, 0)),
        out_shape=jax.ShapeDtypeStruct((t, d), F32),
        compiler_params=_cparams(("parallel",)),
        name="final_rmsnorm",
    )(x, nw)


def _router_kernel(x_ref, nw_ref, sh_ref, sc_ref, rwt_ref, rb_ref, su_ref, sl_ref, hs_ref, g_ref, p_ref, cnt_ref,
                   run_ref, tot_ref, run_scr):
    @pl.when(pl.program_id(0) == 0)
    def _():
        run_scr[...] = jnp.zeros_like(run_scr)

    h = _norm_mod(x_ref[...], nw_ref[...], sh_ref[0], sc_ref[0])
    logits = lax.dot_general(rwt_ref[...], h, (((1,), (1,)), ((), ())),
                             precision=HIGHEST, preferred_element_type=F32) + rb_ref[...]
    eid = lax.broadcasted_iota(jnp.int32, logits.shape, 0)
    vals, idxs, hits = [], [], []
    for _ in range(TOP_K):
        m = jnp.max(logits, axis=0, keepdims=True)
        idx = jnp.min(jnp.where(logits == m, eid, N_EXPERTS), axis=0, keepdims=True)
        hit = eid == idx
        vals.append(m)
        idxs.append(idx)
        hits.append(hit)
        logits = jnp.where(hit, -jnp.inf, logits)
    ex = [jnp.exp(v - vals[0]) for v in vals]
    den = ex[0] + ex[1] + ex[2] + ex[3]
    cnt = (hits[0] | hits[1] | hits[2] | hits[3]).astype(F32)
    cnt_b = cnt.astype(BF16)
    earlier = jnp.dot(cnt_b, su_ref[...], preferred_element_type=F32)
    lower = jnp.dot(sl_ref[...], cnt_b, preferred_element_type=F32)
    cnt_tile = jnp.sum(cnt, axis=1, keepdims=True)
    where_to = earlier + jnp.sum(lower, axis=1, keepdims=True)
    tm = x_ref.shape[0]
    slot = lax.broadcasted_iota(jnp.int32, (TOP_K * tm, tm), 0)
    place = None
    for k in range(TOP_K):
        pos = jnp.sum(jnp.where(hits[k], where_to, 0.0), axis=0, keepdims=True).astype(jnp.int32)
        g_ref[k:k + 1, :] = ex[k] / den
        p_ref[k:k + 1, :] = pos
        place = (slot == pos) if place is None else place | (slot == pos)
    hs_ref[...] = jnp.dot(place.astype(BF16), h.astype(BF16), preferred_element_type=F32).astype(BF16)
    cnt_ref[0] = cnt_tile
    run_ref[0] = run_scr[...]
    run_scr[...] += cnt_tile
    tot_ref[...] = run_scr[...]


def moe_router(x, nw, sh3, sc3, rwt, rb, su, sl, rows_per_group):
    t, d = x.shape
    tm = min(TOKEN_TILE, t)
    n_tiles = t // tm
    return pl.pallas_call(
        _router_kernel,
        grid=(n_tiles,),
        in_specs=[pl.BlockSpec((tm, d), lambda i: (i, 0)),
                  pl.BlockSpec((1, d), lambda i: (0, 0)),
                  _mod_spec(sh3, tm, rows_per_group), _mod_spec(sc3, tm, rows_per_group),
                  pl.BlockSpec(rwt.shape, lambda i: (0, 0)),
                  pl.BlockSpec(rb.shape, lambda i: (0, 0)),
                  pl.BlockSpec(su.shape, lambda i: (0, 0)),
                  pl.BlockSpec(sl.shape, lambda i: (0, 0))],
        out_specs=[pl.BlockSpec((TOP_K * tm, d), lambda i: (i, 0)),
                   pl.BlockSpec((TOP_K, tm), lambda i: (0, i)),
                   pl.BlockSpec((TOP_K, tm), lambda i: (0, i)),
                   pl.BlockSpec((1, N_EXPERTS, 1), lambda i: (i, 0, 0)),
                   pl.BlockSpec((1, N_EXPERTS, 1), lambda i: (i, 0, 0)),
                   pl.BlockSpec((N_EXPERTS, 1), lambda i: (0, 0))],
        out_shape=[jax.ShapeDtypeStruct((TOP_K * t, d), BF16),
                   jax.ShapeDtypeStruct((TOP_K, t), F32),
                   jax.ShapeDtypeStruct((TOP_K, t), jnp.int32),
                   jax.ShapeDtypeStruct((n_tiles, N_EXPERTS, 1), F32),
                   jax.ShapeDtypeStruct((n_tiles, N_EXPERTS, 1), F32),
                   jax.ShapeDtypeStruct((N_EXPERTS, 1), F32)],
        scratch_shapes=[pltpu.VMEM((N_EXPERTS, 1), F32)],
        compiler_params=_cparams(("arbitrary",)),
        name="moe_router",
    )(x, nw, sh3, sc3, rwt, rb, su, sl)


def _ffn_kernel(be_ref, nu_ref, xs_ref, w1_ref, b1_ref, w2_ref, b2_ref, o_ref, w1_scr, w2_scr):
    i = pl.program_id(0)

    @pl.when(i < nu_ref[0])
    def _():
        @pl.when((i == 0) | (be_ref[i] != be_ref[jnp.maximum(i - 1, 0)]))
        def _():
            w1_scr[...] = w1_ref[0, 0].astype(BF16)
            w2_scr[...] = w2_ref[0, 0].astype(BF16)

        hid = jnp.dot(xs_ref[...], w1_scr[...], preferred_element_type=F32) + b1_ref[0, 0]
        glu = jnp.minimum(hid[:, :D_FF], SWIGLU_LIMIT)
        lin = jnp.clip(hid[:, D_FF:], -SWIGLU_LIMIT, SWIGLU_LIMIT)
        act = glu * jax.nn.sigmoid(SWIGLU_ALPHA * glu) * (lin + 1.0)
        out = jnp.dot(act.astype(BF16), w2_scr[...], preferred_element_type=F32) + b2_ref[0, 0]
        o_ref[...] = out.astype(o_ref.dtype)


def moe_ffn_blocks(xs, block_e, n_used, w1, b1, w2, b2, layer):
    n_rows, d = xs.shape
    tm = MOE_TILE
    grid_spec = pltpu.PrefetchScalarGridSpec(
        num_scalar_prefetch=2,
        grid=(n_rows // tm,),
        in_specs=[pl.BlockSpec((tm, d), lambda i, be, nu: (i, 0)),
                  pl.BlockSpec((1, 1, d, 2 * D_FF), lambda i, be, nu: (layer, be[i], 0, 0)),
                  pl.BlockSpec((1, 1, 1, 2 * D_FF), lambda i, be, nu: (layer, be[i], 0, 0)),
                  pl.BlockSpec((1, 1, D_FF, d), lambda i, be, nu: (layer, be[i], 0, 0)),
                  pl.BlockSpec((1, 1, 1, d), lambda i, be, nu: (layer, be[i], 0, 0))],
        out_specs=pl.BlockSpec((tm, d), lambda i, be, nu: (i, 0)),
        scratch_shapes=[pltpu.VMEM((d, 2 * D_FF), BF16), pltpu.VMEM((D_FF, d), BF16)],
    )
    return pl.pallas_call(
        _ffn_kernel,
        grid_spec=grid_spec,
        out_shape=jax.ShapeDtypeStruct((n_rows, d), BF16),
        compiler_params=_cparams(("arbitrary",)),
        name="moe_ffn",
    )(block_e, n_used, xs, w1, b1, w2, b2)


def _combine_kernel(y_ref, gt_ref, p_ref, x_ref, g_ref, o_ref):
    tm = x_ref.shape[0]
    eye = lax.broadcasted_iota(jnp.int32, (tm, tm), 0) == lax.broadcasted_iota(jnp.int32, (tm, tm), 1)
    col = lax.broadcasted_iota(jnp.int32, (tm, TOP_K * tm), 1)
    pick = jnp.zeros((tm, TOP_K * tm), F32)
    for k in range(TOP_K):
        g_col = jnp.sum(jnp.where(eye, gt_ref[k:k + 1, :], 0.0), axis=1, keepdims=True)
        p_col = jnp.sum(jnp.where(eye, p_ref[k:k + 1, :], 0), axis=1, keepdims=True)
        pick = jnp.where(col == p_col, g_col, pick)
    ff = jnp.dot(pick.astype(BF16), y_ref[...], preferred_element_type=F32)
    o_ref[...] = x_ref[...] + g_ref[0] * ff


def moe_combine(ysg, gates, pos, x, g3, rows_per_group):
    t, d = x.shape
    tm = min(TOKEN_TILE, t)
    return pl.pallas_call(
        _combine_kernel,
        grid=(t // tm,),
        in_specs=[pl.BlockSpec((TOP_K * tm, d), lambda i: (i, 0)),
                  pl.BlockSpec((TOP_K, tm), lambda i: (0, i)),
                  pl.BlockSpec((TOP_K, tm), lambda i: (0, i)),
                  pl.BlockSpec((tm, d), lambda i: (i, 0)),
                  _mod_spec(g3, tm, rows_per_group)],
        out_specs=pl.BlockSpec((tm, d), lambda i: (i, 0)),
        out_shape=jax.ShapeDtypeStruct((t, d), F32),
        compiler_params=_cparams(("parallel",)),
        name="moe_combine",
    )(ysg, gates, pos, x, g3)


def _take_rows(a, idx):
    return a.at[idx].get(mode="promise_in_bounds")


def moe_route(x, nw, sh3, sc3, rwt, rb, su, sl, rows_per_group):
    t, d = x.shape
    tt = min(TOKEN_TILE, t)
    n_tiles, tl = t // tt, TOP_K * tt
    hs, gates, pos, cnt, run, tot = moe_router(x, nw, sh3, sc3, rwt, rb, su, sl, rows_per_group)
    tm = MOE_TILE
    n_blocks = (t * TOP_K + N_EXPERTS * (tm - 1) + tm - 1) // tm
    i32 = jnp.int32
    counts = tot[:, 0].astype(i32)
    padded = (counts + tm - 1) // tm * tm
    pend = jnp.cumsum(padded)
    pstart = (pend - padded).astype(i32)
    block_e = jnp.minimum(jnp.sum((pend[None, :] <= (jnp.arange(n_blocks) * tm)[:, None]).astype(i32), axis=1),
                          N_EXPERTS - 1)
    n_used = (pend[-1] // tm).astype(i32).reshape(1)
    cnt = cnt[:, :, 0].astype(i32)
    run = run[:, :, 0].astype(i32)
    eoff = jnp.cumsum(cnt, axis=1) - cnt
    tiles = jnp.arange(n_tiles, dtype=i32)
    in_e = (jnp.arange(n_blocks, dtype=i32) * tm - pstart[block_e])[:, None] + jnp.arange(tm, dtype=i32)[None, :]
    run_e = run.T[block_e]
    tile = jnp.sum((run_e[:, None, :] <= in_e[:, :, None]).astype(i32), axis=-1) - 1
    shift = jnp.sum(jnp.where(tiles[None, None, :] == tile[:, :, None], (eoff.T[block_e] - run_e)[:, None, :], 0),
                    axis=-1)
    spread = (jnp.arange(n_blocks * tm, dtype=i32) % (TOP_K * t)).reshape(n_blocks, tm)
    src = jnp.where(in_e < counts[block_e][:, None], tile * tl + shift + in_e, spread)
    j = jnp.arange(tl, dtype=i32)
    e_j = jnp.sum(((eoff + cnt)[:, None, :] <= j[None, :, None]).astype(i32), axis=-1)
    base = pstart[None, :] + run - eoff
    back = jnp.sum(jnp.where(jnp.arange(N_EXPERTS, dtype=i32)[None, None, :] == e_j[:, :, None], base[:, None, :], 0),
                   axis=-1) + j[None, :]
    return dict(xs=_take_rows(hs, src.reshape(-1)), block_e=block_e, n_used=n_used, back=back.reshape(-1),
                gates=gates, pos=pos)


def moe_experts(m, w1, b1, w2, b2, layer):
    ys = moe_ffn_blocks(m['xs'], m['block_e'], m['n_used'], w1, b1, w2, b2, layer)
    return _take_rows(ys, m['back'])


def _split3(x):
    hi = x.astype(BF16)
    r1 = x - hi.astype(F32)
    mid = r1.astype(BF16)
    lo = (r1 - mid.astype(F32)).astype(BF16)
    return hi, mid, lo


def _ssd_kernel(z_ref, xbc_ref, dt_ref, cw_ref, cb_ref, dtb_ref, alog_ref, dx_ref, nw_ref, e_ref, tri_ref, shift_ref,
                ic_ref, is_ref, g_ref, so_ref, tail_scr, s_scr, y_scr, *, valid_rows):
    q = SSD_Q
    c = pl.program_id(1)

    @pl.when(c == 0)
    def _():
        tail_scr[...] = ic_ref[0]
        s_scr[...] = is_ref[0]

    xb = xbc_ref[...]
    acc = cb_ref[...] + xb.astype(F32) * cw_ref[CONV_W - 1:CONV_W, :]
    for tap in range(CONV_W - 1):
        shifted = jnp.dot(shift_ref[tap], xb, preferred_element_type=F32)
        acc = acc + shifted * cw_ref[tap:tap + 1, :]
    t0, t1, t2 = (tail_scr[8 - (CONV_W - 1) + j:8 - (CONV_W - 1) + j + 1, :] for j in range(CONV_W - 1))
    w0, w1, w2 = (cw_ref[j:j + 1, :] for j in range(CONV_W - 1))
    row8 = lax.broadcasted_iota(jnp.int32, (8, CONV_DIM), 0)
    head = jnp.where(row8 == 0, t0 * w0 + t1 * w1 + t2 * w2,
                     jnp.where(row8 == 1, t1 * w0 + t2 * w1, jnp.where(row8 == 2, t2 * w0, 0.0)))
    tail_scr[...] = xbc_ref[q - 16:q, :].astype(F32)[8:16, :]
    xc = _silu(jnp.concatenate([acc[0:8, :] + head, acc[8:q, :]], axis=0))
    xs = xc[:, :D_INNER]

    dt = _softplus(dt_ref[...] + dtb_ref[...])
    if valid_rows < q:
        rows = lax.broadcasted_iota(jnp.int32, dt.shape, 0)
        dt = jnp.where(rows < valid_rows, dt, 0.0)
    a = dt * (-jnp.exp(alog_ref[...]))
    a_cs = jnp.dot(tri_ref[...], jnp.concatenate(_split3(a), axis=0), preferred_element_type=F32)
    dtx = jnp.dot(jnp.concatenate(_split3(dt), axis=1), e_ref[...], preferred_element_type=F32)
    acsx = jnp.dot(jnp.concatenate(_split3(a_cs), axis=1), e_ref[...], preferred_element_type=F32)

    row = lax.broadcasted_iota(jnp.int32, (q, D_INNER), 0)
    col = lax.broadcasted_iota(jnp.int32, (q, D_INNER), 1) & (HEAD_DIM_A - 1)
    acs_row = jnp.sum(jnp.where(row == col, acsx, 0.0), axis=0, keepdims=True)
    decay_in = jnp.where(row >= col, jnp.exp(jnp.minimum(acsx - acs_row, 0.0)), 0.0)
    last = acsx[q - 1:q, :]
    e_in = jnp.exp(acsx)
    e_last = jnp.exp(last)
    xd = xs * dtx
    xdd = xd * jnp.exp(last - acsx)
    lane = lax.broadcasted_iota(jnp.int32, (q, LANE), 1)
    lo = lane < HEAD_DIM_A

    for g in range(N_GROUPS):
        gsl = slice(g * GROUP_W, (g + 1) * GROUP_W)
        b_off = D_INNER + g * D_STATE
        c_off = D_INNER + N_GROUPS * D_STATE + g * D_STATE
        bg = xc[:, b_off:b_off + D_STATE]
        cgb = xc[:, c_off:c_off + D_STATE].astype(BF16)
        bgb = bg.astype(BF16)
        b2 = jnp.concatenate([bgb, bgb], axis=0)
        cb2 = lax.dot_general(cgb, b2, (((1,), (1,)), ((), ())), preferred_element_type=F32)
        s_g = s_scr[:, gsl]
        y_off = jnp.dot(cgb, s_g.astype(BF16), preferred_element_type=F32) * e_in[:, gsl]
        for pp in range(GROUP_W // LANE):
            sl = slice(g * GROUP_W + pp * LANE, g * GROUP_W + (pp + 1) * LANE)
            m = (decay_in[:, sl] * cb2).astype(BF16)
            xp = xd[:, sl]
            bd = jnp.concatenate([jnp.where(lo, xp, 0.0), jnp.where(lo, 0.0, xp)], axis=0).astype(BF16)
            y_diag = jnp.dot(m, bd, preferred_element_type=F32)
            y_scr[:, sl] = y_diag + y_off[:, pp * LANE:(pp + 1) * LANE] + xs[:, sl] * dx_ref[:, sl]
        s_new = jnp.dot(bg.T.astype(BF16), xdd[:, gsl].astype(BF16), preferred_element_type=F32)
        s_scr[:, gsl] = s_g * e_last[:, gsl] + s_new

    for g in range(N_GROUPS):
        gsl = slice(g * GROUP_W, (g + 1) * GROUP_W)
        gz = y_scr[:, gsl] * _silu(z_ref[:, gsl].astype(F32))
        ms = jnp.mean(gz * gz, axis=-1, keepdims=True)
        g_ref[:, gsl] = (gz * lax.rsqrt(ms + NORM_EPS) * nw_ref[:, gsl]).astype(g_ref.dtype)

    @pl.when(c == pl.num_programs(1) - 1)
    def _():
        so_ref[0] = s_scr[...]


def ssd_mixer(z, xbc, dt, cw, cb, dtb, alog, dx, nw, e_mat, tri, shift, init_conv, init_ssm, nb, nc, valid_rows):
    q = SSD_Q
    const = lambda a: pl.BlockSpec(a.shape, lambda b, c: (0,) * a.ndim)
    return pl.pallas_call(
        functools.partial(_ssd_kernel, valid_rows=valid_rows),
        grid=(nb, nc),
        in_specs=[pl.BlockSpec((q, D_INNER), lambda b, c: (b * nc + c, 0)),
                  pl.BlockSpec((q, CONV_DIM), lambda b, c: (b * nc + c, 0)),
                  pl.BlockSpec((q, LANE), lambda b, c: (b * nc + c, 0)),
                  const(cw), const(cb), const(dtb), const(alog), const(dx), const(nw), const(e_mat), const(tri),
                  const(shift),
                  pl.BlockSpec((1, 8, CONV_DIM), lambda b, c: (b, 0, 0)),
                  pl.BlockSpec((1, D_STATE, D_INNER), lambda b, c: (b, 0, 0))],
        out_specs=[pl.BlockSpec((q, D_INNER), lambda b, c: (b * nc + c, 0)),
                   pl.BlockSpec((1, D_STATE, D_INNER), lambda b, c: (b, 0, 0))],
        out_shape=[jax.ShapeDtypeStruct((nb * nc * q, D_INNER), BF16),
                   jax.ShapeDtypeStruct((nb, D_STATE, D_INNER), F32)],
        scratch_shapes=[pltpu.VMEM((8, CONV_DIM), F32),
                        pltpu.VMEM((D_STATE, D_INNER), F32),
                        pltpu.VMEM((q, D_INNER), F32)],
        compiler_params=_cparams(("parallel", "arbitrary")),
        name="ssd_mixer",
    )(z, xbc, dt, cw, cb, dtb, alog, dx, nw, e_mat, tri, shift, init_conv, init_ssm)


def _attn_kernel(*refs, n_pairs, **static):
    for pair in range(n_pairs):
        _attn_pair(slice(pair * LANE, (pair + 1) * LANE), *refs, **static)


def _attn_pair(lanes, q_ref, k_ref, v_ref, u_ref, o_ref, qs_scr, acc_scr, c_scr, y_scr, t_scr, tot_scr,
               *, tq, tk, pos0, n_kblocks):
    qi = pl.program_id(2)
    qv = q_ref[0, :, lanes]
    lane_q = lax.broadcasted_iota(jnp.int32, qv.shape, 1)
    zero_q = jnp.zeros_like(qv)
    qs_scr[0:tq, :] = jnp.where(lane_q < HEAD_DIM_B, qv, zero_q)
    qs_scr[tq:2 * tq, :] = jnp.where(lane_q < HEAD_DIM_B, zero_q, qv)
    y_scale = LOG2_E / math.sqrt(HEAD_DIM_B)
    q_idx = lax.broadcasted_iota(jnp.int32, (2 * tq, tk), 0) & (tq - 1)
    pos_gap = pos0 + qi * tq + q_idx - lax.broadcasted_iota(jnp.int32, (2 * tq, tk), 1)
    n_vis = jnp.minimum((pos0 + (qi + 1) * tq - 2) // tk + 1, n_kblocks)
    n_full = jnp.minimum((pos0 + qi * tq) // tk, n_kblocks)
    n_masked = n_vis - n_full

    def kblock(i):
        return jnp.maximum(n_vis - 1 - i, 0)

    def rows_of(ref, i):
        return ref[0, pl.ds(pl.multiple_of(kblock(i) * tk, tk), tk), lanes]

    def score_block(i):
        y = lax.dot_general(qs_scr[...], rows_of(k_ref, i), (((1,), (1,)), ((), ())), preferred_element_type=F32)
        return y * y_scale

    def suffix_block(y, i, masked):
        sp = jnp.maximum(y, jnp.log(1.0 + jnp.exp2(jnp.minimum(y, 126.0))) * LOG2_E)
        if masked:
            vis = pos_gap > kblock(i) * tk
            sp = jnp.where(vis, sp, 0.0)
        cs = jnp.dot(sp.astype(BF16), u_ref[...], preferred_element_type=F32)
        t = jnp.minimum(y - cs, 0.0)
        if masked:
            t = jnp.where(vis, t, HIDDEN)
        return t, cs[:, 0:1]

    def weigh_block(t, c, i):
        return jnp.dot(jnp.exp2(t - c).astype(BF16), rows_of(v_ref, i), preferred_element_type=F32)

    def first_two(mask_second):
        t0, tot0 = suffix_block(score_block(0), 0, True)
        t1, tot1 = suffix_block(score_block(1), 1, mask_second)
        acc_scr[...] = weigh_block(t0, 0.0, 0) + weigh_block(t1, tot0, 1)
        c_scr[...] = tot0 + tot1

    @pl.when((n_vis >= 2) & (n_masked >= 2))
    def _():
        first_two(True)

    @pl.when((n_vis >= 2) & (n_masked < 2))
    def _():
        first_two(False)

    @pl.when(n_vis < 2)
    def _():
        t0, tot0 = suffix_block(score_block(0), 0, True)
        acc_scr[...] = weigh_block(t0, 0.0, 0)
        c_scr[...] = tot0

    def scores(i, dst):
        y_scr[dst] = score_block(i)

    def suffix(i, masked, src, dst):
        t_scr[dst], tot_scr[dst] = suffix_block(y_scr[src], i, masked)

    def weigh(i, src):
        c = c_scr[...]
        acc_scr[...] += weigh_block(t_scr[src], c, i)
        c_scr[...] = c + tot_scr[src]

    def trip(i, masked):
        weigh(i, 0)
        suffix(i + 1, masked, 0, 0)
        scores(i + 2, 0)

    def group(size):
        def body(i):
            for j in range(size):
                cur, nxt = j & 1, (j + 1) & 1
                scores(i + j + 2, nxt)
                suffix(i + j + 1, False, cur, nxt)
                weigh(i + j, cur)
        return body

    def run(start, stop, step, body):
        def cond(i):
            return (i < stop) & (jnp.min(c_scr[...]) < SKIP_BITS)

        def step_body(i):
            body(i)
            return i + step

        lax.while_loop(cond, step_body, start)

    @pl.when((n_vis > 2) & (jnp.min(c_scr[...]) < SKIP_BITS))
    def _():
        scores(2, 0)
        suffix(2, True, 0, 0)
        scores(3, 0)
        n_mt = jnp.maximum(n_masked - 1, 2)
        first_pair = n_mt + ((n_vis - n_mt) & 1)
        first_group = first_pair + ((n_vis - first_pair) % ATTN_UNROLL)
        run(2, n_mt, 1, lambda i: trip(i, True))
        run(n_mt, first_pair, 1, lambda i: trip(i, False))
        run(first_pair, first_group, 2, group(2))
        run(first_group, n_vis, ATTN_UNROLL, group(ATTN_UNROLL))

    lane_o = lax.broadcasted_iota(jnp.int32, (tq, LANE), 1)
    o_ref[0, :, lanes] = jnp.where(lane_o < HEAD_DIM_B, acc_scr[0:tq, :], acc_scr[tq:2 * tq, :]).astype(o_ref.dtype)


def stick_breaking(q, k, v, u, pos0, tq):
    b, l, hd = q.shape
    kp = k.shape[1]
    tk = ATTN_TK
    width = ATTN_PAIRS * LANE
    assert tq & (tq - 1) == 0 and l % tq == 0 and kp % tk == 0 and hd % width == 0
    return pl.pallas_call(
        functools.partial(_attn_kernel, n_pairs=ATTN_PAIRS, tq=tq, tk=tk, pos0=pos0, n_kblocks=kp // tk),
        grid=(b, hd // width, l // tq),
        in_specs=[pl.BlockSpec((1, tq, width), lambda bi, p, i: (bi, i, p)),
                  pl.BlockSpec((1, kp, width), lambda bi, p, i: (bi, 0, p)),
                  pl.BlockSpec((1, kp, width), lambda bi, p, i: (bi, 0, p)),
                  pl.BlockSpec(u.shape, lambda bi, p, i: (0, 0))],
        out_specs=pl.BlockSpec((1, tq, width), lambda bi, p, i: (bi, i, p)),
        out_shape=jax.ShapeDtypeStruct((b, l, hd), BF16),
        scratch_shapes=[pltpu.VMEM((2 * tq, LANE), BF16), pltpu.VMEM((2 * tq, LANE), F32),
                        pltpu.VMEM((2 * tq, 1), F32), pltpu.VMEM((2, 2 * tq, tk), F32),
                        pltpu.VMEM((2, 2 * tq, tk), F32), pltpu.VMEM((2, 2 * tq, 1), F32)],
        compiler_params=_cparams(("parallel", "parallel", "arbitrary")),
        name="stick_breaking",
    )(q, k, v, u)


class _Path:
    def __init__(self, x3, mods, kv_mod, conv_state, ssm_state, past_k, past_v, p):
        self.b, self.l, self.d = x3.shape
        self.t = self.b * self.l
        self.x = x3.reshape(self.t, self.d)
        self.mods, self.kv_mod, self.p = mods, kv_mod, p
        self.conv_state, self.ssm_state, self.past_k, self.past_v = conv_state, ssm_state, past_k, past_v
        self.per_token = self.l < TOKEN_TILE
        self.rows_per_group = min(TOKEN_TILE, self.t) if self.per_token else self.l
        self.pos0 = past_k.shape[1]
        self.conv_out, self.ssm_out = [], []

    def mod3(self, v):
        if self.per_token:
            tile = min(TOKEN_TILE, self.t)
            return jnp.repeat(v, self.l, axis=0).reshape(self.t // tile, tile, self.d)
        return v[:, None, :]

    def mix_and_route(self, layer):
        p, b, l, d, t, rpg, mod3 = self.p, self.b, self.l, self.d, self.t, self.rows_per_group, self.mod3
        sh1, sc1, g1, sh2, sc2, self.g2 = jnp.split(self.mods[layer], 6, axis=-1)
        nw1 = p['mix_norm_w'][layer][None, :]
        x = self.x
        if layer < N_A_LAYERS:
            i = layer
            z, xbc, dt = norm_mod_matmul(x, nw1, mod3(sh1), mod3(sc1), [p['w_z'][i], p['w_xbc'][i], p['w_dt'][i]],
                                         [BF16, BF16, F32], rpg)
            xbc3 = xbc.reshape(b, l, CONV_DIM)
            self.conv_out.append(xbc3[:, l - (CONV_W - 1):].astype(F32))
            nc = max(l // SSD_Q, 1)
            if l < SSD_Q:
                pad = lambda a: jnp.pad(a.reshape(b, l, -1), ((0, 0), (0, SSD_Q - l), (0, 0))).reshape(b * SSD_Q, -1)
                z, xbc, dt = pad(z), pad(xbc), pad(dt)
            ic = jnp.pad(self.conv_state[i].astype(F32), ((0, 0), (8 - (CONV_W - 1), 0), (0, 0)))
            iss = self.ssm_state[i].astype(F32).transpose(0, 3, 1, 2).reshape(b, D_STATE, D_INNER)
            gm, s_fin = ssd_mixer(z, xbc, dt, p['a_conv_w'][i], p['a_conv_b'][i][None, :], p['dt_bias'][i],
                                  p['a_log'][i], p['d_x'][i], p['a_norm_w'][i][None, :], p['e_mat'], p['tri'], p['shift'],
                                  ic, iss, b, nc, min(l, SSD_Q))
            if l < SSD_Q:
                gm = gm.reshape(b, SSD_Q, D_INNER)[:, :l].reshape(t, D_INNER)
            self.ssm_out.append(s_fin.reshape(b, D_STATE, N_HEADS_A, HEAD_DIM_A).transpose(0, 2, 3, 1))
            x = matmul_residual(gm, p['a_out_proj'][i], x, mod3(g1), rpg)
        else:
            j = layer - N_A_LAYERS
            (qp,) = norm_mod_matmul(x, nw1, mod3(sh1), mod3(sc1), [p['b_w_q'][j]], [BF16], rpg)
            o = stick_breaking(qp.reshape(b, l, d), self.kb, self.vb, p['u_tri'], self.pos0, min(l, 256))
            x = matmul_residual(o.reshape(t, d), p['b_w_o'][j], x, mod3(g1), rpg)
        self.x = x
        self.moe = moe_route(x, p['ffn_norm_w'][layer][None, :], mod3(sh2), mod3(sc2), p['router_wt'][layer],
                             p['router_b'][layer], p['su_tri'], p['sl_tri'], rpg)

    def experts(self, layer):
        p = self.p
        self.ysg = moe_experts(self.moe, p['moe_w1'], p['moe_b1'], p['moe_w2'], p['moe_b2'], layer)

    def combine(self, layer):
        p, b, l, d, rpg, mod3 = self.p, self.b, self.l, self.d, self.rows_per_group, self.mod3
        self.x = moe_combine(self.ysg, self.moe['gates'], self.moe['pos'], self.x, mod3(self.g2), rpg)
        if layer == N_A_LAYERS - 1:
            sh_kv, sc_kv = jnp.split(self.kv_mod, 2, axis=-1)
            self.k_new, self.v_new = norm_mod_matmul(self.x, p['kv_norm_w'][None, :], mod3(sh_kv), mod3(sc_kv),
                                                     [p['w_k'], p['w_v']], [F32, F32], rpg)
            k_all = jnp.concatenate([self.past_k.reshape(b, self.pos0, d), self.k_new.reshape(b, l, d)], axis=1)
            v_all = jnp.concatenate([self.past_v.reshape(b, self.pos0, d), self.v_new.reshape(b, l, d)], axis=1)
            kpad = (-k_all.shape[1]) % ATTN_TK
            self.kb = jnp.pad(k_all, ((0, 0), (0, kpad), (0, 0))).astype(BF16)
            self.vb = jnp.pad(v_all, ((0, 0), (0, kpad), (0, 0))).astype(BF16)

    def finish(self):
        b, l, d = self.b, self.l, self.d
        y = final_rmsnorm(self.x, self.p['final_norm_w'][None, :])
        return (y.reshape(b, l, d), jnp.stack(self.conv_out), jnp.stack(self.ssm_out),
                self.k_new.reshape(b, l, N_HEADS_B, HEAD_DIM_B), self.v_new.reshape(b, l, N_HEADS_B, HEAD_DIM_B))


def _run_paths(paths):
    for layer in range(DEPTH):
        for step in (_Path.mix_and_route, _Path.experts, _Path.combine):
            for path in paths:
                step(path, layer)
    return [path.finish() for path in paths]


def kernel(x_prompt, x_sample, state_conv, state_ssm, cache_k, cache_v, c_prompt, c_sample, mod_w, mod_b,
           mix_norm_w, ffn_norm_w, a_in_proj, a_conv_w, a_conv_b, a_dt_bias, a_A_log, a_D, a_norm_w, a_out_proj,
           kv_mod_w, kv_mod_b, kv_norm_w, w_kv, b_w_q, b_w_o, router_w, router_b, moe_w1, moe_b1, moe_w2, moe_b2,
           final_norm_w):
    bp = x_prompt.shape[0]
    hb = N_HEADS_B * HEAD_DIM_B
    pad_heads = lambda a: jnp.pad(a, ((0, 0), (0, LANE - N_HEADS_A)))[:, None, :]
    head_of_lane = jnp.arange(D_INNER) // HEAD_DIM_A
    p = dict(
        mix_norm_w=mix_norm_w, ffn_norm_w=ffn_norm_w, kv_norm_w=kv_norm_w, final_norm_w=final_norm_w,
        w_z=a_in_proj[:, :, :D_INNER].astype(BF16),
        w_xbc=a_in_proj[:, :, D_INNER:D_INNER + CONV_DIM].astype(BF16),
        w_dt=jnp.pad(a_in_proj[:, :, D_INNER + CONV_DIM:], ((0, 0), (0, 0), (0, LANE - N_HEADS_A))).astype(BF16),
        a_conv_w=a_conv_w, a_conv_b=a_conv_b, dt_bias=pad_heads(a_dt_bias), a_log=pad_heads(a_A_log),
        d_x=a_D[:, head_of_lane][:, None, :], a_norm_w=a_norm_w, a_out_proj=a_out_proj.astype(BF16),
        e_mat=jnp.tile((jnp.arange(LANE)[:, None] == head_of_lane[None, :]).astype(BF16), (3, 1)),
        tri=jnp.tile((jnp.arange(SSD_Q)[:, None] >= jnp.arange(SSD_Q)[None, :]).astype(BF16), (1, 3)),
        shift=jnp.stack([(jnp.arange(SSD_Q)[:, None] + tap - (CONV_W - 1) == jnp.arange(SSD_Q)[None, :])
                         for tap in range(CONV_W - 1)]).astype(BF16),
        u_tri=(jnp.arange(ATTN_TK)[:, None] >= jnp.arange(ATTN_TK)[None, :]).astype(BF16),
        w_k=w_kv[:, :hb].astype(BF16), w_v=w_kv[:, hb:].astype(BF16),
        b_w_q=b_w_q.astype(BF16), b_w_o=b_w_o.astype(BF16),
        router_wt=router_w.transpose(0, 2, 1), router_b=router_b[:, :, None],
        su_tri=(jnp.arange(TOKEN_TILE)[:, None] < jnp.arange(TOKEN_TILE)[None, :]).astype(BF16),
        sl_tri=(jnp.arange(N_EXPERTS)[:, None] > jnp.arange(N_EXPERTS)[None, :]).astype(BF16),
        moe_w1=moe_w1, moe_b1=moe_b1[:, :, None, :], moe_w2=moe_w2, moe_b2=moe_b2[:, :, None, :],
    )
    c_all = jnp.concatenate([c_prompt, c_sample], axis=0)
    mods = mod_vectors(c_all, mod_w, mod_b[:, None, :])
    kv_mod = mod_vectors(c_all, kv_mod_w[None], kv_mod_b[None, None, :])[0]

    bg = bp // PROMPT_GROUPS
    conv0 = jnp.zeros((N_A_LAYERS, bg, CONV_W - 1, CONV_DIM), F32)
    ssm0 = jnp.zeros((N_A_LAYERS, bg, N_HEADS_A, HEAD_DIM_A, D_STATE), F32)
    kv0 = jnp.zeros((bg, 0, N_HEADS_B, HEAD_DIM_B), F32)
    paths = [_Path(x_prompt[g * bg:(g + 1) * bg], mods[:, g * bg:(g + 1) * bg], kv_mod[g * bg:(g + 1) * bg],
                   conv0, ssm0, kv0, kv0, p) for g in range(PROMPT_GROUPS)]
    paths.append(_Path(x_sample, mods[:, bp:], kv_mod[bp:], state_conv, state_ssm, cache_k, cache_v, p))
    *groups, out_s = _run_paths(paths)
    out_p = tuple(jnp.concatenate([grp[i] for grp in groups], axis=ax) for i, ax in enumerate((0, 1, 1, 0, 0)))
    return (out_p[0], out_s[0]) + out_p[1:] + out_s[1:]
```
